```python
import jax
import jax.numpy as jnp
from jax import lax
import numpy as np

D_MODEL = 1024
BATCH = 4
SEQ = 4096
DEPTH = 2
DEC_BATCH = 32
DEC_SEQ = 8
PAST_LEN = 16384
PAGE_SIZE = 128

HEAD_DIM = 64
MIX_WIDTH = 3 * D_MODEL // 4
MEM_WIDTH = D_MODEL - MIX_WIDTH
MEM_HEADS = MEM_WIDTH // HEAD_DIM
N_MEM = 256
N_HEADS = MIX_WIDTH // HEAD_DIM
KV_HEADS = 4
GROUP = N_HEADS // KV_HEADS
KV_WIDTH = KV_HEADS * HEAD_DIM
CONV_WIDTH = 3
CMP_BLOCK = 32
CMP_STRIDE = 16
SEL_BLOCK = 64
N_SEL = 16
WINDOW = 512
ROT_DIM = HEAD_DIM // 4
ROPE_THETA = 500000.0
NORM_EPS = 1e-6
FORCE_BONUS = 1e4
Q_BLOCK = 64
N_CONV_LAYERS = (DEPTH + 1) // 2
N_NSA_LAYERS = DEPTH // 2
CONV_SPLITS = (MEM_WIDTH, MEM_WIDTH, MIX_WIDTH, MIX_WIDTH, MIX_WIDTH, MIX_WIDTH)
NSA_SPLITS = (MEM_WIDTH, MEM_WIDTH, MIX_WIDTH) + (KV_WIDTH,) * 6 + (3 * N_HEADS, MIX_WIDTH)
CONV_IN = sum(CONV_SPLITS)
NSA_IN = sum(NSA_SPLITS)

kernel_name = 'hybrid_shortconv_nsa_memory_decoder_step'


def rmsnorm(x, g):
    x32 = x.astype(jnp.float32)
    y = x32 * lax.rsqrt(jnp.mean(x32 * x32, axis=-1, keepdims=True) + NORM_EPS)
    return (y * g.astype(jnp.float32)).astype(x.dtype)


def split_cols(p, sizes):
    return jnp.split(p, [int(v) for v in np.cumsum(sizes)[:-1]], axis=-1)


def rope_partial(x, pos):
    half = ROT_DIM // 2
    inv = ROPE_THETA ** (-jnp.arange(half, dtype=jnp.float32) * 2.0 / ROT_DIM)
    ang = pos.astype(jnp.float32)[:, None] * inv[None, :]
    shape = (ang.shape[0],) + (1,) * (x.ndim - 3) + (half,)
    cos = jnp.cos(ang).reshape(shape).astype(x.dtype)
    sin = jnp.sin(ang).reshape(shape).astype(x.dtype)
    x1 = x[..., :half]
    x2 = x[..., half:ROT_DIM]
    return jnp.concatenate([x1 * cos - x2 * sin, x2 * cos + x1 * sin, x[..., ROT_DIM:]], axis=-1)


def masked_softmax(s, mask):
    s = jnp.where(mask, s.astype(jnp.float32), -1e30)
    m = jnp.max(s, axis=-1, keepdims=True)
    p = jnp.exp(s - m) * mask
    return p / jnp.maximum(jnp.sum(p, axis=-1, keepdims=True), 1e-30)


def memory_kv(mem, g, w):
    B = mem.shape[0]
    return (rmsnorm(mem, g) @ w).reshape(B, N_MEM, 2, MEM_HEADS, HEAD_DIM)


def mem_attend(q, kv):
    B, T = q.shape[:2]
    qh = q.reshape(B, T, MEM_HEADS, HEAD_DIM)
    s = jnp.einsum('bthd,bmhd->bhtm', qh, kv[:, :, 0]) * (HEAD_DIM ** -0.5)
    p = jax.nn.softmax(s.astype(jnp.float32), axis=-1).astype(q.dtype)
    return jnp.einsum('bhtm,bmhd->bthd', p, kv[:, :, 1]).reshape(B, T, MEM_WIDTH)


def short_conv(b_g, c_g, h_in, conv_w, u_prev):
    u = c_g * h_in
    T = u.shape[1]
    u_ext = jnp.concatenate([u_prev.astype(u.dtype), u], axis=1)
    y = conv_w[0] * u_ext[:, 0:T]
    for j in range(1, CONV_WIDTH):
        y = y + conv_w[j] * u_ext[:, j:j + T]
    return b_g * y, u_ext[:, -(CONV_WIDTH - 1):]


def nsa_project(q, kc, vc, ks, vs, kw, vw, glog, pos):
    B, T = q.shape[:2]
    q = rope_partial(q.reshape(B, T, KV_HEADS, GROUP, HEAD_DIM), pos)
    heads = lambda a: a.reshape(B, T, KV_HEADS, HEAD_DIM)
    kvc = jnp.stack([rope_partial(heads(kc), pos), heads(vc)], axis=2)
    kvs = jnp.stack([rope_partial(heads(ks), pos), heads(vs)], axis=2)
    kvw = jnp.stack([rope_partial(heads(kw), pos), heads(vw)], axis=2)
    gates = jax.nn.sigmoid(glog.astype(jnp.float32)).astype(q.dtype).reshape(B, T, 3, KV_HEADS, GROUP)
    return q, kvc, kvs, kvw, gates


def chunk_partials(kv, pos_w):
    B, L = kv.shape[:2]
    nck = L // CMP_STRIDE
    ratio = CMP_BLOCK // CMP_STRIDE
    chunks = kv[:, :nck * CMP_STRIDE].reshape(B, nck, CMP_STRIDE, 2, KV_HEADS, HEAD_DIM)
    pw = pos_w.reshape(2, ratio, CMP_STRIDE, HEAD_DIM)
    return jnp.einsum('bcjsgd,smjd->mbcsgd', chunks, pw)


def blocks_from_partials(parts, phi):
    ratio, nck = parts.shape[0], parts.shape[2]
    nc = nck - ratio + 1
    blk = parts[0, :, 0:nc]
    for m in range(1, ratio):
        blk = blk + parts[m, :, m:m + nc]
    blk = jnp.einsum('bnsgd,sde->bnsge', blk, phi)
    return blk[:, :, 0], blk[:, :, 1]


def nsa_core(q, q_pos, kc_blk, vc_blk, gather_sel, n_sel_blocks, kvw, kw_pos):
    scale = HEAD_DIM ** -0.5
    dt = q.dtype
    nc = kc_blk.shape[1]
    s_c = jnp.einsum('bqgrd,bngd->bgrqn', q, kc_blk) * scale
    blk_end = jnp.arange(nc, dtype=jnp.int32) * CMP_STRIDE + (CMP_BLOCK - 1)
    p_c = masked_softmax(s_c, blk_end[None, :] <= q_pos[:, None])
    o_c = jnp.einsum('bgrqn,bngd->bqgrd', p_c.astype(dt), vc_blk)
    imp = p_c.sum(axis=2)
    ratio = CMP_BLOCK // CMP_STRIDE
    chunk = jnp.pad(imp, ((0, 0), (0, 0), (0, 0), (0, ratio - 1)))
    for m in range(1, ratio):
        chunk = chunk + jnp.pad(imp, ((0, 0), (0, 0), (0, 0), (m, ratio - 1 - m)))
    cps = SEL_BLOCK // CMP_STRIDE
    chunk = jnp.pad(chunk, ((0, 0), (0, 0), (0, 0), (0, n_sel_blocks * cps - chunk.shape[-1])))
    imp_s = chunk.reshape(chunk.shape[:3] + (n_sel_blocks, cps)).sum(-1)
    cur = q_pos // SEL_BLOCK
    j = jnp.arange(n_sel_blocks, dtype=jnp.int32)
    forced = (j[None, :] == 0) | (j[None, :] == cur[:, None]) | (j[None, :] == cur[:, None] - 1)
    score = jnp.where(j[None, :] <= cur[:, None], imp_s + jnp.where(forced, FORCE_BONUS, 0.0), -jnp.inf)
    _, idx = lax.top_k(score, min(N_SEL, n_sel_blocks))
    valid = idx <= cur[:, None]
    kv_g = gather_sel(idx)
    key_pos = idx[..., None] * SEL_BLOCK + jnp.arange(SEL_BLOCK, dtype=jnp.int32)
    mask_s = valid[..., None] & (key_pos <= q_pos[:, None, None])
    s_s = jnp.einsum('bqgrd,bgqnkd->bgrqnk', q, kv_g[..., 0, :]) * scale
    sh = s_s.shape
    p_s = masked_softmax(s_s.reshape(sh[:4] + (sh[4] * sh[5],)),
                         mask_s.reshape(sh[:2] + (1, sh[3], sh[4] * sh[5]))).reshape(sh)
    o_s = jnp.einsum('bgrqnk,bgqnkd->bqgrd', p_s.astype(dt), kv_g[..., 1, :])
    s_w = jnp.einsum('bqgrd,bkgd->bgrqk', q, kvw[:, :, 0]) * scale
    mask_w = (kw_pos[None, :] <= q_pos[:, None]) & (kw_pos[None, :] > q_pos[:, None] - WINDOW) & (kw_pos[None, :] >= 0)
    p_w = masked_softmax(s_w, mask_w)
    o_w = jnp.einsum('bgrqk,bkgd->bqgrd', p_w.astype(dt), kvw[:, :, 1])
    return o_c, o_s, o_w


def combine_nsa(gates, o_c, o_s, o_w):
    B, T = o_c.shape[:2]
    o = gates[:, :, 0, :, :, None] * o_c + gates[:, :, 1, :, :, None] * o_s + gates[:, :, 2, :, :, None] * o_w
    return o.reshape(B, T, MIX_WIDTH)


def nsa_prompt(parts, pos, cmp_pos_w, cmp_phi):
    q, kvc, kvs, kvw, gates = nsa_project(*parts, pos)
    B, T = q.shape[:2]
    kc_blk, vc_blk = blocks_from_partials(chunk_partials(kvc, cmp_pos_w), cmp_phi)
    n_sel_blocks = -(-T // SEL_BLOCK)
    sel_blocks = jnp.pad(kvs, ((0, 0), (0, n_sel_blocks * SEL_BLOCK - T), (0, 0), (0, 0), (0, 0))).reshape(
        B, n_sel_blocks, SEL_BLOCK, 2, KV_HEADS, HEAD_DIM)
    bi = jnp.arange(B)[:, None, None, None]
    gi = jnp.arange(KV_HEADS)[None, :, None, None]

    def gather_sel(idx):
        return sel_blocks[bi, idx, :, :, gi, :]

    kvw_pad = jnp.pad(kvw, ((0, 0), (WINDOW, 0), (0, 0), (0, 0), (0, 0)))
    n_qb = T // Q_BLOCK
    q_blocks = q.reshape(B, n_qb, Q_BLOCK, KV_HEADS, GROUP, HEAD_DIM).transpose(1, 0, 2, 3, 4, 5)
    starts = jnp.arange(n_qb, dtype=jnp.int32) * Q_BLOCK

    def step(args):
        qb, s = args
        q_pos = s + jnp.arange(Q_BLOCK, dtype=jnp.int32)
        kw = lax.dynamic_slice_in_dim(kvw_pad, s, WINDOW + Q_BLOCK, axis=1)
        kw_pos = s - WINDOW + jnp.arange(WINDOW + Q_BLOCK, dtype=jnp.int32)
        return nsa_core(qb, q_pos, kc_blk, vc_blk, gather_sel, n_sel_blocks, kw, kw_pos)

    o_c, o_s, o_w = lax.map(step, (q_blocks, starts))
    unblock = lambda o: o.transpose(1, 0, 2, 3, 4, 5).reshape(B, T, KV_HEADS, GROUP, HEAD_DIM)
    out = combine_nsa(gates, unblock(o_c), unblock(o_s), unblock(o_w))
    return out, kvc, kvs, kvw[:, -min(WINDOW, T):]


def nsa_sample(parts, pos, cmp_pos_w, cmp_phi, cache_cmp, cache_sel, page_table, win_state):
    q, kvc, kvs, kvw, gates = nsa_project(*parts, pos)
    Bd, T = q.shape[:2]
    past_c = cache_cmp[page_table].reshape(Bd, PAST_LEN, 2, KV_HEADS, HEAD_DIM)
    partials = jnp.concatenate([chunk_partials(past_c, cmp_pos_w), chunk_partials(kvc, cmp_pos_w)], axis=2)
    kc_blk, vc_blk = blocks_from_partials(partials, cmp_phi)
    total = PAST_LEN + T
    n_sel_blocks = -(-total // SEL_BLOCK)
    npb = PAST_LEN // SEL_BLOCK
    nnb = n_sel_blocks - npb
    bpp = PAGE_SIZE // SEL_BLOCK
    pool = cache_sel.reshape((-1, SEL_BLOCK, 2, KV_HEADS, HEAD_DIM))
    new_blocks = jnp.pad(kvs, ((0, 0), (0, nnb * SEL_BLOCK - T), (0, 0), (0, 0), (0, 0))).reshape(
        Bd, nnb, SEL_BLOCK, 2, KV_HEADS, HEAD_DIM)
    bi = jnp.arange(Bd)[:, None, None, None]
    gi = jnp.arange(KV_HEADS)[None, :, None, None]

    def gather_sel(idx):
        jp = jnp.minimum(idx, npb - 1)
        phys = page_table[bi, jp // bpp] * bpp + jp % bpp
        past = pool[phys, :, :, gi, :]
        jn = jnp.clip(idx - npb, 0, nnb - 1)
        new = new_blocks[bi, jn, :, :, gi, :]
        return jnp.where((idx < npb)[..., None, None, None], past, new)

    w_keep = win_state.shape[1]
    kw = jnp.concatenate([win_state.astype(kvw.dtype), kvw], axis=1)
    kw_pos = PAST_LEN - w_keep + jnp.arange(w_keep + T, dtype=jnp.int32)
    o_c, o_s, o_w = nsa_core(q, pos, kc_blk, vc_blk, gather_sel, n_sel_blocks, kw, kw_pos)
    return combine_nsa(gates, o_c, o_s, o_w), kvc, kvs, kw[:, -w_keep:]


def layer_update(x, mq, mg, mem_kv, mix, zg, w_out_i):
    y = jnp.concatenate([jax.nn.silu(mg) * mem_attend(mq, mem_kv), jax.nn.silu(zg) * mix], axis=-1)
    return x + y @ w_out_i


def forward_prompt(x, mem, norm_g, final_norm_g, mem_norm_g, w_mem_kv, w_in_conv, conv_w,
                   w_in_nsa, cmp_pos_w, cmp_phi, w_out):
    B, T, _ = x.shape
    pos = jnp.arange(T, dtype=jnp.int32)
    conv_st, cmp_rows, sel_rows, win_rows, mem_kvs = [], [], [], [], []
    for i in range(DEPTH):
        l = i // 2
        h = rmsnorm(x, norm_g[i])
        mem_kv = memory_kv(mem, mem_norm_g[i], w_mem_kv[i])
        mem_kvs.append(mem_kv)
        if i % 2 == 0:
            mq, mg, b_g, c_g, h_in, zg = split_cols(h @ w_in_conv[l], CONV_SPLITS)
            u_prev = jnp.zeros((B, CONV_WIDTH - 1, MIX_WIDTH), x.dtype)
            mix, st = short_conv(b_g, c_g, h_in, conv_w[l], u_prev)
            conv_st.append(st)
        else:
            parts = split_cols(h @ w_in_nsa[l], NSA_SPLITS)
            mq, mg, zg = parts[0], parts[1], parts[-1]
            mix, kvc, kvs, kvw = nsa_prompt(parts[2:-1], pos, cmp_pos_w[l], cmp_phi[l])
            cmp_rows.append(kvc)
            sel_rows.append(kvs)
            win_rows.append(kvw)
        x = layer_update(x, mq, mg, mem_kv, mix, zg, w_out[i])
    return (rmsnorm(x, final_norm_g), jnp.stack(conv_st), jnp.stack(cmp_rows), jnp.stack(sel_rows),
            jnp.stack(win_rows), jnp.stack(mem_kvs))


def forward_sample(x, cache_mem_kv, cache_cmp_kv, cache_sel_kv, page_table, state_conv, state_win_kv,
                   norm_g, final_norm_g, w_in_conv, conv_w, w_in_nsa, cmp_pos_w, cmp_phi, w_out):
    T = x.shape[1]
    pos = PAST_LEN + jnp.arange(T, dtype=jnp.int32)
    conv_st, cmp_rows, sel_rows, win_rows = [], [], [], []
    for i in range(DEPTH):
        l = i // 2
        h = rmsnorm(x, norm_g[i])
        mem_kv = cache_mem_kv[i]
        if i % 2 == 0:
            mq, mg, b_g, c_g, h_in, zg = split_cols(h @ w_in_conv[l], CONV_SPLITS)
            mix, st = short_conv(b_g, c_g, h_in, conv_w[l], state_conv[l])
            conv_st.append(st)
        else:
            parts = split_cols(h @ w_in_nsa[l], NSA_SPLITS)
            mq, mg, zg = parts[0], parts[1], parts[-1]
            mix, kvc, kvs, kvw = nsa_sample(parts[2:-1], pos, cmp_pos_w[l], cmp_phi[l], cache_cmp_kv[l],
                                            cache_sel_kv[l], page_table, state_win_kv[l])
            cmp_rows.append(kvc)
            sel_rows.append(kvs)
            win_rows.append(kvw)
        x = layer_update(x, mq, mg, mem_kv, mix, zg, w_out[i])
    return (rmsnorm(x, final_norm_g), jnp.stack(conv_st), jnp.stack(cmp_rows), jnp.stack(sel_rows),
            jnp.stack(win_rows))


def setup_inputs(seed: int = 0) -> dict:
    key = jax.random.key(seed)
    ks = jax.random.split(key, 24)
    f32 = jnp.float32
    n_pages = PAST_LEN // PAGE_SIZE
    n_used = DEC_BATCH * n_pages
    n_pool = n_used + max(1, n_used // 4)
    w_keep = min(WINDOW, PAST_LEN)

    def nrm(k, shape, s=1.0):
        return s * jax.random.normal(k, shape, f32)

    page_table = jax.random.permutation(ks[6], n_pool)[:n_used].reshape(DEC_BATCH, n_pages).astype(jnp.int32)
    return {
        'x_prompt': nrm(ks[0], (BATCH, SEQ, D_MODEL)),
        'x_sample': nrm(ks[1], (DEC_BATCH, DEC_SEQ, D_MODEL)),
        'mem_prompt': nrm(ks[2], (BATCH, N_MEM, D_MODEL)),
        'cache_mem_kv': nrm(ks[3], (DEPTH, DEC_BATCH, N_MEM, 2, MEM_HEADS, HEAD_DIM)),
        'cache_cmp_kv': nrm(ks[4], (N_NSA_LAYERS, n_pool, PAGE_SIZE, 2, KV_HEADS, HEAD_DIM)),
        'cache_sel_kv': nrm(ks[5], (N_NSA_LAYERS, n_pool, PAGE_SIZE, 2, KV_HEADS, HEAD_DIM)),
        'page_table': page_table,
        'state_conv': nrm(ks[7], (N_CONV_LAYERS, DEC_BATCH, CONV_WIDTH - 1, MIX_WIDTH)),
        'state_win_kv': nrm(ks[8], (N_NSA_LAYERS, DEC_BATCH, w_keep, 2, KV_HEADS, HEAD_DIM)),
        'norm_g': 1.0 + nrm(ks[9], (DEPTH, D_MODEL), 0.1),
        'final_norm_g': 1.0 + nrm(ks[10], (D_MODEL,), 0.1),
        'mem_norm_g': 1.0 + nrm(ks[11], (DEPTH, D_MODEL), 0.1),
        'w_mem_kv': nrm(ks[12], (DEPTH, D_MODEL, 2 * MEM_WIDTH), D_MODEL ** -0.5),
        'w_in_conv': nrm(ks[13], (N_CONV_LAYERS, D_MODEL, CONV_IN), D_MODEL ** -0.5),
        'conv_w': nrm(ks[14], (N_CONV_LAYERS, CONV_WIDTH, MIX_WIDTH), CONV_WIDTH ** -0.5),
        'w_in_nsa': nrm(ks[15], (N_NSA_LAYERS, D_MODEL, NSA_IN), D_MODEL ** -0.5),
        'cmp_pos_w': (CMP_BLOCK ** -0.5) * (1.0 + nrm(ks[16], (N_NSA_LAYERS, 2, CMP_BLOCK, HEAD_DIM), 0.2)),
        'cmp_phi': nrm(ks[17], (N_NSA_LAYERS, 2, HEAD_DIM, HEAD_DIM), HEAD_DIM ** -0.5),
        'w_out': nrm(ks[18], (DEPTH, D_MODEL, D_MODEL), D_MODEL ** -0.5),
    }


def reference(x_prompt, x_sample, mem_prompt, cache_mem_kv, cache_cmp_kv, cache_sel_kv, page_table,
              state_conv, state_win_kv, norm_g, final_norm_g, mem_norm_g, w_mem_kv, w_in_conv, conv_w,
              w_in_nsa, cmp_pos_w, cmp_phi, w_out):
    y_prompt, conv_p, cmp_p, sel_p, win_p, mem_p = forward_prompt(
        x_prompt, mem_prompt, norm_g, final_norm_g, mem_norm_g, w_mem_kv, w_in_conv, conv_w,
        w_in_nsa, cmp_pos_w, cmp_phi, w_out)
    y_sample, conv_s, cmp_s, sel_s, win_s = forward_sample(
        x_sample, cache_mem_kv, cache_cmp_kv, cache_sel_kv, page_table, state_conv, state_win_kv,
        norm_g, final_norm_g, w_in_conv, conv_w, w_in_nsa, cmp_pos_w, cmp_phi, w_out)
    return (y_prompt, y_sample, conv_p, cmp_p, sel_p, win_p, mem_p, conv_s, cmp_s, sel_s, win_s)
```

```python
import functools

import numpy as np
import jax
import jax.numpy as jnp
from jax import lax
from jax.experimental import pallas as pl
from jax.experimental.pallas import tpu as pltpu

F32 = jnp.float32
BF16 = jnp.bfloat16

D_MODEL = 1024
HEAD_DIM = 64
MIX_WIDTH = 768
MEM_WIDTH = 256
N_HEADS = 12
KV_HEADS = 4
GROUP = 3
KV_WIDTH = 256
CMP_BLOCK = 32
CMP_STRIDE = 16
SEL_BLOCK = 64
N_SEL = 16
WINDOW = 512
ROT_DIM = 16
ROPE_THETA = 500000.0
NORM_EPS = 1e-6
FORCE_BONUS = 1e4
PAGE_SIZE = 128
SCALE = HEAD_DIM ** -0.5
NEG = -(2.0 ** 100)
M_INIT = -1e30
LANES = 128
VMEM_LIMIT = 56 * 2 ** 20


def _cparams(*sem):
    return pltpu.CompilerParams(dimension_semantics=sem, vmem_limit_bytes=VMEM_LIMIT)


def _dot(a, b):
    return jnp.dot(a, b, preferred_element_type=F32)


def _dot_nt(a, b):
    return lax.dot_general(a, b, (((1,), (1,)), ((), ())), preferred_element_type=F32)


def _rms(x, g):
    return x * lax.rsqrt(jnp.mean(x * x, axis=-1, keepdims=True) + NORM_EPS) * g


def _silu(x):
    return x * jax.nn.sigmoid(x)


def _split3(a):
    hi = a.astype(BF16)
    r1 = a - hi.astype(F32)
    mid = r1.astype(BF16)
    lo = (r1 - mid.astype(F32)).astype(BF16)
    return hi, mid, lo


def _lo_half(rows):
    return lax.broadcasted_iota(jnp.int32, (rows, LANES), 1) < HEAD_DIM


def _memkv_kernel(mem_ref, g_ref, w_ref, o_ref):
    h = _rms(mem_ref[0], g_ref[0]).astype(BF16)
    o_ref[0, 0] = _dot(h, w_ref[0])


def _memory_kv(mem, mem_norm_g, w_mem_kv_b):
    depth = w_mem_kv_b.shape[0]
    b, n_mem, d = mem.shape
    return pl.pallas_call(
        _memkv_kernel,
        grid=(depth, b),
        in_specs=[pl.BlockSpec((1, n_mem, d), lambda i, j: (j, 0, 0)),
                  pl.BlockSpec((1, 1, d), lambda i, j: (i, 0, 0)),
                  pl.BlockSpec((1, d, 2 * MEM_WIDTH), lambda i, j: (i, 0, 0))],
        out_specs=pl.BlockSpec((1, 1, n_mem, 2 * MEM_WIDTH), lambda i, j: (i, j, 0, 0)),
        out_shape=jax.ShapeDtypeStruct((depth, b, n_mem, 2 * MEM_WIDTH), F32),
        compiler_params=_cparams("arbitrary", "arbitrary"),
        name="memory_kv",
    )(mem, mem_norm_g.reshape(depth, 1, d), w_mem_kv_b)


def _proj_conv_kernel(*refs, tm, seg, has_state):
    if has_state:
        x_ref, g_ref, w_ref, cw_ref, s1_ref, s2_ref, mqg_ref, mix_ref, st_ref, ubuf = refs
    else:
        x_ref, g_ref, w_ref, cw_ref, mqg_ref, mix_ref, st_ref, ubuf = refs
    i = pl.program_id(0)
    h = _rms(x_ref[...], g_ref[...]).astype(BF16)
    mqg_ref[...] = _dot(h, w_ref[:, 0:512])
    bg = _dot(h, w_ref[:, 512:1280])
    cg = _dot(h, w_ref[:, 1280:2048])
    hin = _dot(h, w_ref[:, 2048:2816])
    zg = _dot(h, w_ref[:, 2816:3584])
    u = cg * hin

    @pl.when(i == 0)
    def _():
        ubuf[0:8, :] = jnp.zeros((8, MIX_WIDTH), F32)

    @pl.when(i > 0)
    def _():
        ubuf[0:8, :] = ubuf[tm:tm + 8, :]

    ubuf[8:8 + tm, :] = u
    u1 = ubuf[7:7 + tm, :]
    u2 = ubuf[6:6 + tm, :]
    rowpos = lax.rem(i * tm + lax.broadcasted_iota(jnp.int32, (tm, 1), 0), seg)
    if has_state:
        u1 = jnp.where(rowpos >= 1, u1, s1_ref[...])
        u2 = jnp.where(rowpos >= 2, u2, s2_ref[...])
        st_ref[...] = u
    else:
        u1 = jnp.where(rowpos >= 1, u1, 0.0)
        u2 = jnp.where(rowpos >= 2, u2, 0.0)
        st_ref[0] = ubuf[8 + tm - 2:8 + tm, :]
    y = cw_ref[0:1, :] * u2 + cw_ref[1:2, :] * u1 + cw_ref[2:3, :] * u
    mix_ref[...] = (_silu(zg) * (bg * y)).astype(mix_ref.dtype)


def _proj_conv(x2d, g, w_b, cw, seg, tm, state=None):
    m = x2d.shape[0]
    n_in = w_b.shape[1]
    has_state = state is not None
    in_specs = [pl.BlockSpec((tm, D_MODEL), lambda i: (i, 0)),
                pl.BlockSpec((1, D_MODEL), lambda i: (0, 0)),
                pl.BlockSpec((D_MODEL, n_in), lambda i: (0, 0)),
                pl.BlockSpec((3, MIX_WIDTH), lambda i: (0, 0))]
    args = [x2d, g.reshape(1, D_MODEL), w_b, cw]
    if has_state:
        in_specs += [pl.BlockSpec((tm, MIX_WIDTH), lambda i: (i, 0))] * 2
        args += list(state)
        st_shape = jax.ShapeDtypeStruct((m, MIX_WIDTH), F32)
        st_spec = pl.BlockSpec((tm, MIX_WIDTH), lambda i: (i, 0))
    else:
        st_shape = jax.ShapeDtypeStruct((m // seg, 2, MIX_WIDTH), F32)
        st_spec = pl.BlockSpec((1, 2, MIX_WIDTH), lambda i: ((i * tm) // seg, 0, 0))
    return pl.pallas_call(
        functools.partial(_proj_conv_kernel, tm=tm, seg=seg, has_state=has_state),
        grid=(m // tm,),
        in_specs=in_specs,
        out_specs=[pl.BlockSpec((tm, 512), lambda i: (i, 0)),
                   pl.BlockSpec((tm, MIX_WIDTH), lambda i: (i, 0)),
                   st_spec],
        out_shape=[jax.ShapeDtypeStruct((m, 512), F32),
                   jax.ShapeDtypeStruct((m, MIX_WIDTH), F32),
                   st_shape],
        scratch_shapes=[pltpu.VMEM((tm + 8, MIX_WIDTH), F32)],
        compiler_params=_cparams("arbitrary"),
        name="proj_conv",
    )(*args)


def _out_kernel(*refs, nseg, seg, final):
    if final:
        x_ref, mqg_ref, memkv_ref, mix_ref, w_ref, g_ref, o_ref, mo_sc = refs
    else:
        x_ref, mqg_ref, memkv_ref, mix_ref, w_ref, o_ref, mo_sc = refs
    lo = _lo_half(seg)

    def seg_body(s, r0):
        for pair in range(2):
            cols = slice(pair * LANES, (pair + 1) * LANES)
            qp = mqg_ref[pl.ds(r0, seg), cols]
            kp = memkv_ref[s, :, cols].astype(BF16)
            vp = memkv_ref[s, :, MEM_WIDTH + pair * LANES:MEM_WIDTH + (pair + 1) * LANES].astype(BF16)
            outs = []
            for half in range(2):
                keep = lo if half == 0 else jnp.logical_not(lo)
                qm = jnp.where(keep, qp, 0.0).astype(BF16)
                sc = _dot_nt(qm, kp) * SCALE
                e = jnp.exp(sc - jnp.max(sc, axis=-1, keepdims=True))
                p = e / jnp.sum(e, axis=-1, keepdims=True)
                outs.append(_dot(p.astype(BF16), vp))
            mo_sc[pl.ds(r0, seg), cols] = jnp.where(lo, outs[0], outs[1])

    if nseg == 1:
        seg_body(0, 0)
    else:
        def body(s, c):
            seg_body(s, pl.multiple_of(s * seg, seg))
            return c
        lax.fori_loop(0, nseg, body, 0)

    mg = mqg_ref[:, MEM_WIDTH:2 * MEM_WIDTH]
    a = (_silu(mg) * mo_sc[...]).astype(BF16)
    xn = x_ref[...] + (_dot(a, w_ref[0:MEM_WIDTH, :]) + _dot(mix_ref[...].astype(BF16), w_ref[MEM_WIDTH:, :]))
    if final:
        o_ref[...] = _rms(xn, g_ref[...])
    else:
        o_ref[...] = xn


def _layer_out(x2d, mqg, memkv, mix, w_out_b, tm, seg, final_g=None):
    m = x2d.shape[0]
    nseg = max(tm // seg, 1)
    seg_in = min(seg, tm)
    n_mem = memkv.shape[1]
    final = final_g is not None
    in_specs = [pl.BlockSpec((tm, D_MODEL), lambda i: (i, 0)),
                pl.BlockSpec((tm, 512), lambda i: (i, 0)),
                pl.BlockSpec((nseg, n_mem, 2 * MEM_WIDTH), lambda i: ((i * tm) // (seg * nseg), 0, 0)),
                pl.BlockSpec((tm, MIX_WIDTH), lambda i: (i, 0)),
                pl.BlockSpec((D_MODEL, D_MODEL), lambda i: (0, 0))]
    args = [x2d, mqg, memkv, mix, w_out_b]
    if final:
        in_specs.append(pl.BlockSpec((1, D_MODEL), lambda i: (0, 0)))
        args.append(final_g.reshape(1, D_MODEL))
    return pl.pallas_call(
        functools.partial(_out_kernel, nseg=nseg, seg=seg_in, final=final),
        grid=(m // tm,),
        in_specs=in_specs,
        out_specs=pl.BlockSpec((tm, D_MODEL), lambda i: (i, 0)),
        out_shape=jax.ShapeDtypeStruct((m, D_MODEL), F32),
        scratch_shapes=[pltpu.VMEM((tm, MEM_WIDTH), F32)],
        compiler_params=_cparams("arbitrary"),
        name="layer_out",
    )(*args)


_C_Q, _C_KC, _C_KS, _C_KW, _C_ZG, _C_GL, _C_END = 512, 1280, 1792, 2304, 2816, 3584, 3712


def _proj_nsa_kernel(*refs, tm, prompt):
    (x_ref, g_ref, w_ref, c_ref, sa_ref, sb_ref,
     mqg_ref, q_ref, kvc_ref, kvs_ref, kvw_ref, gates_ref, zgs_ref) = refs[:13]
    ti = pl.program_id(1)
    h = _rms(x_ref[0], g_ref[...]).astype(BF16)
    cos, sa, sb = c_ref[...], sa_ref[...], sb_ref[...]
    lo = _lo_half(tm)

    def rope(chunk):
        return chunk * cos + pltpu.roll(chunk, LANES - ROT_DIM // 2, axis=1) * sa + pltpu.roll(chunk, ROT_DIM // 2, axis=1) * sb

    mqg_ref[0] = _dot(h, w_ref[:, 0:_C_Q])
    qf = _dot(h, w_ref[:, _C_Q:_C_KC])
    for c in range(N_HEADS // 2):
        qc = rope(qf[:, c * LANES:(c + 1) * LANES]) * SCALE
        q_ref[0, 2 * c] = jnp.where(lo, qc, 0.0).astype(q_ref.dtype)
        q_ref[0, 2 * c + 1] = jnp.where(lo, pltpu.roll(qc, HEAD_DIM, axis=1), 0.0).astype(q_ref.dtype)

    if prompt:
        ksa_ref, kwa_ref, vsb_ref, vwb_ref = refs[13:17]
        pos = ti * tm + lax.broadcasted_iota(jnp.int32, (tm, LANES), 0)
        lane = lax.broadcasted_iota(jnp.int32, (tm, LANES), 1)
        onehot = jnp.where(jnp.right_shift(pos, 6) == lane - HEAD_DIM, 1.0, 0.0)

    for name, off in (("c", _C_KC), ("s", _C_KS), ("w", _C_KW)):
        kv_ref = {"c": kvc_ref, "s": kvs_ref, "w": kvw_ref}[name]
        kk = _dot(h, w_ref[:, off:off + KV_WIDTH])
        vv = _dot(h, w_ref[:, off + KV_WIDTH:off + 2 * KV_WIDTH])
        kv_ref[0, :, KV_WIDTH:2 * KV_WIDTH] = vv
        for c in range(KV_HEADS // 2):
            kr = rope(kk[:, c * LANES:(c + 1) * LANES])
            kv_ref[0, :, c * LANES:(c + 1) * LANES] = kr
            if prompt and name != "c":
                pad = onehot if name == "s" else 0.0
                aug_ref = ksa_ref if name == "s" else kwa_ref
                aug_ref[0, 2 * c] = jnp.where(lo, kr, pad).astype(BF16)
                aug_ref[0, 2 * c + 1] = jnp.where(lo, pltpu.roll(kr, HEAD_DIM, axis=1), pad).astype(BF16)
        if prompt and name == "s":
            vsb_ref[0] = vv.astype(BF16)
        if prompt and name == "w":
            vwb_ref[0] = vv.astype(BF16)

    gates_ref[0] = jax.nn.sigmoid(_dot(h, w_ref[:, _C_GL:_C_END]))
    zgs_ref[0] = _silu(_dot(h, w_ref[:, _C_ZG:_C_GL]))


def _proj_nsa(x3d, g, w_b, tabs, tm, prompt):
    nb, t, _ = x3d.shape
    row = lambda w: pl.BlockSpec((1, tm, w), lambda b, i: (b, i, 0))
    head = lambda n: pl.BlockSpec((1, n, tm, LANES), lambda b, i: (b, 0, i, 0))
    tab = pl.BlockSpec((tm, LANES), lambda b, i: (i, 0))
    sds = jax.ShapeDtypeStruct
    out_specs = [row(512), head(N_HEADS), row(512), row(512), row(512), row(LANES), row(MIX_WIDTH)]
    out_shape = [sds((nb, t, 512), F32), sds((nb, N_HEADS, t, LANES), BF16 if prompt else F32),
                 sds((nb, t, 512), F32), sds((nb, t, 512), F32), sds((nb, t, 512), F32),
                 sds((nb, t, LANES), F32), sds((nb, t, MIX_WIDTH), F32)]
    if prompt:
        out_specs += [head(KV_HEADS), head(KV_HEADS), row(KV_WIDTH), row(KV_WIDTH)]
        out_shape += [sds((nb, KV_HEADS, t, LANES), BF16), sds((nb, KV_HEADS, t, LANES), BF16),
                      sds((nb, t, KV_WIDTH), BF16), sds((nb, t, KV_WIDTH), BF16)]
    return pl.pallas_call(
        functools.partial(_proj_nsa_kernel, tm=tm, prompt=prompt),
        grid=(nb, t // tm),
        in_specs=[pl.BlockSpec((1, tm, D_MODEL), lambda b, i: (b, i, 0)),
                  pl.BlockSpec((1, D_MODEL), lambda b, i: (0, 0)),
                  pl.BlockSpec((D_MODEL, _C_END), lambda b, i: (0, 0)),
                  tab, tab, tab],
        out_specs=out_specs,
        out_shape=out_shape,
        compiler_params=_cparams("arbitrary", "arbitrary"),
        name="proj_nsa",
    )(x3d, g.reshape(1, D_MODEL), w_b, *tabs)


def _rope_tables(pos):
    half = ROT_DIM // 2
    inv = ROPE_THETA ** (-jnp.arange(half, dtype=F32) * 2.0 / ROT_DIM)
    ang = pos.astype(F32)[:, None] * inv[None, :]
    cos, sin = jnp.cos(ang), jnp.sin(ang)
    n = pos.shape[0]
    one = jnp.ones((n, HEAD_DIM - ROT_DIM), F32)
    zero = jnp.zeros((n, HEAD_DIM - ROT_DIM), F32)
    zh = jnp.zeros((n, half), F32)
    c = jnp.concatenate([cos, cos, one], axis=1)
    sa = jnp.concatenate([-sin, zh, zero], axis=1)
    sb = jnp.concatenate([zh, sin, zero], axis=1)
    return tuple(jnp.tile(a, (1, LANES // HEAD_DIM)) for a in (c, sa, sb))


def _emit_cmp_blocks(out, kca_ref, vcb_ref):
    n = out.shape[0]
    lo = _lo_half(n)
    for c in range(KV_HEADS // 2):
        kc = out[:, c * LANES:(c + 1) * LANES]
        kca_ref[0, 2 * c] = jnp.where(lo, kc, 0.0).astype(BF16)
        kca_ref[0, 2 * c + 1] = jnp.where(lo, pltpu.roll(kc, HEAD_DIM, axis=1), 0.0).astype(BF16)
    vcb_ref[0] = out[:, KV_WIDTH:2 * KV_WIDTH].astype(BF16)


def _cmp_prompt_kernel(kvc_ref, pw0_ref, pw1_ref, phi_ref, kca_ref, vcb_ref, p0_sc, p1_sc, *, rows):
    t = kvc_ref.shape[1]
    cpr = rows // CMP_STRIDE
    pw0 = pw0_ref[...][None]
    pw1 = pw1_ref[...][None]
    for i in range(t // rows):
        x = kvc_ref[0, i * rows:(i + 1) * rows, :].reshape(cpr, CMP_STRIDE, 2 * KV_WIDTH)
        p0_sc[i * cpr:(i + 1) * cpr, :] = jnp.sum(x * pw0, axis=1)
        p1_sc[i * cpr:(i + 1) * cpr, :] = jnp.sum(x * pw1, axis=1)
    nck = t // CMP_STRIDE
    blk = p0_sc[...] + pltpu.roll(p1_sc[...], nck - 1, axis=0)
    _emit_cmp_blocks(_dot(blk.astype(BF16), phi_ref[...]), kca_ref, vcb_ref)


def _cmp_prompt(kvc, pw0, pw1, phi_bd):
    b, t, _ = kvc.shape
    nck = t // CMP_STRIDE
    full = lambda s: pl.BlockSpec(s, lambda i: (0,) * len(s))
    return pl.pallas_call(
        functools.partial(_cmp_prompt_kernel, rows=512),
        grid=(b,),
        in_specs=[pl.BlockSpec((1, t, 2 * KV_WIDTH), lambda i: (i, 0, 0)),
                  full((CMP_STRIDE, 2 * KV_WIDTH)), full((CMP_STRIDE, 2 * KV_WIDTH)),
                  full((2 * KV_WIDTH, 2 * KV_WIDTH))],
        out_specs=[pl.BlockSpec((1, KV_HEADS, nck, LANES), lambda i: (i, 0, 0, 0)),
                   pl.BlockSpec((1, nck, KV_WIDTH), lambda i: (i, 0, 0))],
        out_shape=[jax.ShapeDtypeStruct((b, KV_HEADS, nck, LANES), BF16),
                   jax.ShapeDtypeStruct((b, nck, KV_WIDTH), BF16)],
        scratch_shapes=[pltpu.VMEM((nck, 2 * KV_WIDTH), F32), pltpu.VMEM((nck, 2 * KV_WIDTH), F32)],
        compiler_params=_cparams("arbitrary"),
        name="cmp_prompt",
    )(kvc, pw0, pw1, phi_bd)


def _cmp_sample_kernel(pt_ref, pw0_ref, pw1_ref, phi_ref, *refs, pg):
    pages = refs[:pg]
    kca_ref, vcb_ref, p0_sc, p1_sc = refs[pg:]
    step = pl.program_id(1)
    cpp = PAGE_SIZE // CMP_STRIDE
    pw0 = pw0_ref[...][None]
    pw1 = pw1_ref[...][None]
    for k in range(pg):
        x = pages[k][0].reshape(cpp, CMP_STRIDE, 2 * KV_WIDTH)
        r0 = pl.multiple_of((step * pg + k) * cpp, cpp)
        p0_sc[pl.ds(r0, cpp), :] = jnp.sum(x * pw0, axis=1)
        p1_sc[pl.ds(r0, cpp), :] = jnp.sum(x * pw1, axis=1)

    @pl.when(step == pl.num_programs(1) - 1)
    def _():
        nck = p0_sc.shape[0]
        blk = p0_sc[...] + pltpu.roll(p1_sc[...], nck - 1, axis=0)
        _emit_cmp_blocks(_dot(blk.astype(BF16), phi_ref[...]), kca_ref, vcb_ref)


def _page_specs(pg):
    return [pl.BlockSpec((1, PAGE_SIZE, 2 * KV_WIDTH), functools.partial(
        lambda b, s, pt, k: (pt[b, s * pg + k], 0, 0), k=k)) for k in range(pg)]


def _cmp_sample(cache, page_table, pw0, pw1, phi_bd, pg):
    bd, n_pages = page_table.shape
    nck = n_pages * PAGE_SIZE // CMP_STRIDE
    full = lambda s: pl.BlockSpec(s, lambda b, i, pt: (0,) * len(s))
    grid_spec = pltpu.PrefetchScalarGridSpec(
        num_scalar_prefetch=1,
        grid=(bd, n_pages // pg),
        in_specs=[full((CMP_STRIDE, 2 * KV_WIDTH)), full((CMP_STRIDE, 2 * KV_WIDTH)),
                  full((2 * KV_WIDTH, 2 * KV_WIDTH))] + _page_specs(pg),
        out_specs=[pl.BlockSpec((1, KV_HEADS, nck, LANES), lambda b, i, pt: (b, 0, 0, 0)),
                   pl.BlockSpec((1, nck, KV_WIDTH), lambda b, i, pt: (b, 0, 0))],
        scratch_shapes=[pltpu.VMEM((nck, 2 * KV_WIDTH), F32), pltpu.VMEM((nck, 2 * KV_WIDTH), F32)],
    )
    return pl.pallas_call(
        functools.partial(_cmp_sample_kernel, pg=pg),
        grid_spec=grid_spec,
        out_shape=[jax.ShapeDtypeStruct((bd, KV_HEADS, nck, LANES), BF16),
                   jax.ShapeDtypeStruct((bd, nck, KV_WIDTH), BF16)],
        compiler_params=_cparams("arbitrary", "arbitrary"),
        name="cmp_sample",
    )(page_table, pw0, pw1, phi_bd, *([cache] * pg))


def _imp_matrix(nck, n_sel_blocks):
    cps = SEL_BLOCK // CMP_STRIDE
    c = np.arange(nck)[:, None]
    j = np.arange(n_sel_blocks)[None, :]
    m = ((c >= cps * j) & (c <= cps * j + cps - 1)).astype(np.float32)
    m += ((c + 1 >= cps * j) & (c + 1 <= cps * j + cps - 1)).astype(np.float32)
    return m


def _softmax_masked(s, mask):
    s = jnp.where(mask, s, -1e30)
    p = jnp.exp(s - jnp.max(s, axis=-1, keepdims=True)) * mask.astype(F32)
    return p / jnp.maximum(jnp.sum(p, axis=-1, keepdims=True), 1e-30)


def _online_update(s, v, m_sc, l_sc, acc_sc):
    m_old = m_sc[...]
    m_new = jnp.maximum(m_old, jnp.max(s, axis=-1, keepdims=True))
    alpha = jnp.exp(m_old - m_new)
    p = jnp.exp(s - m_new)
    l_sc[...] = alpha * l_sc[...] + jnp.sum(p, axis=-1, keepdims=True)
    acc_sc[...] = alpha * acc_sc[...] + _dot(p.astype(BF16), v)
    m_sc[...] = m_new


def _reset_online(m_sc, l_sc, acc_sc):
    m_sc[...] = jnp.full(m_sc.shape, M_INIT, F32)
    l_sc[...] = jnp.zeros(l_sc.shape, F32)
    acc_sc[...] = jnp.zeros(acc_sc.shape, F32)


def _attn_prompt_kernel(q_ref, kca_ref, vcb_ref, ksa_ref, vsb_ref, kwa_ref, vwb_ref, gates_ref, zgs_ref,
                        m2t_ref, o_ref, m_sc, l_sc, acc_sc, *, tq, n_sel):
    qi = pl.program_id(1)
    s0 = qi * tq
    rows = GROUP * tq
    nsb = LANES - HEAD_DIM
    qpos = s0 + lax.broadcasted_iota(jnp.int32, (tq, 1), 0)
    qpos3 = jnp.concatenate([qpos] * GROUP, axis=0)
    lo = _lo_half(tq)
    gates = gates_ref[0]
    placed = [None] * N_HEADS
    for g in range(KV_HEADS):
        pair, half = g // 2, g % 2
        vcols = slice(pair * LANES, (pair + 1) * LANES)
        qc = jnp.concatenate([q_ref[0, GROUP * g + r] for r in range(GROUP)], axis=0)

        sc = _dot_nt(qc, kca_ref[0, g])
        ncp = sc.shape[1]
        blk_end = lax.broadcasted_iota(jnp.int32, (1, ncp), 1) * CMP_STRIDE + (CMP_BLOCK - 1)
        pc = _softmax_masked(sc, blk_end <= qpos3)
        oc = _dot(pc.astype(BF16), vcb_ref[0, :, vcols])
        imp = pc[0:tq] + pc[tq:2 * tq] + pc[2 * tq:3 * tq]

        imp_t = None
        for part in _split3(imp):
            d = _dot_nt(m2t_ref[...], part)
            imp_t = d if imp_t is None else imp_t + d
        imp_t = imp_t[HEAD_DIM:, :]
        j = lax.broadcasted_iota(jnp.int32, (nsb, tq), 0)
        cur = jnp.right_shift(s0 + lax.broadcasted_iota(jnp.int32, (nsb, tq), 1), 6)
        valid = j <= cur
        forced = (j == 0) | (j == cur) | (j == cur - 1)
        score = jnp.where(valid, imp_t + jnp.where(forced, FORCE_BONUS, 0.0), -jnp.inf)
        cnt = jnp.zeros((nsb, tq), F32)
        for i in range(nsb):
            row = score[i:i + 1, :]
            tie = jnp.where(j > i, 1.0, 0.0)
            cnt = cnt + jnp.where(row > score, 1.0, jnp.where(row == score, tie, 0.0))
        sel = (cnt < float(n_sel)) & valid
        pen_t = jnp.concatenate([jnp.zeros((HEAD_DIM, tq), F32), jnp.where(sel, 0.0, NEG)], axis=0)
        pen = pen_t.T.astype(BF16)
        qa = qc + jnp.concatenate([pen] * GROUP, axis=0)

        _reset_online(m_sc, l_sc, acc_sc)

        def sel_tile(kt, causal):
            k0 = pl.multiple_of(kt * tq, tq)
            s = _dot_nt(qa, ksa_ref[0, g, pl.ds(k0, tq), :])
            if causal:
                kpos = k0 + lax.broadcasted_iota(jnp.int32, (1, tq), 1)
                s = jnp.where(kpos <= qpos3, s, NEG)
            _online_update(s, vsb_ref[0, pl.ds(k0, tq), vcols], m_sc, l_sc, acc_sc)

        def sel_body(kt, c):
            sel_tile(kt, False)
            return c

        lax.fori_loop(0, qi, sel_body, 0)
        sel_tile(qi, True)
        o_s = acc_sc[...] / l_sc[...]

        _reset_online(m_sc, l_sc, acc_sc)

        def win_body(kt, c):
            k0 = pl.multiple_of(kt * tq, tq)
            s = _dot_nt(qc, kwa_ref[0, g, pl.ds(k0, tq), :])
            kpos = k0 + lax.broadcasted_iota(jnp.int32, (1, tq), 1)
            s = jnp.where((kpos <= qpos3) & (kpos > qpos3 - WINDOW), s, NEG)
            _online_update(s, vwb_ref[0, pl.ds(k0, tq), vcols], m_sc, l_sc, acc_sc)
            return c

        lax.fori_loop(jnp.maximum(qi - WINDOW // tq, 0), qi + 1, win_body, 0)
        o_w = acc_sc[...] / l_sc[...]

        for r in range(GROUP):
            hd = GROUP * g + r
            rs = slice(r * tq, (r + 1) * tq)
            mix = (gates[:, hd:hd + 1] * oc[rs] + gates[:, N_HEADS + hd:N_HEADS + hd + 1] * o_s[rs]
                   + gates[:, 2 * N_HEADS + hd:2 * N_HEADS + hd + 1] * o_w[rs])
            placed[hd] = mix if half == hd % 2 else pltpu.roll(mix, HEAD_DIM, axis=1)
    for c in range(N_HEADS // 2):
        cols = slice(c * LANES, (c + 1) * LANES)
        o_ref[0, :, cols] = (zgs_ref[0, :, cols] * jnp.where(lo, placed[2 * c], placed[2 * c + 1])).astype(o_ref.dtype)


def _attn_prompt(q, kca, vcb, ksa, vsb, kwa, vwb, gates, zgs, tq):
    b, _, t, _ = q.shape
    nck = kca.shape[2]
    nsb = -(-t // SEL_BLOCK)
    assert nsb <= LANES - HEAD_DIM
    m2t = np.zeros((LANES, nck), np.float32)
    m2t[HEAD_DIM:HEAD_DIM + nsb] = _imp_matrix(nck, nsb).T
    per_b4 = lambda n, r: pl.BlockSpec((1, n, r, LANES), lambda i, j: (i, 0, 0, 0))
    per_b3 = lambda r, w: pl.BlockSpec((1, r, w), lambda i, j: (i, 0, 0))
    rows = GROUP * tq
    return pl.pallas_call(
        functools.partial(_attn_prompt_kernel, tq=tq, n_sel=min(N_SEL, nsb)),
        grid=(b, t // tq),
        in_specs=[pl.BlockSpec((1, N_HEADS, tq, LANES), lambda i, j: (i, 0, j, 0)),
                  per_b4(KV_HEADS, nck), per_b3(nck, KV_WIDTH),
                  per_b4(KV_HEADS, t), per_b3(t, KV_WIDTH),
                  per_b4(KV_HEADS, t), per_b3(t, KV_WIDTH),
                  pl.BlockSpec((1, tq, LANES), lambda i, j: (i, j, 0)),
                  pl.BlockSpec((1, tq, MIX_WIDTH), lambda i, j: (i, j, 0)),
                  pl.BlockSpec(m2t.shape, lambda i, j: (0, 0))],
        out_specs=pl.BlockSpec((1, tq, MIX_WIDTH), lambda i, j: (i, j, 0)),
        out_shape=jax.ShapeDtypeStruct((b, t, MIX_WIDTH), F32),
        scratch_shapes=[pltpu.VMEM((rows, 1), F32), pltpu.VMEM((rows, 1), F32), pltpu.VMEM((rows, LANES), F32)],
        compiler_params=_cparams("arbitrary", "arbitrary"),
        name="attn_prompt",
    )(q, kca, vcb, ksa, vsb, kwa, vwb, gates, zgs, jnp.asarray(m2t, BF16))


def _attn_sample_kernel(pt_ref, q_ref, kca_ref, vcb_ref, ksn_ref, kwn_ref, win_ref, gates_ref, zgs_ref, mmat_ref,
                        *refs, pg, past_len):
    pages = refs[:pg]
    o_ref, qp_sc, m_sc, l_sc, acc_sc, oc_sc, ow_sc = refs[pg:]
    step = pl.program_id(1)
    t = q_ref.shape[2]
    rows = N_HEADS * t
    rpad = qp_sc.shape[0]
    npb = mmat_ref.shape[1]
    tok = jnp.bitwise_and(lax.broadcasted_iota(jnp.int32, (rpad, 1), 0), t - 1)
    qpos = past_len + tok

    @pl.when(step == 0)
    def _():
        zero_t = jnp.zeros((t, LANES), F32)
        qrows = []
        for hd in range(N_HEADS):
            g = hd // GROUP
            qh = q_ref[0, hd]
            if g % 2 == 1:
                qh = pltpu.roll(qh, HEAD_DIM, axis=1)
            qrows.append(jnp.concatenate([qh, zero_t] if g // 2 == 0 else [zero_t, qh], axis=1))
        qbd = jnp.concatenate(qrows + [jnp.zeros((rpad - rows, 2 * LANES), F32)], axis=0).astype(BF16)

        imps = []
        rg = GROUP * t
        qpos_g = qpos[0:rg]
        for g in range(KV_HEADS):
            qg = jnp.concatenate([q_ref[0, GROUP * g + r] for r in range(GROUP)], axis=0).astype(BF16)
            sc = _dot_nt(qg, kca_ref[0, g])
            ncp = sc.shape[1]
            blk_end = lax.broadcasted_iota(jnp.int32, (1, ncp), 1) * CMP_STRIDE + (CMP_BLOCK - 1)
            pc = _softmax_masked(sc, blk_end <= qpos_g)
            oc_sc[g * rg:(g + 1) * rg, :] = _dot(pc.astype(BF16), vcb_ref[0])
            imps.append(pc[0:t] + pc[t:2 * t] + pc[2 * t:3 * t])
        oc_sc[rows:, :] = jnp.zeros((rpad - rows, 2 * LANES), F32)
        imp = jnp.concatenate(imps, axis=0)
        imp_s = None
        for part in _split3(imp):
            d = _dot(part, mmat_ref[...])
            imp_s = d if imp_s is None else imp_s + d
        ngt = KV_HEADS * t
        width = npb + LANES
        base = jnp.concatenate([imp_s, jnp.zeros((ngt, LANES), F32)], axis=1)
        j = lax.broadcasted_iota(jnp.int32, (ngt, width), 1)
        cur = jnp.right_shift(past_len + jnp.bitwise_and(lax.broadcasted_iota(jnp.int32, (ngt, width), 0), t - 1), 6)
        valid = j <= cur
        forced = (j == 0) | (j == cur) | (j == cur - 1)
        score = jnp.where(valid, base + jnp.where(forced, FORCE_BONUS, 0.0), -jnp.inf)
        picked = jnp.zeros((ngt, width), F32)
        jf = j.astype(F32)
        for _ in range(N_SEL):
            mx = jnp.max(score, axis=-1, keepdims=True)
            idx = jnp.min(jnp.where(score == mx, jf, float(width)), axis=-1, keepdims=True)
            hit = jf == idx
            picked = jnp.where(hit, 1.0, picked)
            score = jnp.where(hit, -jnp.inf, score)
        pen_gt = jnp.where((picked > 0.5) & valid, 0.0, NEG)
        pen = jnp.concatenate([pen_gt[(hd // GROUP) * t:(hd // GROUP + 1) * t] for hd in range(N_HEADS)]
                              + [jnp.zeros((rpad - rows, width), F32)], axis=0)
        qp_sc[:, 0:2 * LANES] = qbd
        qp_sc[:, 2 * LANES:] = pen[:, 0:npb].astype(BF16)
        pen_new = pen[:, npb:npb + 1]

        def padded(ref, cols):
            return jnp.concatenate([ref[0, :, cols], jnp.zeros((LANES - t, KV_WIDTH), F32)], axis=0).astype(BF16)

        kcols, vcols = slice(0, KV_WIDTH), slice(KV_WIDTH, 2 * KV_WIDTH)
        inew = lax.broadcasted_iota(jnp.int32, (1, LANES), 1)
        new_ok = (inew < t) & (past_len + inew <= qpos)

        s = jnp.where(new_ok, _dot_nt(qbd, padded(ksn_ref, kcols)) + pen_new, NEG)
        m0 = jnp.maximum(jnp.max(s, axis=-1, keepdims=True), M_INIT)
        p = jnp.exp(s - m0)
        m_sc[...] = m0
        l_sc[...] = jnp.sum(p, axis=-1, keepdims=True)
        acc_sc[...] = _dot(p.astype(BF16), padded(ksn_ref, vcols))

        wk = win_ref.shape[1]
        kw_pos = past_len - wk + lax.broadcasted_iota(jnp.int32, (1, wk), 1)
        mask1 = (kw_pos <= qpos) & (kw_pos > qpos - WINDOW) & (kw_pos >= 0)
        mask2 = new_ok & (past_len + inew > qpos - WINDOW)
        s1 = jnp.where(mask1, _dot_nt(qbd, win_ref[0, :, kcols].astype(BF16)), -1e30)
        s2 = jnp.where(mask2, _dot_nt(qbd, padded(kwn_ref, kcols)), -1e30)
        mw = jnp.maximum(jnp.max(s1, axis=-1, keepdims=True), jnp.max(s2, axis=-1, keepdims=True))
        p1 = jnp.exp(s1 - mw) * mask1.astype(F32)
        p2 = jnp.exp(s2 - mw) * mask2.astype(F32)
        lw = jnp.sum(p1, axis=-1, keepdims=True) + jnp.sum(p2, axis=-1, keepdims=True)
        ow = _dot(p1.astype(BF16), win_ref[0, :, vcols].astype(BF16)) + _dot(p2.astype(BF16), padded(kwn_ref, vcols))
        ow_sc[...] = ow / jnp.maximum(lw, 1e-30)

    bpp = PAGE_SIZE // SEL_BLOCK
    jj = lax.broadcasted_iota(jnp.int32, (PAGE_SIZE, npb), 1)
    kblk = jnp.right_shift(lax.broadcasted_iota(jnp.int32, (PAGE_SIZE, npb), 0), 6)
    for k in range(pg):
        page = pages[k][0]
        pidx = step * pg + k
        onehot = jnp.where(jj == pidx * bpp + kblk, 1.0, 0.0).astype(BF16)
        ke = jnp.concatenate([page[:, 0:KV_WIDTH].astype(BF16), onehot], axis=1)
        s = _dot_nt(qp_sc[...], ke)
        _online_update(s, page[:, KV_WIDTH:].astype(BF16), m_sc, l_sc, acc_sc)

    @pl.when(step == pl.num_programs(1) - 1)
    def _():
        o_s = acc_sc[...] / l_sc[...]
        lo = _lo_half(t)
        gates = gates_ref[0]
        placed = []
        for hd in range(N_HEADS):
            g = hd // GROUP
            rs = slice(hd * t, (hd + 1) * t)
            cols = slice((g // 2) * LANES, (g // 2 + 1) * LANES)
            mix = (gates[:, hd:hd + 1] * oc_sc[rs, cols] + gates[:, N_HEADS + hd:N_HEADS + hd + 1] * o_s[rs, cols]
                   + gates[:, 2 * N_HEADS + hd:2 * N_HEADS + hd + 1] * ow_sc[rs, cols])
            placed.append(mix if g % 2 == hd % 2 else pltpu.roll(mix, HEAD_DIM, axis=1))
        for c in range(N_HEADS // 2):
            cols = slice(c * LANES, (c + 1) * LANES)
            o_ref[0, :, cols] = zgs_ref[0, :, cols] * jnp.where(lo, placed[2 * c], placed[2 * c + 1])


def _attn_sample(q, kca, vcb, kvs_new, kvw_new, win_state, gates, zgs, cache_sel, page_table, t, pg):
    bd, n_pages = page_table.shape
    past_len = n_pages * PAGE_SIZE
    nck = kca.shape[2]
    npb = past_len // SEL_BLOCK
    mmat = jnp.asarray(_imp_matrix(nck, npb), BF16)
    wk = win_state.shape[1]
    rpad = -(-N_HEADS * t // LANES) * LANES
    tok = lambda w: pl.BlockSpec((1, t, w), lambda b, s, pt: (0, b, 0))
    grid_spec = pltpu.PrefetchScalarGridSpec(
        num_scalar_prefetch=1,
        grid=(bd, n_pages // pg),
        in_specs=[pl.BlockSpec((1, N_HEADS, t, LANES), lambda b, s, pt: (0, 0, b, 0)),
                  pl.BlockSpec((1, KV_HEADS, nck, LANES), lambda b, s, pt: (b, 0, 0, 0)),
                  pl.BlockSpec((1, nck, KV_WIDTH), lambda b, s, pt: (b, 0, 0)),
                  tok(2 * KV_WIDTH), tok(2 * KV_WIDTH),
                  pl.BlockSpec((1, wk, 2 * KV_WIDTH), lambda b, s, pt: (b, 0, 0)),
                  tok(LANES), tok(MIX_WIDTH),
                  pl.BlockSpec(mmat.shape, lambda b, s, pt: (0, 0))] + _page_specs(pg),
        out_specs=tok(MIX_WIDTH),
        scratch_shapes=[pltpu.VMEM((rpad, 2 * LANES + npb), BF16),
                        pltpu.VMEM((rpad, 1), F32), pltpu.VMEM((rpad, 1), F32),
                        pltpu.VMEM((rpad, 2 * LANES), F32), pltpu.VMEM((rpad, 2 * LANES), F32),
                        pltpu.VMEM((rpad, 2 * LANES), F32)],
    )
    return pl.pallas_call(
        functools.partial(_attn_sample_kernel, pg=pg, past_len=past_len),
        grid_spec=grid_spec,
        out_shape=jax.ShapeDtypeStruct((1, bd * t, MIX_WIDTH), F32),
        compiler_params=_cparams("arbitrary", "arbitrary"),
        name="attn_sample",
    )(page_table, q, kca, vcb, kvs_new, kvw_new, win_state, gates, zgs, mmat, *([cache_sel] * pg))


def _reorder_nsa_weight(w):
    d = w.shape[0]
    n_gate = 3 * N_HEADS
    zg0 = _C_ZG + n_gate
    return jnp.concatenate([w[:, :_C_ZG], w[:, zg0:zg0 + MIX_WIDTH], w[:, _C_ZG:zg0],
                            jnp.zeros((d, LANES - n_gate), w.dtype)], axis=1).astype(BF16)


def _cmp_weights(cmp_pos_w, cmp_phi):
    ratio = CMP_BLOCK // CMP_STRIDE
    pw = cmp_pos_w.reshape(2, ratio, CMP_STRIDE, HEAD_DIM)
    tiles = [jnp.concatenate([jnp.tile(pw[s, m], (1, KV_HEADS)) for s in range(2)], axis=1) for m in range(ratio)]
    eye = jnp.eye(KV_HEADS, dtype=cmp_phi.dtype)
    z = jnp.zeros((KV_WIDTH, KV_WIDTH), cmp_phi.dtype)
    phi_bd = jnp.concatenate([jnp.concatenate([jnp.kron(eye, cmp_phi[0]), z], axis=1),
                              jnp.concatenate([z, jnp.kron(eye, cmp_phi[1])], axis=1)], axis=0).astype(BF16)
    return tiles[0], tiles[1], phi_bd


def kernel(x_prompt, x_sample, mem_prompt, cache_mem_kv, cache_cmp_kv, cache_sel_kv, page_table, state_conv, state_win_kv, norm_g, final_norm_g, mem_norm_g, w_mem_kv, w_in_conv, conv_w, w_in_nsa, cmp_pos_w, cmp_phi, w_out):
    b, t, d = x_prompt.shape
    bd, td, _ = x_sample.shape
    n_mem = mem_prompt.shape[1]
    n_pages = page_table.shape[1]
    past_len = n_pages * PAGE_SIZE
    assert w_in_conv.shape[0] == 1 and w_in_nsa.shape[0] == 1 and w_out.shape[0] == 2
    assert CMP_BLOCK == 2 * CMP_STRIDE and td & (td - 1) == 0 and td < CMP_STRIDE

    w_conv_b = w_in_conv[0].astype(BF16)
    w_nsa_b = _reorder_nsa_weight(w_in_nsa[0])
    w_out_b = w_out.astype(BF16)
    w_mem_b = w_mem_kv.astype(BF16)
    pw0, pw1, phi_bd = _cmp_weights(cmp_pos_w[0], cmp_phi[0])
    kv6 = lambda a, n, r: a.reshape(1, n, r, 2, KV_HEADS, HEAD_DIM)

    mem_kv_p = _memory_kv(mem_prompt, mem_norm_g, w_mem_b)
    xp = x_prompt.reshape(b * t, d)
    tm = min(512, t)
    mqg, mix, conv_p = _proj_conv(xp, norm_g[0], w_conv_b, conv_w[0], seg=t, tm=tm)
    x1 = _layer_out(xp, mqg, mem_kv_p[0], mix, w_out_b[0], tm=tm, seg=t)
    tabs = _rope_tables(jnp.arange(t, dtype=jnp.int32))
    tq = min(256, t)
    (mqg, q, kvc_p, kvs_p, kvw_p, gates, zgs, ksa, kwa, vsb, vwb) = _proj_nsa(
        x1.reshape(b, t, d), norm_g[1], w_nsa_b, tabs, tm=tq, prompt=True)
    kca, vcb = _cmp_prompt(kvc_p, pw0, pw1, phi_bd)
    mix = _attn_prompt(q, kca, vcb, ksa, vsb, kwa, vwb, gates, zgs, tq=tq)
    y_prompt = _layer_out(x1, mqg.reshape(b * t, 512), mem_kv_p[1], mix.reshape(b * t, MIX_WIDTH), w_out_b[1],
                          tm=tm, seg=t, final_g=final_norm_g).reshape(b, t, d)
    w_keep_p = min(WINDOW, t)

    ms = bd * td
    xs = x_sample.reshape(ms, d)
    st = state_conv[0]
    zrow = jnp.zeros((bd, td - 1, MIX_WIDTH), F32)
    s1 = jnp.concatenate([st[:, 1:2], zrow], axis=1).reshape(ms, MIX_WIDTH)
    s2 = jnp.concatenate([st, zrow[:, 1:]], axis=1).reshape(ms, MIX_WIDTH)
    mqg, mix, u_s = _proj_conv(xs, norm_g[0], w_conv_b, conv_w[0], seg=td, tm=ms, state=(s1, s2))
    conv_s = u_s.reshape(bd, td, MIX_WIDTH)[:, td - 2:]
    tmo = 8 * td
    x1s = _layer_out(xs, mqg, cache_mem_kv[0].reshape(bd, n_mem, 2 * MEM_WIDTH), mix, w_out_b[0], tm=tmo, seg=td)
    tabs_s = _rope_tables(jnp.tile(past_len + jnp.arange(td, dtype=jnp.int32), bd))
    (mqg, q_s, kvc_s, kvs_s, kvw_s, gates_s, zgs_s) = _proj_nsa(
        x1s.reshape(1, ms, d), norm_g[1], w_nsa_b, tabs_s, tm=ms, prompt=False)
    pg = 16
    kca_s, vcb_s = _cmp_sample(cache_cmp_kv[0].reshape(-1, PAGE_SIZE, 2 * KV_WIDTH), page_table, pw0, pw1, phi_bd, pg)
    win_state = state_win_kv[0].reshape(bd, -1, 2 * KV_WIDTH)
    mix_s = _attn_sample(q_s, kca_s, vcb_s, kvs_s, kvw_s, win_state, gates_s, zgs_s,
                         cache_sel_kv[0].reshape(-1, PAGE_SIZE, 2 * KV_WIDTH), page_table, td, pg)
    y_sample = _layer_out(x1s, mqg.reshape(ms, 512), cache_mem_kv[1].reshape(bd, n_mem, 2 * MEM_WIDTH),
                          mix_s.reshape(ms, MIX_WIDTH), w_out_b[1], tm=tmo, seg=td, final_g=final_norm_g).reshape(bd, td, d)
    w_keep = win_state.shape[1]
    win_s = jnp.concatenate([win_state, kvw_s.reshape(bd, td, 2 * KV_WIDTH)], axis=1)[:, -w_keep:]

    return (y_prompt, y_sample, conv_p[None],
            kv6(kvc_p, b, t), kv6(kvs_p, b, t), kv6(kvw_p[:, t - w_keep_p:], b, w_keep_p),
            mem_kv_p.reshape(2, b, n_mem, 2, MEM_WIDTH // HEAD_DIM, HEAD_DIM),
            conv_s[None], kv6(kvc_s, bd, td), kv6(kvs_s, bd, td), kv6(win_s, bd, w_keep))
```

```python
import functools

import numpy as np
import jax
import jax.numpy as jnp
from jax import lax
from jax.experimental import pallas as pl
from jax.experimental.pallas import tpu as pltpu

F32 = jnp.float32
BF16 = jnp.bfloat16

D_MODEL = 1024
HEAD_DIM = 64
MIX_WIDTH = 768
MEM_WIDTH = 256
N_HEADS = 12
KV_HEADS = 4
GROUP = 3
KV_WIDTH = 256
CMP_BLOCK = 32
CMP_STRIDE = 16
SEL_BLOCK = 64
N_SEL = 16
WINDOW = 512
ROT_DIM = 16
ROPE_THETA = 500000.0
NORM_EPS = 1e-6
FORCE_BONUS = 1e4
PAGE_SIZE = 128
SCALE = HEAD_DIM ** -0.5
NEG = -(2.0 ** 100)
M_INIT = -1e30
LANES = 128
PAGES_PER_UPDATE = 4
VMEM_LIMIT = 56 * 2 ** 20


def _cparams(*sem):
    return pltpu.CompilerParams(dimension_semantics=sem, vmem_limit_bytes=VMEM_LIMIT)


def _dot(a, b):
    return jnp.dot(a, b, preferred_element_type=F32)


def _dot_nt(a, b):
    return lax.dot_general(a, b, (((1,), (1,)), ((), ())), preferred_element_type=F32)


def _rms(x, g):
    return x * lax.rsqrt(jnp.mean(x * x, axis=-1, keepdims=True) + NORM_EPS) * g


def _silu(x):
    return x * jax.nn.sigmoid(x)


def _split3(a):
    hi = a.astype(BF16)
    r1 = a - hi.astype(F32)
    mid = r1.astype(BF16)
    lo = (r1 - mid.astype(F32)).astype(BF16)
    return hi, mid, lo


def _lo_half(rows):
    return lax.broadcasted_iota(jnp.int32, (rows, LANES), 1) < HEAD_DIM


def _memkv_kernel(mem_ref, g_ref, w_ref, o_ref):
    h = _rms(mem_ref[0], g_ref[0]).astype(BF16)
    o_ref[0, 0] = _dot(h, w_ref[0])


def _memory_kv(mem, mem_norm_g, w_mem_kv_b):
    depth = w_mem_kv_b.shape[0]
    b, n_mem, d = mem.shape
    return pl.pallas_call(
        _memkv_kernel,
        grid=(depth, b),
        in_specs=[pl.BlockSpec((1, n_mem, d), lambda i, j: (j, 0, 0)),
                  pl.BlockSpec((1, 1, d), lambda i, j: (i, 0, 0)),
                  pl.BlockSpec((1, d, 2 * MEM_WIDTH), lambda i, j: (i, 0, 0))],
        out_specs=pl.BlockSpec((1, 1, n_mem, 2 * MEM_WIDTH), lambda i, j: (i, j, 0, 0)),
        out_shape=jax.ShapeDtypeStruct((depth, b, n_mem, 2 * MEM_WIDTH), F32),
        compiler_params=_cparams("arbitrary", "arbitrary"),
        name="memory_kv",
    )(mem, mem_norm_g.reshape(depth, 1, d), w_mem_kv_b)


def _proj_conv_kernel(*refs, tm, seg, has_state):
    if has_state:
        x_ref, g_ref, w_ref, cw_ref, s1_ref, s2_ref, mqg_ref, mix_ref, st_ref, ubuf = refs
    else:
        x_ref, g_ref, w_ref, cw_ref, mqg_ref, mix_ref, st_ref, ubuf = refs
    i = pl.program_id(0)
    h = _rms(x_ref[...], g_ref[...]).astype(BF16)
    mqg_ref[...] = _dot(h, w_ref[:, 0:512])
    bg = _dot(h, w_ref[:, 512:1280])
    cg = _dot(h, w_ref[:, 1280:2048])
    hin = _dot(h, w_ref[:, 2048:2816])
    zg = _dot(h, w_ref[:, 2816:3584])
    u = cg * hin

    @pl.when(i == 0)
    def _():
        ubuf[0:8, :] = jnp.zeros((8, MIX_WIDTH), F32)

    @pl.when(i > 0)
    def _():
        ubuf[0:8, :] = ubuf[tm:tm + 8, :]

    ubuf[8:8 + tm, :] = u
    u1 = ubuf[7:7 + tm, :]
    u2 = ubuf[6:6 + tm, :]
    rowpos = lax.rem(i * tm + lax.broadcasted_iota(jnp.int32, (tm, 1), 0), seg)
    if has_state:
        u1 = jnp.where(rowpos >= 1, u1, s1_ref[...])
        u2 = jnp.where(rowpos >= 2, u2, s2_ref[...])
        st_ref[...] = u
    else:
        u1 = jnp.where(rowpos >= 1, u1, 0.0)
        u2 = jnp.where(rowpos >= 2, u2, 0.0)
        st_ref[0] = ubuf[8 + tm - 2:8 + tm, :]
    y = cw_ref[0:1, :] * u2 + cw_ref[1:2, :] * u1 + cw_ref[2:3, :] * u
    mix_ref[...] = (_silu(zg) * (bg * y)).astype(mix_ref.dtype)


def _proj_conv(x2d, g, w_b, cw, seg, tm, state=None):
    m = x2d.shape[0]
    n_in = w_b.shape[1]
    has_state = state is not None
    in_specs = [pl.BlockSpec((tm, D_MODEL), lambda i: (i, 0)),
                pl.BlockSpec((1, D_MODEL), lambda i: (0, 0)),
                pl.BlockSpec((D_MODEL, n_in), lambda i: (0, 0)),
                pl.BlockSpec((3, MIX_WIDTH), lambda i: (0, 0))]
    args = [x2d, g.reshape(1, D_MODEL), w_b, cw]
    if has_state:
        in_specs += [pl.BlockSpec((tm, MIX_WIDTH), lambda i: (i, 0))] * 2
        args += list(state)
        st_shape = jax.ShapeDtypeStruct((m, MIX_WIDTH), F32)
        st_spec = pl.BlockSpec((tm, MIX_WIDTH), lambda i: (i, 0))
    else:
        st_shape = jax.ShapeDtypeStruct((m // seg, 2, MIX_WIDTH), F32)
        st_spec = pl.BlockSpec((1, 2, MIX_WIDTH), lambda i: ((i * tm) // seg, 0, 0))
    return pl.pallas_call(
        functools.partial(_proj_conv_kernel, tm=tm, seg=seg, has_state=has_state),
        grid=(m // tm,),
        in_specs=in_specs,
        out_specs=[pl.BlockSpec((tm, 512), lambda i: (i, 0)),
                   pl.BlockSpec((tm, MIX_WIDTH), lambda i: (i, 0)),
                   st_spec],
        out_shape=[jax.ShapeDtypeStruct((m, 512), F32),
                   jax.ShapeDtypeStruct((m, MIX_WIDTH), F32),
                   st_shape],
        scratch_shapes=[pltpu.VMEM((tm + 8, MIX_WIDTH), F32)],
        compiler_params=_cparams("arbitrary"),
        name="proj_conv",
    )(*args)


def _out_kernel(*refs, nseg, seg, final):
    if final:
        x_ref, mqg_ref, memkv_ref, mix_ref, w_ref, g_ref, o_ref, mo_sc = refs
    else:
        x_ref, mqg_ref, memkv_ref, mix_ref, w_ref, o_ref, mo_sc = refs
    lo = _lo_half(seg)

    def seg_body(s, r0):
        for pair in range(2):
            cols = slice(pair * LANES, (pair + 1) * LANES)
            qp = mqg_ref[pl.ds(r0, seg), cols]
            kp = memkv_ref[s, :, cols].astype(BF16)
            vp = memkv_ref[s, :, MEM_WIDTH + pair * LANES:MEM_WIDTH + (pair + 1) * LANES].astype(BF16)
            outs = []
            for half in range(2):
                keep = lo if half == 0 else jnp.logical_not(lo)
                qm = jnp.where(keep, qp, 0.0).astype(BF16)
                sc = _dot_nt(qm, kp) * SCALE
                e = jnp.exp(sc - jnp.max(sc, axis=-1, keepdims=True))
                p = e / jnp.sum(e, axis=-1, keepdims=True)
                outs.append(_dot(p.astype(BF16), vp))
            mo_sc[pl.ds(r0, seg), cols] = jnp.where(lo, outs[0], outs[1])

    if nseg == 1:
        seg_body(0, 0)
    else:
        def body(s, c):
            seg_body(s, pl.multiple_of(s * seg, seg))
            return c
        lax.fori_loop(0, nseg, body, 0)

    mg = mqg_ref[:, MEM_WIDTH:2 * MEM_WIDTH]
    a = (_silu(mg) * mo_sc[...]).astype(BF16)
    xn = x_ref[...] + (_dot(a, w_ref[0:MEM_WIDTH, :]) + _dot(mix_ref[...].astype(BF16), w_ref[MEM_WIDTH:, :]))
    if final:
        o_ref[...] = _rms(xn, g_ref[...])
    else:
        o_ref[...] = xn


def _layer_out(x2d, mqg, memkv, mix, w_out_b, tm, seg, final_g=None):
    m = x2d.shape[0]
    nseg = max(tm // seg, 1)
    seg_in = min(seg, tm)
    n_mem = memkv.shape[1]
    final = final_g is not None
    in_specs = [pl.BlockSpec((tm, D_MODEL), lambda i: (i, 0)),
                pl.BlockSpec((tm, 512), lambda i: (i, 0)),
                pl.BlockSpec((nseg, n_mem, 2 * MEM_WIDTH), lambda i: ((i * tm) // (seg * nseg), 0, 0)),
                pl.BlockSpec((tm, MIX_WIDTH), lambda i: (i, 0)),
                pl.BlockSpec((D_MODEL, D_MODEL), lambda i: (0, 0))]
    args = [x2d, mqg, memkv, mix, w_out_b]
    if final:
        in_specs.append(pl.BlockSpec((1, D_MODEL), lambda i: (0, 0)))
        args.append(final_g.reshape(1, D_MODEL))
    return pl.pallas_call(
        functools.partial(_out_kernel, nseg=nseg, seg=seg_in, final=final),
        grid=(m // tm,),
        in_specs=in_specs,
        out_specs=pl.BlockSpec((tm, D_MODEL), lambda i: (i, 0)),
        out_shape=jax.ShapeDtypeStruct((m, D_MODEL), F32),
        scratch_shapes=[pltpu.VMEM((tm, MEM_WIDTH), F32)],
        compiler_params=_cparams("arbitrary"),
        name="layer_out",
    )(*args)


_C_Q, _C_KC, _C_KS, _C_KW, _C_ZG, _C_GL, _C_END = 512, 1280, 1792, 2304, 2816, 3584, 3712


def _proj_nsa_kernel(*refs, tm, prompt):
    (x_ref, g_ref, w_ref, c_ref, sa_ref, sb_ref,
     mqg_ref, q_ref, kvc_ref, kvs_ref, kvw_ref, gates_ref, zgs_ref) = refs[:13]
    ti = pl.program_id(1)
    h = _rms(x_ref[0], g_ref[...]).astype(BF16)
    cos, sa, sb = c_ref[...], sa_ref[...], sb_ref[...]
    lo = _lo_half(tm)

    def rope(chunk):
        return chunk * cos + pltpu.roll(chunk, LANES - ROT_DIM // 2, axis=1) * sa + pltpu.roll(chunk, ROT_DIM // 2, axis=1) * sb

    mqg_ref[0] = _dot(h, w_ref[:, 0:_C_Q])
    qf = _dot(h, w_ref[:, _C_Q:_C_KC])
    for c in range(N_HEADS // 2):
        qc = rope(qf[:, c * LANES:(c + 1) * LANES]) * SCALE
        if prompt:
            q_ref[0, c * LANES:(c + 1) * LANES, :] = qc.T.astype(q_ref.dtype)
        else:
            q_ref[0, 2 * c] = jnp.where(lo, qc, 0.0).astype(q_ref.dtype)
            q_ref[0, 2 * c + 1] = jnp.where(lo, pltpu.roll(qc, HEAD_DIM, axis=1), 0.0).astype(q_ref.dtype)

    if prompt:
        ksa_ref, kwa_ref, vsb_ref, vwb_ref = refs[13:17]
        pos = ti * tm + lax.broadcasted_iota(jnp.int32, (tm, LANES), 0)
        lane = lax.broadcasted_iota(jnp.int32, (tm, LANES), 1)
        onehot = jnp.where(jnp.right_shift(pos, 6) == lane - HEAD_DIM, 1.0, 0.0)

    for name, off in (("c", _C_KC), ("s", _C_KS), ("w", _C_KW)):
        kv_ref = {"c": kvc_ref, "s": kvs_ref, "w": kvw_ref}[name]
        kk = _dot(h, w_ref[:, off:off + KV_WIDTH])
        vv = _dot(h, w_ref[:, off + KV_WIDTH:off + 2 * KV_WIDTH])
        kv_ref[0, :, KV_WIDTH:2 * KV_WIDTH] = vv
        for c in range(KV_HEADS // 2):
            kr = rope(kk[:, c * LANES:(c + 1) * LANES])
            kv_ref[0, :, c * LANES:(c + 1) * LANES] = kr
            if prompt and name != "c":
                pad = onehot if name == "s" else 0.0
                aug_ref = ksa_ref if name == "s" else kwa_ref
                aug_ref[0, 2 * c] = jnp.where(lo, kr, pad).astype(BF16)
                aug_ref[0, 2 * c + 1] = jnp.where(lo, pltpu.roll(kr, HEAD_DIM, axis=1), pad).astype(BF16)
        if prompt and name != "c":
            vt_ref = vsb_ref if name == "s" else vwb_ref
            for c in range(KV_WIDTH // LANES):
                vt_ref[0, c * LANES:(c + 1) * LANES, :] = vv[:, c * LANES:(c + 1) * LANES].T.astype(BF16)

    gates = jax.nn.sigmoid(_dot(h, w_ref[:, _C_GL:_C_END]))
    gates_ref[0] = gates.T if prompt else gates
    zgs_ref[0] = _silu(_dot(h, w_ref[:, _C_ZG:_C_GL]))


def _proj_nsa(x3d, g, w_b, tabs, tm, prompt):
    nb, t, _ = x3d.shape
    row = lambda w: pl.BlockSpec((1, tm, w), lambda b, i: (b, i, 0))
    head = lambda n: pl.BlockSpec((1, n, tm, LANES), lambda b, i: (b, 0, i, 0))
    tab = pl.BlockSpec((tm, LANES), lambda b, i: (i, 0))
    sds = jax.ShapeDtypeStruct
    col = lambda w: pl.BlockSpec((1, w, tm), lambda b, i: (b, 0, i))
    if prompt:
        q_spec, q_shape = col(MIX_WIDTH), sds((nb, MIX_WIDTH, t), BF16)
        g_spec, g_shape = col(LANES), sds((nb, LANES, t), F32)
    else:
        q_spec, q_shape = head(N_HEADS), sds((nb, N_HEADS, t, LANES), F32)
        g_spec, g_shape = row(LANES), sds((nb, t, LANES), F32)
    out_specs = [row(512), q_spec, row(512), row(512), row(512), g_spec, row(MIX_WIDTH)]
    out_shape = [sds((nb, t, 512), F32), q_shape,
                 sds((nb, t, 512), F32), sds((nb, t, 512), F32), sds((nb, t, 512), F32),
                 g_shape, sds((nb, t, MIX_WIDTH), F32)]
    if prompt:
        out_specs += [head(KV_HEADS), head(KV_HEADS), col(KV_WIDTH), col(KV_WIDTH)]
        out_shape += [sds((nb, KV_HEADS, t, LANES), BF16), sds((nb, KV_HEADS, t, LANES), BF16),
                      sds((nb, KV_WIDTH, t), BF16), sds((nb, KV_WIDTH, t), BF16)]
    return pl.pallas_call(
        functools.partial(_proj_nsa_kernel, tm=tm, prompt=prompt),
        grid=(nb, t // tm),
        in_specs=[pl.BlockSpec((1, tm, D_MODEL), lambda b, i: (b, i, 0)),
                  pl.BlockSpec((1, D_MODEL), lambda b, i: (0, 0)),
                  pl.BlockSpec((D_MODEL, _C_END), lambda b, i: (0, 0)),
                  tab, tab, tab],
        out_specs=out_specs,
        out_shape=out_shape,
        compiler_params=_cparams("arbitrary", "arbitrary"),
        name="proj_nsa",
    )(x3d, g.reshape(1, D_MODEL), w_b, *tabs)


def _rope_tables(pos):
    half = ROT_DIM // 2
    inv = ROPE_THETA ** (-jnp.arange(half, dtype=F32) * 2.0 / ROT_DIM)
    ang = pos.astype(F32)[:, None] * inv[None, :]
    cos, sin = jnp.cos(ang), jnp.sin(ang)
    n = pos.shape[0]
    one = jnp.ones((n, HEAD_DIM - ROT_DIM), F32)
    zero = jnp.zeros((n, HEAD_DIM - ROT_DIM), F32)
    zh = jnp.zeros((n, half), F32)
    c = jnp.concatenate([cos, cos, one], axis=1)
    sa = jnp.concatenate([-sin, zh, zero], axis=1)
    sb = jnp.concatenate([zh, sin, zero], axis=1)
    return tuple(jnp.tile(a, (1, LANES // HEAD_DIM)) for a in (c, sa, sb))


def _emit_cmp_blocks(out, kca_ref, vct_ref):
    n = out.shape[0]
    lo = _lo_half(n)
    for c in range(KV_HEADS // 2):
        kc = out[:, c * LANES:(c + 1) * LANES]
        kca_ref[0, 2 * c] = jnp.where(lo, kc, 0.0).astype(BF16)
        kca_ref[0, 2 * c + 1] = jnp.where(lo, pltpu.roll(kc, HEAD_DIM, axis=1), 0.0).astype(BF16)
        vct_ref[0, c * LANES:(c + 1) * LANES, :] = out[:, KV_WIDTH + c * LANES:KV_WIDTH + (c + 1) * LANES].T.astype(BF16)


def _cmp_prompt_kernel(kvc_ref, pw0_ref, pw1_ref, phi_ref, kca_ref, vcb_ref, p0_sc, p1_sc, *, rows):
    t = kvc_ref.shape[1]
    cpr = rows // CMP_STRIDE
    pw0 = pw0_ref[...][None]
    pw1 = pw1_ref[...][None]
    for i in range(t // rows):
        x = kvc_ref[0, i * rows:(i + 1) * rows, :].reshape(cpr, CMP_STRIDE, 2 * KV_WIDTH)
        p0_sc[i * cpr:(i + 1) * cpr, :] = jnp.sum(x * pw0, axis=1)
        p1_sc[i * cpr:(i + 1) * cpr, :] = jnp.sum(x * pw1, axis=1)
    nck = t // CMP_STRIDE
    blk = p0_sc[...] + pltpu.roll(p1_sc[...], nck - 1, axis=0)
    _emit_cmp_blocks(_dot(blk.astype(BF16), phi_ref[...]), kca_ref, vcb_ref)


def _cmp_prompt(kvc, pw0, pw1, phi_bd):
    b, t, _ = kvc.shape
    nck = t // CMP_STRIDE
    full = lambda s: pl.BlockSpec(s, lambda i: (0,) * len(s))
    return pl.pallas_call(
        functools.partial(_cmp_prompt_kernel, rows=512),
        grid=(b,),
        in_specs=[pl.BlockSpec((1, t, 2 * KV_WIDTH), lambda i: (i, 0, 0)),
                  full((CMP_STRIDE, 2 * KV_WIDTH)), full((CMP_STRIDE, 2 * KV_WIDTH)),
                  full((2 * KV_WIDTH, 2 * KV_WIDTH))],
        out_specs=[pl.BlockSpec((1, KV_HEADS, nck, LANES), lambda i: (i, 0, 0, 0)),
                   pl.BlockSpec((1, KV_WIDTH, nck), lambda i: (i, 0, 0))],
        out_shape=[jax.ShapeDtypeStruct((b, KV_HEADS, nck, LANES), BF16),
                   jax.ShapeDtypeStruct((b, KV_WIDTH, nck), BF16)],
        scratch_shapes=[pltpu.VMEM((nck, 2 * KV_WIDTH), F32), pltpu.VMEM((nck, 2 * KV_WIDTH), F32)],
        compiler_params=_cparams("arbitrary"),
        name="cmp_prompt",
    )(kvc, pw0, pw1, phi_bd)


def _cmp_sample_kernel(pt_ref, w_ref, phi_ref, *refs, pg):
    pages = refs[:pg]
    kca_ref, vca_ref, p_sc, stage_sc = refs[pg:]
    step = pl.program_id(1)
    ratio = CMP_BLOCK // CMP_STRIDE
    cpp = PAGE_SIZE // CMP_STRIDE
    sub = 8 // KV_HEADS
    rpp = cpp * KV_HEADS
    for k in range(pg):
        r0 = pl.multiple_of((step * pg + k) * rpp, rpp)
        for s in range(2):
            x = pages[k][0, :, s].reshape(cpp, CMP_STRIDE // sub, 8, HEAD_DIM)
            for m in range(ratio):
                y = jnp.sum(x * w_ref[s, m][None], axis=1)
                p_sc[s, m, pl.ds(r0, rpp), :] = (y[:, 0:KV_HEADS] + y[:, KV_HEADS:]).reshape(rpp, HEAD_DIM)

    @pl.when(step == pl.num_programs(1) - 1)
    def _():
        n = p_sc.shape[2]
        stage_sc[:, HEAD_DIM:] = jnp.zeros((n, LANES - HEAD_DIM), F32)
        for s, o_ref in ((0, kca_ref), (1, vca_ref)):
            blk = p_sc[s, 0] + pltpu.roll(p_sc[s, 1], n - KV_HEADS, axis=0)
            stage_sc[:, 0:HEAD_DIM] = _dot(blk.astype(BF16), phi_ref[s])
            for g in range(KV_HEADS):
                o_ref[0, g] = stage_sc[pl.ds(g, n // KV_HEADS, stride=KV_HEADS), :].astype(BF16)


def _page_specs(pg):
    return [pl.BlockSpec((1, PAGE_SIZE, 2, KV_HEADS, HEAD_DIM), functools.partial(
        lambda b, s, pt, k: (pt[b, s * pg + k], 0, 0, 0, 0), k=k)) for k in range(pg)]


def _cmp_sample(cache, page_table, w_cmp, phi_b, pg):
    bd, n_pages = page_table.shape
    nck = n_pages * PAGE_SIZE // CMP_STRIDE
    full = lambda s: pl.BlockSpec(s, lambda b, i, pt: (0,) * len(s))
    out_spec = pl.BlockSpec((1, KV_HEADS, nck, LANES), lambda b, i, pt: (b, 0, 0, 0))
    grid_spec = pltpu.PrefetchScalarGridSpec(
        num_scalar_prefetch=1,
        grid=(bd, n_pages // pg),
        in_specs=[full(w_cmp.shape), full(phi_b.shape)] + _page_specs(pg),
        out_specs=[out_spec, out_spec],
        scratch_shapes=[pltpu.VMEM((2, CMP_BLOCK // CMP_STRIDE, nck * KV_HEADS, HEAD_DIM), F32),
                        pltpu.VMEM((nck * KV_HEADS, LANES), F32)],
    )
    return pl.pallas_call(
        functools.partial(_cmp_sample_kernel, pg=pg),
        grid_spec=grid_spec,
        out_shape=[jax.ShapeDtypeStruct((bd, KV_HEADS, nck, LANES), BF16)] * 2,
        compiler_params=_cparams("arbitrary", "arbitrary"),
        name="cmp_sample",
    )(page_table, w_cmp, phi_b, *([cache] * pg))


def _imp_matrix(nck, n_sel_blocks):
    cps = SEL_BLOCK // CMP_STRIDE
    c = np.arange(nck)[:, None]
    j = np.arange(n_sel_blocks)[None, :]
    m = ((c >= cps * j) & (c <= cps * j + cps - 1)).astype(np.float32)
    m += ((c + 1 >= cps * j) & (c + 1 <= cps * j + cps - 1)).astype(np.float32)
    return m


def _softmax_masked(s, mask, axis=-1):
    s = jnp.where(mask, s, -1e30)
    p = jnp.exp(s - jnp.max(s, axis=axis, keepdims=True)) * mask.astype(F32)
    return p / jnp.maximum(jnp.sum(p, axis=axis, keepdims=True), 1e-30)


def _online_update_t(s, vt, m_sc, l_sc, acc_sc):
    m_old = m_sc[...]
    m_new = jnp.maximum(m_old, jnp.max(s, axis=0, keepdims=True))
    alpha = jnp.exp(m_old - m_new)
    p = jnp.exp(s - m_new)
    l_sc[...] = alpha * l_sc[...] + jnp.sum(p, axis=0, keepdims=True)
    acc_sc[...] = alpha * acc_sc[...] + _dot(vt, p.astype(BF16))
    m_sc[...] = m_new


def _reset_online(m_sc, l_sc, acc_sc):
    m_sc[...] = jnp.full(m_sc.shape, M_INIT, F32)
    l_sc[...] = jnp.zeros(l_sc.shape, F32)
    acc_sc[...] = jnp.zeros(acc_sc.shape, F32)


def _attn_prompt_kernel(q_ref, kca_ref, vcb_ref, ksa_ref, vsb_ref, kwa_ref, vwb_ref, gates_ref, zgs_ref,
                        m2t_ref, o_ref, m_sc, l_sc, acc_sc, *, tq, n_sel):
    qi = pl.program_id(1)
    s0 = qi * tq
    cols = GROUP * tq
    nsb = m2t_ref.shape[0]
    qpos = s0 + lax.broadcasted_iota(jnp.int32, (1, tq), 1)
    qpos3 = jnp.concatenate([qpos] * GROUP, axis=1)
    gates = gates_ref[0]
    zero_h = jnp.zeros((HEAD_DIM, tq), BF16)
    mix_t = []
    for g in range(KV_HEADS):
        vrows = slice(g * HEAD_DIM, (g + 1) * HEAD_DIM)
        qh = [q_ref[0, (GROUP * g + r) * HEAD_DIM:(GROUP * g + r + 1) * HEAD_DIM, :] for r in range(GROUP)]
        qc = jnp.concatenate([jnp.concatenate([x, zero_h], axis=0) for x in qh], axis=1)

        sc = _dot(kca_ref[0, g], qc)
        ncp = sc.shape[0]
        blk_end = lax.broadcasted_iota(jnp.int32, (ncp, 1), 0) * CMP_STRIDE + (CMP_BLOCK - 1)
        pc = _softmax_masked(sc, blk_end <= qpos3, axis=0)
        oc = _dot(vcb_ref[0, vrows, :], pc.astype(BF16))
        imp = pc[:, 0:tq] + pc[:, tq:2 * tq] + pc[:, 2 * tq:3 * tq]

        imp_t = None
        for part in _split3(imp):
            d = _dot(m2t_ref[...], part)
            imp_t = d if imp_t is None else imp_t + d
        j = lax.broadcasted_iota(jnp.int32, (nsb, tq), 0)
        cur = jnp.right_shift(s0 + lax.broadcasted_iota(jnp.int32, (nsb, tq), 1), 6)
        valid = j <= cur
        forced = (j == 0) | (j == cur) | (j == cur - 1)
        score = jnp.where(valid, imp_t + jnp.where(forced, FORCE_BONUS, 0.0), -jnp.inf)
        cnt = jnp.zeros((nsb, tq), F32)
        for i in range(nsb):
            row = score[i:i + 1, :]
            tie = jnp.where(j > i, 1.0, 0.0)
            cnt = cnt + jnp.where(row > score, 1.0, jnp.where(row == score, tie, 0.0))
        sel = (cnt < float(n_sel)) & valid
        pen = jnp.where(sel, 0.0, NEG).astype(BF16)
        qa = jnp.concatenate([jnp.concatenate([x, pen], axis=0) for x in qh], axis=1)

        def tile_start(kt):
            return pl.multiple_of(kt * tq, tq)

        def key_pos(kt):
            return tile_start(kt) + lax.broadcasted_iota(jnp.int32, (tq, 1), 0)


        _reset_online(m_sc, l_sc, acc_sc)

        def sel_scores(kt):
            return _dot(ksa_ref[0, g, pl.ds(tile_start(kt), tq), :], qa)

        def sel_body(kt, s):
            s_next = sel_scores(kt + 1)
            _online_update_t(s, vsb_ref[0, vrows, pl.ds(tile_start(kt), tq)], m_sc, l_sc, acc_sc)
            return s_next

        s = lax.fori_loop(0, qi, sel_body, sel_scores(0))
        s = jnp.where(key_pos(qi) <= qpos3, s, NEG)
        _online_update_t(s, vsb_ref[0, vrows, pl.ds(tile_start(qi), tq)], m_sc, l_sc, acc_sc)
        o_s = acc_sc[...] / l_sc[...]

        _reset_online(m_sc, l_sc, acc_sc)

        def win_scores(kt):
            kpos = key_pos(kt)
            s = _dot(kwa_ref[0, g, pl.ds(tile_start(kt), tq), :], qc)
            return jnp.where((kpos <= qpos3) & (kpos > qpos3 - WINDOW), s, NEG)

        def win_body(kt, s):
            s_next = win_scores(kt + 1)
            _online_update_t(s, vwb_ref[0, vrows, pl.ds(tile_start(kt), tq)], m_sc, l_sc, acc_sc)
            return s_next

        w0 = jnp.maximum(qi - WINDOW // tq, 0)
        s = lax.fori_loop(w0, qi, win_body, win_scores(w0))
        _online_update_t(s, vwb_ref[0, vrows, pl.ds(tile_start(qi), tq)], m_sc, l_sc, acc_sc)
        o_w = acc_sc[...] / l_sc[...]

        for r in range(GROUP):
            hd = GROUP * g + r
            cs = slice(r * tq, (r + 1) * tq)
            mix_t.append(gates[hd:hd + 1, :] * oc[:, cs] + gates[N_HEADS + hd:N_HEADS + hd + 1, :] * o_s[:, cs]
                         + gates[2 * N_HEADS + hd:2 * N_HEADS + hd + 1, :] * o_w[:, cs])
    for c in range(N_HEADS // 2):
        lanes = slice(c * LANES, (c + 1) * LANES)
        mix = jnp.concatenate([mix_t[2 * c], mix_t[2 * c + 1]], axis=0).T
        o_ref[0, :, lanes] = (zgs_ref[0, :, lanes] * mix).astype(o_ref.dtype)


def _attn_prompt(q, kca, vcb, ksa, vsb, kwa, vwb, gates, zgs, tq):
    b, _, t = q.shape
    nck = kca.shape[2]
    nsb = -(-t // SEL_BLOCK)
    slots = LANES - HEAD_DIM
    assert nsb <= slots
    m2t = np.zeros((slots, nck), np.float32)
    m2t[:nsb] = _imp_matrix(nck, nsb).T
    per_b4 = lambda n, r: pl.BlockSpec((1, n, r, LANES), lambda i, j: (i, 0, 0, 0))
    per_b3 = lambda r, w: pl.BlockSpec((1, r, w), lambda i, j: (i, 0, 0))
    cols = GROUP * tq
    return pl.pallas_call(
        functools.partial(_attn_prompt_kernel, tq=tq, n_sel=min(N_SEL, nsb)),
        grid=(b, t // tq),
        in_specs=[pl.BlockSpec((1, MIX_WIDTH, tq), lambda i, j: (i, 0, j)),
                  per_b4(KV_HEADS, nck), per_b3(KV_WIDTH, nck),
                  per_b4(KV_HEADS, t), per_b3(KV_WIDTH, t),
                  per_b4(KV_HEADS, t), per_b3(KV_WIDTH, t),
                  pl.BlockSpec((1, LANES, tq), lambda i, j: (i, 0, j)),
                  pl.BlockSpec((1, tq, MIX_WIDTH), lambda i, j: (i, j, 0)),
                  pl.BlockSpec(m2t.shape, lambda i, j: (0, 0))],
        out_specs=pl.BlockSpec((1, tq, MIX_WIDTH), lambda i, j: (i, j, 0)),
        out_shape=jax.ShapeDtypeStruct((b, t, MIX_WIDTH), F32),
        scratch_shapes=[pltpu.VMEM((1, cols), F32), pltpu.VMEM((1, cols), F32), pltpu.VMEM((HEAD_DIM, cols), F32)],
        compiler_params=_cparams("arbitrary", "arbitrary"),
        name="attn_prompt",
    )(q, kca, vcb, ksa, vsb, kwa, vwb, gates, zgs, jnp.asarray(m2t, BF16))


def _online_update_tn(s, v, m_sc, l_sc, acc_sc):
    m_old = m_sc[...]
    m_new = jnp.maximum(m_old, jnp.max(s, axis=0, keepdims=True))
    alpha = jnp.exp(m_old - m_new)
    p = jnp.exp(s - m_new)
    l_sc[...] = alpha * l_sc[...] + jnp.sum(p, axis=0, keepdims=True)
    pv = lax.dot_general(v, p.astype(BF16), (((0,), (0,)), ((), ())), preferred_element_type=F32)
    acc_sc[...] = alpha * acc_sc[...] + pv
    m_sc[...] = m_new


def _attn_sample_kernel(pt_ref, q_ref, kca_ref, vca_ref, ksn_ref, kwn_ref, win_ref, gates_ref, zgs_ref, mmat_ref,
                        *refs, pg, past_len):
    pages = refs[:pg]
    o_ref, qt_sc, pen_sc, m_sc, l_sc, acc_sc, oc_sc, ow_sc = refs[pg:]
    step = pl.program_id(1)
    t = q_ref.shape[2]
    rows = N_HEADS * t
    cpad = qt_sc.shape[1]
    npb = mmat_ref.shape[1]
    rg = GROUP * t
    col = lax.broadcasted_iota(jnp.int32, (8, cpad), 1)
    col_head = jnp.right_shift(col, t.bit_length() - 1)
    col_group = sum(jnp.where(col_head >= GROUP * i, 1, 0) for i in range(1, KV_HEADS))
    row_group = jnp.bitwise_and(lax.broadcasted_iota(jnp.int32, (8, cpad), 0), KV_HEADS - 1)
    gmask = jnp.where((col_group == row_group) & (col < rows), 0.0, NEG)

    @pl.when(step == 0)
    def _():
        tok = jnp.bitwise_and(lax.broadcasted_iota(jnp.int32, (cpad, 1), 0), t - 1)
        qpos = past_len + tok
        zero_t = jnp.zeros((t, LANES), F32)
        qheads = [q_ref[0, hd] for hd in range(N_HEADS)]
        qrows = []
        for hd in range(N_HEADS):
            g = hd // GROUP
            qh = pltpu.roll(qheads[hd], HEAD_DIM, axis=1) if g % 2 == 1 else qheads[hd]
            qrows.append(jnp.concatenate([qh, zero_t] if g // 2 == 0 else [zero_t, qh], axis=1))
        qbd = jnp.concatenate(qrows + [jnp.zeros((cpad - rows, 2 * LANES), F32)], axis=0).astype(BF16)
        q_all = jnp.concatenate(qheads + [jnp.zeros((cpad - rows, LANES), F32)], axis=0)
        qt_sc[...] = q_all.T[0:HEAD_DIM, :].astype(BF16)

        imps = []
        qpos_g = qpos[0:rg]
        for g in range(KV_HEADS):
            qg = jnp.concatenate(qheads[GROUP * g:GROUP * (g + 1)], axis=0).astype(BF16)
            sc = _dot_nt(qg, kca_ref[0, g])
            ncp = sc.shape[1]
            blk_end = lax.broadcasted_iota(jnp.int32, (1, ncp), 1) * CMP_STRIDE + (CMP_BLOCK - 1)
            pc = _softmax_masked(sc, blk_end <= qpos_g)
            oc_sc[g * rg:(g + 1) * rg, :] = _dot(pc.astype(BF16), vca_ref[0, g])
            imps.append(pc[0:t] + pc[t:2 * t] + pc[2 * t:3 * t])
        oc_sc[rows:, :] = jnp.zeros((cpad - rows, LANES), F32)
        imp = jnp.concatenate(imps, axis=0)
        imp_s = None
        for part in _split3(imp):
            d = _dot(part, mmat_ref[...])
            imp_s = d if imp_s is None else imp_s + d
        ngt = KV_HEADS * t
        width = npb + LANES
        base = jnp.concatenate([imp_s, jnp.zeros((ngt, LANES), F32)], axis=1)
        j = lax.broadcasted_iota(jnp.int32, (ngt, width), 1)
        cur = jnp.right_shift(past_len + jnp.bitwise_and(lax.broadcasted_iota(jnp.int32, (ngt, width), 0), t - 1), 6)
        valid = j <= cur
        forced = (j == 0) | (j == cur) | (j == cur - 1)
        score = jnp.where(valid, base + jnp.where(forced, FORCE_BONUS, 0.0), -jnp.inf)
        picked = jnp.zeros((ngt, width), F32)
        jf = j.astype(F32)
        for _ in range(N_SEL):
            mx = jnp.max(score, axis=-1, keepdims=True)
            idx = jnp.min(jnp.where(score == mx, jf, float(width)), axis=-1, keepdims=True)
            hit = jf == idx
            picked = jnp.where(hit, 1.0, picked)
            score = jnp.where(hit, -jnp.inf, score)
        pen_gt = jnp.where((picked > 0.5) & valid, 0.0, NEG)
        pen = jnp.concatenate([pen_gt[(hd // GROUP) * t:(hd // GROUP + 1) * t] for hd in range(N_HEADS)]
                              + [jnp.zeros((cpad - rows, width), F32)], axis=0)
        for c in range(width // LANES):
            pen_sc[c * LANES:(c + 1) * LANES, :] = pen[:, c * LANES:(c + 1) * LANES].T

        _reset_online(m_sc, l_sc, acc_sc)
        xn = ksn_ref[...]
        nrow = t * KV_HEADS
        key_t = jnp.right_shift(lax.broadcasted_iota(jnp.int32, (nrow, 1), 0), KV_HEADS.bit_length() - 1)
        col_tok = jnp.bitwise_and(lax.broadcasted_iota(jnp.int32, (1, cpad), 1), t - 1)
        s = _dot(xn[:, 0].reshape(nrow, HEAD_DIM).astype(BF16), qt_sc[...])
        s = s + jnp.concatenate([gmask] * (nrow // 8), axis=0) + pen_sc[npb:npb + 1, :]
        s = jnp.where(key_t <= col_tok, s, NEG)
        _online_update_tn(s, xn[:, 1].reshape(nrow, HEAD_DIM).astype(BF16), m_sc, l_sc, acc_sc)

        def padded(ref, cols):
            return jnp.concatenate([ref[0, :, cols], jnp.zeros((LANES - t, KV_WIDTH), F32)], axis=0).astype(BF16)

        kcols, vcols = slice(0, KV_WIDTH), slice(KV_WIDTH, 2 * KV_WIDTH)
        inew = lax.broadcasted_iota(jnp.int32, (1, LANES), 1)
        wk = win_ref.shape[1]
        kw_pos = past_len - wk + lax.broadcasted_iota(jnp.int32, (1, wk), 1)
        mask1 = (kw_pos <= qpos) & (kw_pos > qpos - WINDOW) & (kw_pos >= 0)
        mask2 = (inew < t) & (past_len + inew <= qpos) & (past_len + inew > qpos - WINDOW)
        s1 = jnp.where(mask1, _dot_nt(qbd, win_ref[0, :, kcols].astype(BF16)), -1e30)
        s2 = jnp.where(mask2, _dot_nt(qbd, padded(kwn_ref, kcols)), -1e30)
        mw = jnp.maximum(jnp.max(s1, axis=-1, keepdims=True), jnp.max(s2, axis=-1, keepdims=True))
        p1 = jnp.exp(s1 - mw) * mask1.astype(F32)
        p2 = jnp.exp(s2 - mw) * mask2.astype(F32)
        lw = jnp.sum(p1, axis=-1, keepdims=True) + jnp.sum(p2, axis=-1, keepdims=True)
        ow = _dot(p1.astype(BF16), win_ref[0, :, vcols].astype(BF16)) + _dot(p2.astype(BF16), padded(kwn_ref, vcols))
        ow = ow / jnp.maximum(lw, 1e-30)
        for g in range(KV_HEADS):
            chunk = ow[g * rg:(g + 1) * rg, (g // 2) * LANES:(g // 2 + 1) * LANES]
            ow_sc[g * rg:(g + 1) * rg, :] = pltpu.roll(chunk, HEAD_DIM, axis=1) if g % 2 == 1 else chunk
        ow_sc[rows:, :] = jnp.zeros((cpad - rows, LANES), F32)

    bpp = PAGE_SIZE // SEL_BLOCK
    rpb = SEL_BLOCK * KV_HEADS
    prow = PAGE_SIZE * KV_HEADS
    for k0 in range(0, pg, PAGES_PER_UPDATE):
        parts, vals = [], []
        for k in range(k0, k0 + PAGES_PER_UPDATE):
            x = pages[k][0]
            s = _dot(x[:, 0].reshape(prow, HEAD_DIM).astype(BF16), qt_sc[...])
            for e in range(bpp):
                bias = gmask + pen_sc[pl.ds((step * pg + k) * bpp + e, 1), :]
                parts.append((s[e * rpb:(e + 1) * rpb].reshape(rpb // 8, 8, cpad) + bias[None]).reshape(rpb, cpad))
            vals.append(x[:, 1].reshape(prow, HEAD_DIM).astype(BF16))
        _online_update_tn(jnp.concatenate(parts, axis=0), jnp.concatenate(vals, axis=0), m_sc, l_sc, acc_sc)

    @pl.when(step == pl.num_programs(1) - 1)
    def _():
        o_s_t = acc_sc[...] / l_sc[...]
        o_s = jnp.concatenate([o_s_t, jnp.zeros((cpad - HEAD_DIM, cpad), F32)], axis=0).T
        lo = _lo_half(t)
        gates = gates_ref[0]
        placed = []
        for hd in range(N_HEADS):
            rs = slice(hd * t, (hd + 1) * t)
            mix = (gates[:, hd:hd + 1] * oc_sc[rs, :] + gates[:, N_HEADS + hd:N_HEADS + hd + 1] * o_s[rs, :]
                   + gates[:, 2 * N_HEADS + hd:2 * N_HEADS + hd + 1] * ow_sc[rs, :])
            placed.append(mix if hd % 2 == 0 else pltpu.roll(mix, HEAD_DIM, axis=1))
        for c in range(N_HEADS // 2):
            cols = slice(c * LANES, (c + 1) * LANES)
            o_ref[0, :, cols] = zgs_ref[0, :, cols] * jnp.where(lo, placed[2 * c], placed[2 * c + 1])


def _attn_sample(q, kca, vca, kvs_new, kvw_new, win_state, gates, zgs, cache_sel, page_table, t, pg):
    bd, n_pages = page_table.shape
    past_len = n_pages * PAGE_SIZE
    nck = kca.shape[2]
    npb = past_len // SEL_BLOCK
    mmat = jnp.asarray(_imp_matrix(nck, npb), BF16)
    wk = win_state.shape[1]
    cpad = -(-N_HEADS * t // LANES) * LANES
    assert cpad == LANES
    tok = lambda w: pl.BlockSpec((1, t, w), lambda b, s, pt: (0, b, 0))
    per_b = pl.BlockSpec((1, KV_HEADS, nck, LANES), lambda b, s, pt: (b, 0, 0, 0))
    grid_spec = pltpu.PrefetchScalarGridSpec(
        num_scalar_prefetch=1,
        grid=(bd, n_pages // pg),
        in_specs=[pl.BlockSpec((1, N_HEADS, t, LANES), lambda b, s, pt: (0, 0, b, 0)),
                  per_b, per_b,
                  pl.BlockSpec((t, 2, KV_HEADS, HEAD_DIM), lambda b, s, pt: (b, 0, 0, 0)),
                  tok(2 * KV_WIDTH),
                  pl.BlockSpec((1, wk, 2 * KV_WIDTH), lambda b, s, pt: (b, 0, 0)),
                  tok(LANES), tok(MIX_WIDTH),
                  pl.BlockSpec(mmat.shape, lambda b, s, pt: (0, 0))] + _page_specs(pg),
        out_specs=tok(MIX_WIDTH),
        scratch_shapes=[pltpu.VMEM((HEAD_DIM, cpad), BF16),
                        pltpu.VMEM((npb + LANES, cpad), F32),
                        pltpu.VMEM((1, cpad), F32), pltpu.VMEM((1, cpad), F32), pltpu.VMEM((HEAD_DIM, cpad), F32),
                        pltpu.VMEM((cpad, LANES), F32), pltpu.VMEM((cpad, LANES), F32)],
    )
    return pl.pallas_call(
        functools.partial(_attn_sample_kernel, pg=pg, past_len=past_len),
        grid_spec=grid_spec,
        out_shape=jax.ShapeDtypeStruct((1, bd * t, MIX_WIDTH), F32),
        compiler_params=_cparams("arbitrary", "arbitrary"),
        name="attn_sample",
    )(page_table, q, kca, vca, kvs_new, kvw_new, win_state, gates, zgs, mmat, *([cache_sel] * pg))


def _reorder_nsa_weight(w):
    d = w.shape[0]
    n_gate = 3 * N_HEADS
    zg0 = _C_ZG + n_gate
    return jnp.concatenate([w[:, :_C_ZG], w[:, zg0:zg0 + MIX_WIDTH], w[:, _C_ZG:zg0],
                            jnp.zeros((d, LANES - n_gate), w.dtype)], axis=1).astype(BF16)


def _cmp_weights(cmp_pos_w, cmp_phi):
    ratio = CMP_BLOCK // CMP_STRIDE
    pw = cmp_pos_w.reshape(2, ratio, CMP_STRIDE, HEAD_DIM)
    tiles = [jnp.concatenate([jnp.tile(pw[s, m], (1, KV_HEADS)) for s in range(2)], axis=1) for m in range(ratio)]
    eye = jnp.eye(KV_HEADS, dtype=cmp_phi.dtype)
    z = jnp.zeros((KV_WIDTH, KV_WIDTH), cmp_phi.dtype)
    phi_bd = jnp.concatenate([jnp.concatenate([jnp.kron(eye, cmp_phi[0]), z], axis=1),
                              jnp.concatenate([z, jnp.kron(eye, cmp_phi[1])], axis=1)], axis=0).astype(BF16)
    sub = 8 // KV_HEADS
    w_rows = pw.reshape(2, ratio, CMP_STRIDE // sub, sub, 1, HEAD_DIM)
    w_cmp = jnp.broadcast_to(w_rows, (2, ratio, CMP_STRIDE // sub, sub, KV_HEADS, HEAD_DIM)).reshape(
        2, ratio, CMP_STRIDE // sub, 8, HEAD_DIM)
    return tiles[0], tiles[1], phi_bd, w_cmp, cmp_phi.astype(BF16)


def kernel(x_prompt, x_sample, mem_prompt, cache_mem_kv, cache_cmp_kv, cache_sel_kv, page_table, state_conv, state_win_kv, norm_g, final_norm_g, mem_norm_g, w_mem_kv, w_in_conv, conv_w, w_in_nsa, cmp_pos_w, cmp_phi, w_out):
    b, t, d = x_prompt.shape
    bd, td, _ = x_sample.shape
    n_mem = mem_prompt.shape[1]
    n_pages = page_table.shape[1]
    past_len = n_pages * PAGE_SIZE
    assert w_in_conv.shape[0] == 1 and w_in_nsa.shape[0] == 1 and w_out.shape[0] == 2
    assert CMP_BLOCK == 2 * CMP_STRIDE and td & (td - 1) == 0 and td < CMP_STRIDE

    w_conv_b = w_in_conv[0].astype(BF16)
    w_nsa_b = _reorder_nsa_weight(w_in_nsa[0])
    w_out_b = w_out.astype(BF16)
    w_mem_b = w_mem_kv.astype(BF16)
    pw0, pw1, phi_bd, w_cmp, phi_b = _cmp_weights(cmp_pos_w[0], cmp_phi[0])
    kv6 = lambda a, n, r: a.reshape(1, n, r, 2, KV_HEADS, HEAD_DIM)

    mem_kv_p = _memory_kv(mem_prompt, mem_norm_g, w_mem_b)
    xp = x_prompt.reshape(b * t, d)
    tm = min(512, t)
    mqg, mix, conv_p = _proj_conv(xp, norm_g[0], w_conv_b, conv_w[0], seg=t, tm=tm)
    x1 = _layer_out(xp, mqg, mem_kv_p[0], mix, w_out_b[0], tm=tm, seg=t)
    tabs = _rope_tables(jnp.arange(t, dtype=jnp.int32))
    tq = min(256, t)
    (mqg, q, kvc_p, kvs_p, kvw_p, gates, zgs, ksa, kwa, vsb, vwb) = _proj_nsa(
        x1.reshape(b, t, d), norm_g[1], w_nsa_b, tabs, tm=tq, prompt=True)
    kca, vcb = _cmp_prompt(kvc_p, pw0, pw1, phi_bd)
    mix = _attn_prompt(q, kca, vcb, ksa, vsb, kwa, vwb, gates, zgs, tq=tq)
    y_prompt = _layer_out(x1, mqg.reshape(b * t, 512), mem_kv_p[1], mix.reshape(b * t, MIX_WIDTH), w_out_b[1],
                          tm=tm, seg=t, final_g=final_norm_g).reshape(b, t, d)
    w_keep_p = min(WINDOW, t)

    ms = bd * td
    xs = x_sample.reshape(ms, d)
    st = state_conv[0]
    zrow = jnp.zeros((bd, td - 1, MIX_WIDTH), F32)
    s1 = jnp.concatenate([st[:, 1:2], zrow], axis=1).reshape(ms, MIX_WIDTH)
    s2 = jnp.concatenate([st, zrow[:, 1:]], axis=1).reshape(ms, MIX_WIDTH)
    mqg, mix, u_s = _proj_conv(xs, norm_g[0], w_conv_b, conv_w[0], seg=td, tm=ms, state=(s1, s2))
    conv_s = u_s.reshape(bd, td, MIX_WIDTH)[:, td - 2:]
    tmo = 8 * td
    x1s = _layer_out(xs, mqg, cache_mem_kv[0].reshape(bd, n_mem, 2 * MEM_WIDTH), mix, w_out_b[0], tm=tmo, seg=td)
    tabs_s = _rope_tables(jnp.tile(past_len + jnp.arange(td, dtype=jnp.int32), bd))
    (mqg, q_s, kvc_s, kvs_s, kvw_s, gates_s, zgs_s) = _proj_nsa(
        x1s.reshape(1, ms, d), norm_g[1], w_nsa_b, tabs_s, tm=ms, prompt=False)
    pg = 16
    kca_s, vca_s = _cmp_sample(cache_cmp_kv[0], page_table, w_cmp, phi_b, pg)
    win_state = state_win_kv[0].reshape(bd, -1, 2 * KV_WIDTH)
    mix_s = _attn_sample(q_s, kca_s, vca_s, kvs_s.reshape(ms, 2, KV_HEADS, HEAD_DIM), kvw_s, win_state, gates_s, zgs_s,
                         cache_sel_kv[0], page_table, td, pg)
    y_sample = _layer_out(x1s, mqg.reshape(ms, 512), cache_mem_kv[1].reshape(bd, n_mem, 2 * MEM_WIDTH),
                          mix_s.reshape(ms, MIX_WIDTH), w_out_b[1], tm=tmo, seg=td, final_g=final_norm_g).reshape(bd, td, d)
    w_keep = win_state.shape[1]
    win_s = jnp.concatenate([win_state, kvw_s.reshape(bd, td, 2 * KV_WIDTH)], axis=1)[:, -w_keep:]

    return (y_prompt, y_sample, conv_p[None],
            kv6(kvc_p, b, t), kv6(kvs_p, b, t), kv6(kvw_p[:, t - w_keep_p:], b, w_keep_p),
            mem_kv_p.reshape(2, b, n_mem, 2, MEM_WIDTH // HEAD_DIM, HEAD_DIM),
            conv_s[None], kv6(kvc_s, bd, td), kv6(kvs_s, bd, td), kv6(win_s, bd, w_keep))
```

```python
import functools

import numpy as np
import jax
import jax.numpy as jnp
from jax import lax
from jax.experimental import pallas as pl
from jax.experimental.pallas import tpu as pltpu

F32 = jnp.float32
BF16 = jnp.bfloat16

D_MODEL = 1024
HEAD_DIM = 64
MIX_WIDTH = 768
MEM_WIDTH = 256
N_HEADS = 12
KV_HEADS = 4
GROUP = 3
KV_WIDTH = 256
CMP_BLOCK = 32
CMP_STRIDE = 16
SEL_BLOCK = 64
N_SEL = 16
WINDOW = 512
ROT_DIM = 16
ROPE_THETA = 500000.0
NORM_EPS = 1e-6
FORCE_BONUS = 1e4
PAGE_SIZE = 128
SCALE = HEAD_DIM ** -0.5
NEG = -(2.0 ** 100)
M_INIT = -1e30
LANES = 128
PAGES_PER_UPDATE = 8
VMEM_LIMIT = 56 * 2 ** 20


def _cparams(*sem):
    return pltpu.CompilerParams(dimension_semantics=sem, vmem_limit_bytes=VMEM_LIMIT)


def _dot(a, b):
    return jnp.dot(a, b, preferred_element_type=F32)


def _dot_nt(a, b):
    return lax.dot_general(a, b, (((1,), (1,)), ((), ())), preferred_element_type=F32)


def _rms(x, g):
    return x * lax.rsqrt(jnp.mean(x * x, axis=-1, keepdims=True) + NORM_EPS) * g


def _silu(x):
    return x * jax.nn.sigmoid(x)


def _split3(a):
    hi = a.astype(BF16)
    r1 = a - hi.astype(F32)
    mid = r1.astype(BF16)
    lo = (r1 - mid.astype(F32)).astype(BF16)
    return hi, mid, lo


def _lo_half(rows):
    return lax.broadcasted_iota(jnp.int32, (rows, LANES), 1) < HEAD_DIM


def _memkv_kernel(mem_ref, g_ref, w_ref, o_ref):
    h = _rms(mem_ref[0], g_ref[0]).astype(BF16)
    o_ref[0, 0] = _dot(h, w_ref[0])


def _memory_kv(mem, mem_norm_g, w_mem_kv_b):
    depth = w_mem_kv_b.shape[0]
    b, n_mem, d = mem.shape
    return pl.pallas_call(
        _memkv_kernel,
        grid=(depth, b),
        in_specs=[pl.BlockSpec((1, n_mem, d), lambda i, j: (j, 0, 0)),
                  pl.BlockSpec((1, 1, d), lambda i, j: (i, 0, 0)),
                  pl.BlockSpec((1, d, 2 * MEM_WIDTH), lambda i, j: (i, 0, 0))],
        out_specs=pl.BlockSpec((1, 1, n_mem, 2 * MEM_WIDTH), lambda i, j: (i, j, 0, 0)),
        out_shape=jax.ShapeDtypeStruct((depth, b, n_mem, 2 * MEM_WIDTH), F32),
        compiler_params=_cparams("arbitrary", "arbitrary"),
        name="memory_kv",
    )(mem, mem_norm_g.reshape(depth, 1, d), w_mem_kv_b)


def _proj_conv_kernel(*refs, tm, seg, has_state):
    if has_state:
        x_ref, g_ref, w_ref, cw_ref, s1_ref, s2_ref, mqg_ref, mix_ref, st_ref, ubuf = refs
    else:
        x_ref, g_ref, w_ref, cw_ref, mqg_ref, mix_ref, st_ref, ubuf = refs
    i = pl.program_id(0)
    h = _rms(x_ref[...], g_ref[...]).astype(BF16)
    mqg_ref[...] = _dot(h, w_ref[:, 0:512])
    bg = _dot(h, w_ref[:, 512:1280])
    cg = _dot(h, w_ref[:, 1280:2048])
    hin = _dot(h, w_ref[:, 2048:2816])
    zg = _dot(h, w_ref[:, 2816:3584])
    u = cg * hin

    @pl.when(i == 0)
    def _():
        ubuf[0:8, :] = jnp.zeros((8, MIX_WIDTH), F32)

    @pl.when(i > 0)
    def _():
        ubuf[0:8, :] = ubuf[tm:tm + 8, :]

    ubuf[8:8 + tm, :] = u
    u1 = ubuf[7:7 + tm, :]
    u2 = ubuf[6:6 + tm, :]
    rowpos = lax.rem(i * tm + lax.broadcasted_iota(jnp.int32, (tm, 1), 0), seg)
    if has_state:
        u1 = jnp.where(rowpos >= 1, u1, s1_ref[...])
        u2 = jnp.where(rowpos >= 2, u2, s2_ref[...])
        st_ref[...] = u
    else:
        u1 = jnp.where(rowpos >= 1, u1, 0.0)
        u2 = jnp.where(rowpos >= 2, u2, 0.0)
        st_ref[0] = ubuf[8 + tm - 2:8 + tm, :]
    y = cw_ref[0:1, :] * u2 + cw_ref[1:2, :] * u1 + cw_ref[2:3, :] * u
    mix_ref[...] = (_silu(zg) * (bg * y)).astype(mix_ref.dtype)


def _proj_conv(x2d, g, w_b, cw, seg, tm, state=None):
    m = x2d.shape[0]
    n_in = w_b.shape[1]
    has_state = state is not None
    in_specs = [pl.BlockSpec((tm, D_MODEL), lambda i: (i, 0)),
                pl.BlockSpec((1, D_MODEL), lambda i: (0, 0)),
                pl.BlockSpec((D_MODEL, n_in), lambda i: (0, 0)),
                pl.BlockSpec((3, MIX_WIDTH), lambda i: (0, 0))]
    args = [x2d, g.reshape(1, D_MODEL), w_b, cw]
    if has_state:
        in_specs += [pl.BlockSpec((tm, MIX_WIDTH), lambda i: (i, 0))] * 2
        args += list(state)
        st_shape = jax.ShapeDtypeStruct((m, MIX_WIDTH), F32)
        st_spec = pl.BlockSpec((tm, MIX_WIDTH), lambda i: (i, 0))
    else:
        st_shape = jax.ShapeDtypeStruct((m // seg, 2, MIX_WIDTH), F32)
        st_spec = pl.BlockSpec((1, 2, MIX_WIDTH), lambda i: ((i * tm) // seg, 0, 0))
    return pl.pallas_call(
        functools.partial(_proj_conv_kernel, tm=tm, seg=seg, has_state=has_state),
        grid=(m // tm,),
        in_specs=in_specs,
        out_specs=[pl.BlockSpec((tm, 512), lambda i: (i, 0)),
                   pl.BlockSpec((tm, MIX_WIDTH), lambda i: (i, 0)),
                   st_spec],
        out_shape=[jax.ShapeDtypeStruct((m, 512), F32),
                   jax.ShapeDtypeStruct((m, MIX_WIDTH), F32),
                   st_shape],
        scratch_shapes=[pltpu.VMEM((tm + 8, MIX_WIDTH), F32)],
        compiler_params=_cparams("arbitrary"),
        name="proj_conv",
    )(*args)


def _out_kernel(*refs, nseg, seg, final):
    if final:
        x_ref, mqg_ref, memkv_ref, mix_ref, w_ref, g_ref, o_ref, mo_sc = refs
    else:
        x_ref, mqg_ref, memkv_ref, mix_ref, w_ref, o_ref, mo_sc = refs
    lo = _lo_half(seg)

    def seg_body(s, r0):
        for pair in range(2):
            cols = slice(pair * LANES, (pair + 1) * LANES)
            qp = mqg_ref[pl.ds(r0, seg), cols]
            kp = memkv_ref[s, :, cols].astype(BF16)
            vp = memkv_ref[s, :, MEM_WIDTH + pair * LANES:MEM_WIDTH + (pair + 1) * LANES].astype(BF16)
            outs = []
            for half in range(2):
                keep = lo if half == 0 else jnp.logical_not(lo)
                qm = jnp.where(keep, qp, 0.0).astype(BF16)
                sc = _dot_nt(qm, kp) * SCALE
                e = jnp.exp(sc - jnp.max(sc, axis=-1, keepdims=True))
                p = e / jnp.sum(e, axis=-1, keepdims=True)
                outs.append(_dot(p.astype(BF16), vp))
            mo_sc[pl.ds(r0, seg), cols] = jnp.where(lo, outs[0], outs[1])

    if nseg == 1:
        seg_body(0, 0)
    else:
        def body(s, c):
            seg_body(s, pl.multiple_of(s * seg, seg))
            return c
        lax.fori_loop(0, nseg, body, 0)

    mg = mqg_ref[:, MEM_WIDTH:2 * MEM_WIDTH]
    a = (_silu(mg) * mo_sc[...]).astype(BF16)
    xn = x_ref[...] + (_dot(a, w_ref[0:MEM_WIDTH, :]) + _dot(mix_ref[...].astype(BF16), w_ref[MEM_WIDTH:, :]))
    if final:
        o_ref[...] = _rms(xn, g_ref[...])
    else:
        o_ref[...] = xn


def _layer_out(x2d, mqg, memkv, mix, w_out_b, tm, seg, final_g=None):
    m = x2d.shape[0]
    nseg = max(tm // seg, 1)
    seg_in = min(seg, tm)
    n_mem = memkv.shape[1]
    final = final_g is not None
    in_specs = [pl.BlockSpec((tm, D_MODEL), lambda i: (i, 0)),
                pl.BlockSpec((tm, 512), lambda i: (i, 0)),
                pl.BlockSpec((nseg, n_mem, 2 * MEM_WIDTH), lambda i: ((i * tm) // (seg * nseg), 0, 0)),
                pl.BlockSpec((tm, MIX_WIDTH), lambda i: (i, 0)),
                pl.BlockSpec((D_MODEL, D_MODEL), lambda i: (0, 0))]
    args = [x2d, mqg, memkv, mix, w_out_b]
    if final:
        in_specs.append(pl.BlockSpec((1, D_MODEL), lambda i: (0, 0)))
        args.append(final_g.reshape(1, D_MODEL))
    return pl.pallas_call(
        functools.partial(_out_kernel, nseg=nseg, seg=seg_in, final=final),
        grid=(m // tm,),
        in_specs=in_specs,
        out_specs=pl.BlockSpec((tm, D_MODEL), lambda i: (i, 0)),
        out_shape=jax.ShapeDtypeStruct((m, D_MODEL), F32),
        scratch_shapes=[pltpu.VMEM((tm, MEM_WIDTH), F32)],
        compiler_params=_cparams("arbitrary"),
        name="layer_out",
    )(*args)


_C_Q, _C_KC, _C_KS, _C_KW, _C_ZG, _C_GL, _C_END = 512, 1280, 1792, 2304, 2816, 3584, 3712


def _proj_nsa_kernel(*refs, tm, prompt):
    (x_ref, g_ref, w_ref, c_ref, sa_ref, sb_ref,
     mqg_ref, q_ref, kvc_ref, kvs_ref, kvw_ref, gates_ref, zgs_ref) = refs[:13]
    ti = pl.program_id(1)
    h = _rms(x_ref[0], g_ref[...]).astype(BF16)
    cos, sa, sb = c_ref[...], sa_ref[...], sb_ref[...]
    lo = _lo_half(tm)

    def rope(chunk):
        return chunk * cos + pltpu.roll(chunk, LANES - ROT_DIM // 2, axis=1) * sa + pltpu.roll(chunk, ROT_DIM // 2, axis=1) * sb

    mqg_ref[0] = _dot(h, w_ref[:, 0:_C_Q])
    qf = _dot(h, w_ref[:, _C_Q:_C_KC])
    for c in range(N_HEADS // 2):
        qc = rope(qf[:, c * LANES:(c + 1) * LANES]) * SCALE
        if prompt:
            q_ref[0, c * LANES:(c + 1) * LANES, :] = qc.T.astype(q_ref.dtype)
        else:
            q_ref[0, 2 * c] = jnp.where(lo, qc, 0.0).astype(q_ref.dtype)
            q_ref[0, 2 * c + 1] = jnp.where(lo, pltpu.roll(qc, HEAD_DIM, axis=1), 0.0).astype(q_ref.dtype)

    if prompt:
        ksa_ref, kwa_ref, vsb_ref, vwb_ref = refs[13:17]
        pos = ti * tm + lax.broadcasted_iota(jnp.int32, (tm, LANES), 0)
        lane = lax.broadcasted_iota(jnp.int32, (tm, LANES), 1)
        onehot = jnp.where(jnp.right_shift(pos, 6) == lane - HEAD_DIM, 1.0, 0.0)

    for name, off in (("c", _C_KC), ("s", _C_KS), ("w", _C_KW)):
        kv_ref = {"c": kvc_ref, "s": kvs_ref, "w": kvw_ref}[name]
        kk = _dot(h, w_ref[:, off:off + KV_WIDTH])
        vv = _dot(h, w_ref[:, off + KV_WIDTH:off + 2 * KV_WIDTH])
        kv_ref[0, :, KV_WIDTH:2 * KV_WIDTH] = vv
        for c in range(KV_HEADS // 2):
            kr = rope(kk[:, c * LANES:(c + 1) * LANES])
            kv_ref[0, :, c * LANES:(c + 1) * LANES] = kr
            if prompt and name != "c":
                pad = onehot if name == "s" else 0.0
                aug_ref = ksa_ref if name == "s" else kwa_ref
                aug_ref[0, 2 * c] = jnp.where(lo, kr, pad).astype(BF16)
                aug_ref[0, 2 * c + 1] = jnp.where(lo, pltpu.roll(kr, HEAD_DIM, axis=1), pad).astype(BF16)
        if prompt and name != "c":
            vt_ref = vsb_ref if name == "s" else vwb_ref
            for c in range(KV_WIDTH // LANES):
                vt_ref[0, c * LANES:(c + 1) * LANES, :] = vv[:, c * LANES:(c + 1) * LANES].T.astype(BF16)

    gates = jax.nn.sigmoid(_dot(h, w_ref[:, _C_GL:_C_END]))
    gates_ref[0] = gates.T if prompt else gates
    zgs_ref[0] = _silu(_dot(h, w_ref[:, _C_ZG:_C_GL]))


def _proj_nsa(x3d, g, w_b, tabs, tm, prompt):
    nb, t, _ = x3d.shape
    row = lambda w: pl.BlockSpec((1, tm, w), lambda b, i: (b, i, 0))
    head = lambda n: pl.BlockSpec((1, n, tm, LANES), lambda b, i: (b, 0, i, 0))
    tab = pl.BlockSpec((tm, LANES), lambda b, i: (i, 0))
    sds = jax.ShapeDtypeStruct
    col = lambda w: pl.BlockSpec((1, w, tm), lambda b, i: (b, 0, i))
    if prompt:
        q_spec, q_shape = col(MIX_WIDTH), sds((nb, MIX_WIDTH, t), BF16)
        g_spec, g_shape = col(LANES), sds((nb, LANES, t), F32)
    else:
        q_spec, q_shape = head(N_HEADS), sds((nb, N_HEADS, t, LANES), F32)
        g_spec, g_shape = row(LANES), sds((nb, t, LANES), F32)
    out_specs = [row(512), q_spec, row(512), row(512), row(512), g_spec, row(MIX_WIDTH)]
    out_shape = [sds((nb, t, 512), F32), q_shape,
                 sds((nb, t, 512), F32), sds((nb, t, 512), F32), sds((nb, t, 512), F32),
                 g_shape, sds((nb, t, MIX_WIDTH), F32)]
    if prompt:
        out_specs += [head(KV_HEADS), head(KV_HEADS), col(KV_WIDTH), col(KV_WIDTH)]
        out_shape += [sds((nb, KV_HEADS, t, LANES), BF16), sds((nb, KV_HEADS, t, LANES), BF16),
                      sds((nb, KV_WIDTH, t), BF16), sds((nb, KV_WIDTH, t), BF16)]
    return pl.pallas_call(
        functools.partial(_proj_nsa_kernel, tm=tm, prompt=prompt),
        grid=(nb, t // tm),
        in_specs=[pl.BlockSpec((1, tm, D_MODEL), lambda b, i: (b, i, 0)),
                  pl.BlockSpec((1, D_MODEL), lambda b, i: (0, 0)),
                  pl.BlockSpec((D_MODEL, _C_END), lambda b, i: (0, 0)),
                  tab, tab, tab],
        out_specs=out_specs,
        out_shape=out_shape,
        compiler_params=_cparams("arbitrary", "arbitrary"),
        name="proj_nsa",
    )(x3d, g.reshape(1, D_MODEL), w_b, *tabs)


def _rope_tables(pos):
    half = ROT_DIM // 2
    inv = ROPE_THETA ** (-jnp.arange(half, dtype=F32) * 2.0 / ROT_DIM)
    ang = pos.astype(F32)[:, None] * inv[None, :]
    cos, sin = jnp.cos(ang), jnp.sin(ang)
    n = pos.shape[0]
    one = jnp.ones((n, HEAD_DIM - ROT_DIM), F32)
    zero = jnp.zeros((n, HEAD_DIM - ROT_DIM), F32)
    zh = jnp.zeros((n, half), F32)
    c = jnp.concatenate([cos, cos, one], axis=1)
    sa = jnp.concatenate([-sin, zh, zero], axis=1)
    sb = jnp.concatenate([zh, sin, zero], axis=1)
    return tuple(jnp.tile(a, (1, LANES // HEAD_DIM)) for a in (c, sa, sb))


def _emit_cmp_blocks(out, kca_ref, v_ref, transpose_v):
    n = out.shape[0]
    lo = _lo_half(n)
    for c in range(KV_HEADS // 2):
        kc = out[:, c * LANES:(c + 1) * LANES]
        kca_ref[0, 2 * c] = jnp.where(lo, kc, 0.0).astype(BF16)
        kca_ref[0, 2 * c + 1] = jnp.where(lo, pltpu.roll(kc, HEAD_DIM, axis=1), 0.0).astype(BF16)
        if transpose_v:
            v_ref[0, c * LANES:(c + 1) * LANES, :] = out[:, KV_WIDTH + c * LANES:KV_WIDTH + (c + 1) * LANES].T.astype(BF16)
    if not transpose_v:
        v_ref[0] = out[:, KV_WIDTH:].astype(BF16)


def _cmp_prompt_kernel(kvc_ref, pw0_ref, pw1_ref, phi_ref, kca_ref, vcb_ref, p0_sc, p1_sc, *, rows):
    t = kvc_ref.shape[1]
    cpr = rows // CMP_STRIDE
    pw0 = pw0_ref[...][None]
    pw1 = pw1_ref[...][None]
    for i in range(t // rows):
        x = kvc_ref[0, i * rows:(i + 1) * rows, :].reshape(cpr, CMP_STRIDE, 2 * KV_WIDTH)
        p0_sc[i * cpr:(i + 1) * cpr, :] = jnp.sum(x * pw0, axis=1)
        p1_sc[i * cpr:(i + 1) * cpr, :] = jnp.sum(x * pw1, axis=1)
    nck = t // CMP_STRIDE
    blk = p0_sc[...] + pltpu.roll(p1_sc[...], nck - 1, axis=0)
    _emit_cmp_blocks(_dot(blk.astype(BF16), phi_ref[...]), kca_ref, vcb_ref, transpose_v=True)


def _cmp_prompt(kvc, pw0, pw1, phi_bd):
    b, t, _ = kvc.shape
    nck = t // CMP_STRIDE
    full = lambda s: pl.BlockSpec(s, lambda i: (0,) * len(s))
    return pl.pallas_call(
        functools.partial(_cmp_prompt_kernel, rows=512),
        grid=(b,),
        in_specs=[pl.BlockSpec((1, t, 2 * KV_WIDTH), lambda i: (i, 0, 0)),
                  full((CMP_STRIDE, 2 * KV_WIDTH)), full((CMP_STRIDE, 2 * KV_WIDTH)),
                  full((2 * KV_WIDTH, 2 * KV_WIDTH))],
        out_specs=[pl.BlockSpec((1, KV_HEADS, nck, LANES), lambda i: (i, 0, 0, 0)),
                   pl.BlockSpec((1, KV_WIDTH, nck), lambda i: (i, 0, 0))],
        out_shape=[jax.ShapeDtypeStruct((b, KV_HEADS, nck, LANES), BF16),
                   jax.ShapeDtypeStruct((b, KV_WIDTH, nck), BF16)],
        scratch_shapes=[pltpu.VMEM((nck, 2 * KV_WIDTH), F32), pltpu.VMEM((nck, 2 * KV_WIDTH), F32)],
        compiler_params=_cparams("arbitrary"),
        name="cmp_prompt",
    )(kvc, pw0, pw1, phi_bd)


def _cmp_sample_kernel(pt_ref, pw0_ref, pw1_ref, phi_ref, *refs, pg):
    pages = refs[:pg]
    kca_ref, vcb_ref, p0_sc, p1_sc = refs[pg:]
    step = pl.program_id(1)
    cpp = PAGE_SIZE // CMP_STRIDE
    pw0 = pw0_ref[...][None]
    pw1 = pw1_ref[...][None]
    for k in range(pg):
        xt = pages[k][0]
        x = jnp.concatenate([xt[c * LANES:(c + 1) * LANES, :].T for c in range(2 * KV_WIDTH // LANES)], axis=1)
        x = x.reshape(cpp, CMP_STRIDE, 2 * KV_WIDTH)
        r0 = pl.multiple_of((step * pg + k) * cpp, cpp)
        p0_sc[pl.ds(r0, cpp), :] = jnp.sum(x * pw0, axis=1)
        p1_sc[pl.ds(r0, cpp), :] = jnp.sum(x * pw1, axis=1)

    @pl.when(step == pl.num_programs(1) - 1)
    def _():
        nck = p0_sc.shape[0]
        blk = p0_sc[...] + pltpu.roll(p1_sc[...], nck - 1, axis=0)
        _emit_cmp_blocks(_dot(blk.astype(BF16), phi_ref[...]), kca_ref, vcb_ref, transpose_v=False)


def _page_specs(pg):
    return [pl.BlockSpec((1, 2 * KV_WIDTH, PAGE_SIZE), functools.partial(
        lambda b, s, pt, k: (pt[b, s * pg + k], 0, 0), k=k)) for k in range(pg)]


def _cmp_sample(cache_t, page_table, pw0, pw1, phi_bd, pg):
    bd, n_pages = page_table.shape
    nck = n_pages * PAGE_SIZE // CMP_STRIDE
    full = lambda s: pl.BlockSpec(s, lambda b, i, pt: (0,) * len(s))
    grid_spec = pltpu.PrefetchScalarGridSpec(
        num_scalar_prefetch=1,
        grid=(bd, n_pages // pg),
        in_specs=[full((CMP_STRIDE, 2 * KV_WIDTH)), full((CMP_STRIDE, 2 * KV_WIDTH)),
                  full((2 * KV_WIDTH, 2 * KV_WIDTH))] + _page_specs(pg),
        out_specs=[pl.BlockSpec((1, KV_HEADS, nck, LANES), lambda b, i, pt: (b, 0, 0, 0)),
                   pl.BlockSpec((1, nck, KV_WIDTH), lambda b, i, pt: (b, 0, 0))],
        scratch_shapes=[pltpu.VMEM((nck, 2 * KV_WIDTH), F32), pltpu.VMEM((nck, 2 * KV_WIDTH), F32)],
    )
    return pl.pallas_call(
        functools.partial(_cmp_sample_kernel, pg=pg),
        grid_spec=grid_spec,
        out_shape=[jax.ShapeDtypeStruct((bd, KV_HEADS, nck, LANES), BF16),
                   jax.ShapeDtypeStruct((bd, nck, KV_WIDTH), BF16)],
        compiler_params=_cparams("arbitrary", "arbitrary"),
        name="cmp_sample",
    )(page_table, pw0, pw1, phi_bd, *([cache_t] * pg))


def _imp_matrix(nck, n_sel_blocks):
    cps = SEL_BLOCK // CMP_STRIDE
    c = np.arange(nck)[:, None]
    j = np.arange(n_sel_blocks)[None, :]
    m = ((c >= cps * j) & (c <= cps * j + cps - 1)).astype(np.float32)
    m += ((c + 1 >= cps * j) & (c + 1 <= cps * j + cps - 1)).astype(np.float32)
    return m


def _softmax_masked(s, mask, axis=-1):
    s = jnp.where(mask, s, -1e30)
    p = jnp.exp(s - jnp.max(s, axis=axis, keepdims=True)) * mask.astype(F32)
    return p / jnp.maximum(jnp.sum(p, axis=axis, keepdims=True), 1e-30)


def _online_update_t(s, vt, m_sc, l_sc, acc_sc):
    m_old = m_sc[...]
    m_new = jnp.maximum(m_old, jnp.max(s, axis=0, keepdims=True))
    alpha = jnp.exp(m_old - m_new)
    p = jnp.exp(s - m_new)
    l_sc[...] = alpha * l_sc[...] + jnp.sum(p, axis=0, keepdims=True)
    acc_sc[...] = alpha * acc_sc[...] + _dot(vt, p.astype(BF16))
    m_sc[...] = m_new


def _reset_online(m_sc, l_sc, acc_sc):
    m_sc[...] = jnp.full(m_sc.shape, M_INIT, F32)
    l_sc[...] = jnp.zeros(l_sc.shape, F32)
    acc_sc[...] = jnp.zeros(acc_sc.shape, F32)


def _attn_prompt_kernel(q_ref, kca_ref, vcb_ref, ksa_ref, vsb_ref, kwa_ref, vwb_ref, gates_ref, zgs_ref,
                        m2t_ref, o_ref, m_sc, l_sc, acc_sc, score_sc, *, tq, n_sel):
    qi = pl.program_id(1)
    s0 = qi * tq
    cols = GROUP * tq
    nsb = m2t_ref.shape[0]
    qpos = s0 + lax.broadcasted_iota(jnp.int32, (1, tq), 1)
    qpos3 = jnp.concatenate([qpos] * GROUP, axis=1)
    gates = gates_ref[0]
    zero_h = jnp.zeros((HEAD_DIM, tq), BF16)
    mix_t = []
    for g in range(KV_HEADS):
        vrows = slice(g * HEAD_DIM, (g + 1) * HEAD_DIM)
        qh = [q_ref[0, (GROUP * g + r) * HEAD_DIM:(GROUP * g + r + 1) * HEAD_DIM, :] for r in range(GROUP)]
        qc = jnp.concatenate([jnp.concatenate([x, zero_h], axis=0) for x in qh], axis=1)

        sc = _dot(kca_ref[0, g], qc)
        ncp = sc.shape[0]
        blk_end = lax.broadcasted_iota(jnp.int32, (ncp, 1), 0) * CMP_STRIDE + (CMP_BLOCK - 1)
        pc = _softmax_masked(sc, blk_end <= qpos3, axis=0)
        oc = _dot(vcb_ref[0, vrows, :], pc.astype(BF16))
        imp = pc[:, 0:tq] + pc[:, tq:2 * tq] + pc[:, 2 * tq:3 * tq]

        imp_t = None
        for part in _split3(imp):
            d = _dot(m2t_ref[...], part)
            imp_t = d if imp_t is None else imp_t + d
        j = lax.broadcasted_iota(jnp.int32, (nsb, tq), 0)
        cur = jnp.right_shift(s0 + lax.broadcasted_iota(jnp.int32, (nsb, tq), 1), 6)
        valid = j <= cur
        forced = (j == 0) | (j == cur) | (j == cur - 1)
        score = jnp.where(valid, imp_t + jnp.where(forced, FORCE_BONUS, 0.0), -jnp.inf)
        score_sc[...] = score
        bpt = tq // SEL_BLOCK

        def rank_body(it, cnt):
            for u in range(bpt):
                i = it * bpt + u
                row = score_sc[pl.ds(i, 1), :]
                tie = jnp.where(j > i, 1.0, 0.0)
                cnt = cnt + jnp.where(row > score, 1.0, jnp.where(row == score, tie, 0.0))
            return cnt

        cnt = lax.fori_loop(0, jnp.minimum(qi + 1, nsb // bpt), rank_body, jnp.zeros((nsb, tq), F32))
        sel = (cnt < float(n_sel)) & valid
        pen = jnp.where(sel, 0.0, NEG).astype(BF16)
        qa = jnp.concatenate([jnp.concatenate([x, pen], axis=0) for x in qh], axis=1)

        def tile_start(kt):
            return pl.multiple_of(kt * tq, tq)

        def key_pos(kt):
            return tile_start(kt) + lax.broadcasted_iota(jnp.int32, (tq, 1), 0)

        def run_branch(scores, values, first, last_mask):
            _reset_online(m_sc, l_sc, acc_sc)

            def body(kt, s):
                s_next = scores(kt + 1)
                _online_update_t(s, values(kt), m_sc, l_sc, acc_sc)
                return s_next

            s = lax.fori_loop(first, qi, body, scores(first))
            _online_update_t(last_mask(s), values(qi), m_sc, l_sc, acc_sc)
            return acc_sc[...] / l_sc[...]

        o_s = run_branch(
            lambda kt: _dot(ksa_ref[0, g, pl.ds(tile_start(kt), tq), :], qa),
            lambda kt: vsb_ref[0, vrows, pl.ds(tile_start(kt), tq)],
            0, lambda s: jnp.where(key_pos(qi) <= qpos3, s, NEG))

        def win_scores(kt):
            kpos = key_pos(kt)
            s = _dot(kwa_ref[0, g, pl.ds(tile_start(kt), tq), :], qc)
            return jnp.where((kpos <= qpos3) & (kpos > qpos3 - WINDOW), s, NEG)

        o_w = run_branch(win_scores, lambda kt: vwb_ref[0, vrows, pl.ds(tile_start(kt), tq)],
                         jnp.maximum(qi - WINDOW // tq, 0), lambda s: s)

        for r in range(GROUP):
            hd = GROUP * g + r
            cs = slice(r * tq, (r + 1) * tq)
            mix_t.append(gates[hd:hd + 1, :] * oc[:, cs] + gates[N_HEADS + hd:N_HEADS + hd + 1, :] * o_s[:, cs]
                         + gates[2 * N_HEADS + hd:2 * N_HEADS + hd + 1, :] * o_w[:, cs])
    for c in range(N_HEADS // 2):
        lanes = slice(c * LANES, (c + 1) * LANES)
        mix = jnp.concatenate([mix_t[2 * c], mix_t[2 * c + 1]], axis=0).T
        o_ref[0, :, lanes] = (zgs_ref[0, :, lanes] * mix).astype(o_ref.dtype)


def _attn_prompt(q, kca, vcb, ksa, vsb, kwa, vwb, gates, zgs, tq):
    b, _, t = q.shape
    nck = kca.shape[2]
    nsb = -(-t // SEL_BLOCK)
    slots = LANES - HEAD_DIM
    assert nsb <= slots
    m2t = np.zeros((slots, nck), np.float32)
    m2t[:nsb] = _imp_matrix(nck, nsb).T
    per_b4 = lambda n, r: pl.BlockSpec((1, n, r, LANES), lambda i, j: (i, 0, 0, 0))
    per_b3 = lambda r, w: pl.BlockSpec((1, r, w), lambda i, j: (i, 0, 0))
    cols = GROUP * tq
    return pl.pallas_call(
        functools.partial(_attn_prompt_kernel, tq=tq, n_sel=min(N_SEL, nsb)),
        grid=(b, t // tq),
        in_specs=[pl.BlockSpec((1, MIX_WIDTH, tq), lambda i, j: (i, 0, j)),
                  per_b4(KV_HEADS, nck), per_b3(KV_WIDTH, nck),
                  per_b4(KV_HEADS, t), per_b3(KV_WIDTH, t),
                  per_b4(KV_HEADS, t), per_b3(KV_WIDTH, t),
                  pl.BlockSpec((1, LANES, tq), lambda i, j: (i, 0, j)),
                  pl.BlockSpec((1, tq, MIX_WIDTH), lambda i, j: (i, j, 0)),
                  pl.BlockSpec(m2t.shape, lambda i, j: (0, 0))],
        out_specs=pl.BlockSpec((1, tq, MIX_WIDTH), lambda i, j: (i, j, 0)),
        out_shape=jax.ShapeDtypeStruct((b, t, MIX_WIDTH), F32),
        scratch_shapes=[pltpu.VMEM((1, cols), F32), pltpu.VMEM((1, cols), F32), pltpu.VMEM((HEAD_DIM, cols), F32),
                        pltpu.VMEM((slots, tq), F32)],
        compiler_params=_cparams("arbitrary", "arbitrary"),
        name="attn_prompt",
    )(q, kca, vcb, ksa, vsb, kwa, vwb, gates, zgs, jnp.asarray(m2t, BF16))


def _online_update(s, vt, m_sc, l_sc, acc_sc):
    m_old = m_sc[...]
    m_new = jnp.maximum(m_old, jnp.max(s, axis=-1, keepdims=True))
    alpha = jnp.exp(m_old - m_new)
    p = jnp.exp(s - m_new)
    l_sc[...] = alpha * l_sc[...] + jnp.sum(p, axis=-1, keepdims=True)
    acc_sc[...] = alpha * acc_sc[...] + _dot_nt(p.astype(BF16), vt)
    m_sc[...] = m_new


def _attn_sample_kernel(pt_ref, q_ref, kca_ref, vcb_ref, ksn_ref, kwn_ref, win_ref, gates_ref, zgs_ref, mmat_ref,
                        *refs, pg, past_len):
    pages = refs[:pg]
    o_ref, qbd_sc, pen_sc, m_sc, l_sc, acc_sc, oc_sc, ow_sc = refs[pg:]
    step = pl.program_id(1)
    t = q_ref.shape[2]
    rows = N_HEADS * t
    rpad = qbd_sc.shape[0]
    npb = mmat_ref.shape[1]
    tok = jnp.bitwise_and(lax.broadcasted_iota(jnp.int32, (rpad, 1), 0), t - 1)
    qpos = past_len + tok

    @pl.when(step == 0)
    def _():
        zero_t = jnp.zeros((t, LANES), F32)
        qrows = []
        for hd in range(N_HEADS):
            g = hd // GROUP
            qh = q_ref[0, hd]
            if g % 2 == 1:
                qh = pltpu.roll(qh, HEAD_DIM, axis=1)
            qrows.append(jnp.concatenate([qh, zero_t] if g // 2 == 0 else [zero_t, qh], axis=1))
        qbd = jnp.concatenate(qrows + [jnp.zeros((rpad - rows, 2 * LANES), F32)], axis=0).astype(BF16)

        imps = []
        rg = GROUP * t
        qpos_g = qpos[0:rg]
        for g in range(KV_HEADS):
            qg = jnp.concatenate([q_ref[0, GROUP * g + r] for r in range(GROUP)], axis=0).astype(BF16)
            sc = _dot_nt(qg, kca_ref[0, g])
            ncp = sc.shape[1]
            blk_end = lax.broadcasted_iota(jnp.int32, (1, ncp), 1) * CMP_STRIDE + (CMP_BLOCK - 1)
            pc = _softmax_masked(sc, blk_end <= qpos_g)
            oc_sc[g * rg:(g + 1) * rg, :] = _dot(pc.astype(BF16), vcb_ref[0])
            imps.append(pc[0:t] + pc[t:2 * t] + pc[2 * t:3 * t])
        oc_sc[rows:, :] = jnp.zeros((rpad - rows, 2 * LANES), F32)
        imp = jnp.concatenate(imps, axis=0)
        imp_s = None
        for part in _split3(imp):
            d = _dot(part, mmat_ref[...])
            imp_s = d if imp_s is None else imp_s + d
        ngt = KV_HEADS * t
        width = npb + LANES
        base = jnp.concatenate([imp_s, jnp.zeros((ngt, LANES), F32)], axis=1)
        j = lax.broadcasted_iota(jnp.int32, (ngt, width), 1)
        cur = jnp.right_shift(past_len + jnp.bitwise_and(lax.broadcasted_iota(jnp.int32, (ngt, width), 0), t - 1), 6)
        valid = j <= cur
        forced = (j == 0) | (j == cur) | (j == cur - 1)
        score = jnp.where(valid, base + jnp.where(forced, FORCE_BONUS, 0.0), -jnp.inf)
        picked = jnp.zeros((ngt, width), F32)
        jf = j.astype(F32)
        for _ in range(N_SEL):
            mx = jnp.max(score, axis=-1, keepdims=True)
            idx = jnp.min(jnp.where(score == mx, jf, float(width)), axis=-1, keepdims=True)
            hit = jf == idx
            picked = jnp.where(hit, 1.0, picked)
            score = jnp.where(hit, -jnp.inf, score)
        pen_gt = jnp.where((picked > 0.5) & valid, 0.0, NEG)
        pen = jnp.concatenate([pen_gt[(hd // GROUP) * t:(hd // GROUP + 1) * t] for hd in range(N_HEADS)]
                              + [jnp.zeros((rpad - rows, width), F32)], axis=0)
        qbd_sc[...] = qbd
        for c in range(width // LANES):
            pen_sc[c * LANES:(c + 1) * LANES, :] = pen[:, c * LANES:(c + 1) * LANES].T.astype(BF16)
        pen_new = pen[:, npb:npb + 1]

        def padded(ref, cols):
            return jnp.concatenate([ref[0, :, cols], jnp.zeros((LANES - t, KV_WIDTH), F32)], axis=0).astype(BF16)

        kcols, vcols = slice(0, KV_WIDTH), slice(KV_WIDTH, 2 * KV_WIDTH)
        inew = lax.broadcasted_iota(jnp.int32, (1, LANES), 1)
        new_ok = (inew < t) & (past_len + inew <= qpos)

        s = jnp.where(new_ok, _dot_nt(qbd, padded(ksn_ref, kcols)) + pen_new, NEG)
        m0 = jnp.maximum(jnp.max(s, axis=-1, keepdims=True), M_INIT)
        p = jnp.exp(s - m0)
        m_sc[...] = m0
        l_sc[...] = jnp.sum(p, axis=-1, keepdims=True)
        acc_sc[...] = _dot(p.astype(BF16), padded(ksn_ref, vcols))

        wk = win_ref.shape[1]
        kw_pos = past_len - wk + lax.broadcasted_iota(jnp.int32, (1, wk), 1)
        mask1 = (kw_pos <= qpos) & (kw_pos > qpos - WINDOW) & (kw_pos >= 0)
        mask2 = new_ok & (past_len + inew > qpos - WINDOW)
        s1 = jnp.where(mask1, _dot_nt(qbd, win_ref[0, :, kcols].astype(BF16)), -1e30)
        s2 = jnp.where(mask2, _dot_nt(qbd, padded(kwn_ref, kcols)), -1e30)
        mw = jnp.maximum(jnp.max(s1, axis=-1, keepdims=True), jnp.max(s2, axis=-1, keepdims=True))
        p1 = jnp.exp(s1 - mw) * mask1.astype(F32)
        p2 = jnp.exp(s2 - mw) * mask2.astype(F32)
        lw = jnp.sum(p1, axis=-1, keepdims=True) + jnp.sum(p2, axis=-1, keepdims=True)
        ow = _dot(p1.astype(BF16), win_ref[0, :, vcols].astype(BF16)) + _dot(p2.astype(BF16), padded(kwn_ref, vcols))
        ow_sc[...] = ow / jnp.maximum(lw, 1e-30)

    bpu = PAGES_PER_UPDATE * (PAGE_SIZE // SEL_BLOCK)
    keys = PAGES_PER_UPDATE * PAGE_SIZE
    onehot = jnp.where(jnp.right_shift(lax.broadcasted_iota(jnp.int32, (bpu, keys), 1), 6)
                       == lax.broadcasted_iota(jnp.int32, (bpu, keys), 0), 1.0, 0.0).astype(BF16)
    for u in range(pg // PAGES_PER_UPDATE):
        blocks = [pages[u * PAGES_PER_UPDATE + i][0] for i in range(PAGES_PER_UPDATE)]
        kt = jnp.concatenate([x[0:KV_WIDTH, :] for x in blocks], axis=1).astype(BF16)
        vt = jnp.concatenate([x[KV_WIDTH:, :] for x in blocks], axis=1).astype(BF16)
        b0 = pl.multiple_of((step * (pg // PAGES_PER_UPDATE) + u) * bpu, bpu)
        bias = lax.dot_general(pen_sc[pl.ds(b0, bpu), :], onehot, (((0,), (0,)), ((), ())), preferred_element_type=F32)
        _online_update(_dot(qbd_sc[...], kt) + bias, vt, m_sc, l_sc, acc_sc)

    @pl.when(step == pl.num_programs(1) - 1)
    def _():
        o_s = acc_sc[...] / l_sc[...]
        lo = _lo_half(t)
        gates = gates_ref[0]
        placed = []
        for hd in range(N_HEADS):
            g = hd // GROUP
            rs = slice(hd * t, (hd + 1) * t)
            cols = slice((g // 2) * LANES, (g // 2 + 1) * LANES)
            mix = (gates[:, hd:hd + 1] * oc_sc[rs, cols] + gates[:, N_HEADS + hd:N_HEADS + hd + 1] * o_s[rs, cols]
                   + gates[:, 2 * N_HEADS + hd:2 * N_HEADS + hd + 1] * ow_sc[rs, cols])
            placed.append(mix if g % 2 == hd % 2 else pltpu.roll(mix, HEAD_DIM, axis=1))
        for c in range(N_HEADS // 2):
            cols = slice(c * LANES, (c + 1) * LANES)
            o_ref[0, :, cols] = zgs_ref[0, :, cols] * jnp.where(lo, placed[2 * c], placed[2 * c + 1])


def _attn_sample(q, kca, vcb, kvs_new, kvw_new, win_state, gates, zgs, cache_sel, page_table, t, pg):
    bd, n_pages = page_table.shape
    past_len = n_pages * PAGE_SIZE
    nck = kca.shape[2]
    npb = past_len // SEL_BLOCK
    mmat = jnp.asarray(_imp_matrix(nck, npb), BF16)
    wk = win_state.shape[1]
    rpad = -(-N_HEADS * t // LANES) * LANES
    tok = lambda w: pl.BlockSpec((1, t, w), lambda b, s, pt: (0, b, 0))
    grid_spec = pltpu.PrefetchScalarGridSpec(
        num_scalar_prefetch=1,
        grid=(bd, n_pages // pg),
        in_specs=[pl.BlockSpec((1, N_HEADS, t, LANES), lambda b, s, pt: (0, 0, b, 0)),
                  pl.BlockSpec((1, KV_HEADS, nck, LANES), lambda b, s, pt: (b, 0, 0, 0)),
                  pl.BlockSpec((1, nck, KV_WIDTH), lambda b, s, pt: (b, 0, 0)),
                  tok(2 * KV_WIDTH), tok(2 * KV_WIDTH),
                  pl.BlockSpec((1, wk, 2 * KV_WIDTH), lambda b, s, pt: (b, 0, 0)),
                  tok(LANES), tok(MIX_WIDTH),
                  pl.BlockSpec(mmat.shape, lambda b, s, pt: (0, 0))] + _page_specs(pg),
        out_specs=tok(MIX_WIDTH),
        scratch_shapes=[pltpu.VMEM((rpad, 2 * LANES), BF16), pltpu.VMEM((npb + LANES, rpad), BF16),
                        pltpu.VMEM((rpad, 1), F32), pltpu.VMEM((rpad, 1), F32),
                        pltpu.VMEM((rpad, 2 * LANES), F32), pltpu.VMEM((rpad, 2 * LANES), F32),
                        pltpu.VMEM((rpad, 2 * LANES), F32)],
    )
    return pl.pallas_call(
        functools.partial(_attn_sample_kernel, pg=pg, past_len=past_len),
        grid_spec=grid_spec,
        out_shape=jax.ShapeDtypeStruct((1, bd * t, MIX_WIDTH), F32),
        compiler_params=_cparams("arbitrary", "arbitrary"),
        name="attn_sample",
    )(page_table, q, kca, vcb, kvs_new, kvw_new, win_state, gates, zgs, mmat, *([cache_sel] * pg))


def _reorder_nsa_weight(w):
    d = w.shape[0]
    n_gate = 3 * N_HEADS
    zg0 = _C_ZG + n_gate
    return jnp.concatenate([w[:, :_C_ZG], w[:, zg0:zg0 + MIX_WIDTH], w[:, _C_ZG:zg0],
                            jnp.zeros((d, LANES - n_gate), w.dtype)], axis=1).astype(BF16)


def _cmp_weights(cmp_pos_w, cmp_phi):
    ratio = CMP_BLOCK // CMP_STRIDE
    pw = cmp_pos_w.reshape(2, ratio, CMP_STRIDE, HEAD_DIM)
    tiles = [jnp.concatenate([jnp.tile(pw[s, m], (1, KV_HEADS)) for s in range(2)], axis=1) for m in range(ratio)]
    eye = jnp.eye(KV_HEADS, dtype=cmp_phi.dtype)
    z = jnp.zeros((KV_WIDTH, KV_WIDTH), cmp_phi.dtype)
    phi_bd = jnp.concatenate([jnp.concatenate([jnp.kron(eye, cmp_phi[0]), z], axis=1),
                              jnp.concatenate([z, jnp.kron(eye, cmp_phi[1])], axis=1)], axis=0).astype(BF16)
    return tiles[0], tiles[1], phi_bd


def _pages_feature_major(cache):
    n_pool, rows = cache.shape[:2]
    return jnp.transpose(cache, (0, 2, 3, 4, 1)).reshape(n_pool, 2 * KV_WIDTH, rows)


def kernel(x_prompt, x_sample, mem_prompt, cache_mem_kv, cache_cmp_kv, cache_sel_kv, page_table, state_conv, state_win_kv, norm_g, final_norm_g, mem_norm_g, w_mem_kv, w_in_conv, conv_w, w_in_nsa, cmp_pos_w, cmp_phi, w_out):
    b, t, d = x_prompt.shape
    bd, td, _ = x_sample.shape
    n_mem = mem_prompt.shape[1]
    n_pages = page_table.shape[1]
    past_len = n_pages * PAGE_SIZE
    assert w_in_conv.shape[0] == 1 and w_in_nsa.shape[0] == 1 and w_out.shape[0] == 2
    assert CMP_BLOCK == 2 * CMP_STRIDE and td & (td - 1) == 0 and td < CMP_STRIDE

    w_conv_b = w_in_conv[0].astype(BF16)
    w_nsa_b = _reorder_nsa_weight(w_in_nsa[0])
    w_out_b = w_out.astype(BF16)
    w_mem_b = w_mem_kv.astype(BF16)
    pw0, pw1, phi_bd = _cmp_weights(cmp_pos_w[0], cmp_phi[0])
    kv6 = lambda a, n, r: a.reshape(1, n, r, 2, KV_HEADS, HEAD_DIM)

    mem_kv_p = _memory_kv(mem_prompt, mem_norm_g, w_mem_b)
    xp = x_prompt.reshape(b * t, d)
    tm = min(512, t)
    mqg, mix, conv_p = _proj_conv(xp, norm_g[0], w_conv_b, conv_w[0], seg=t, tm=tm)
    x1 = _layer_out(xp, mqg, mem_kv_p[0], mix, w_out_b[0], tm=tm, seg=t)
    tabs = _rope_tables(jnp.arange(t, dtype=jnp.int32))
    tq = min(256, t)
    (mqg, q, kvc_p, kvs_p, kvw_p, gates, zgs, ksa, kwa, vsb, vwb) = _proj_nsa(
        x1.reshape(b, t, d), norm_g[1], w_nsa_b, tabs, tm=tq, prompt=True)
    kca, vcb = _cmp_prompt(kvc_p, pw0, pw1, phi_bd)
    mix = _attn_prompt(q, kca, vcb, ksa, vsb, kwa, vwb, gates, zgs, tq=tq)
    y_prompt = _layer_out(x1, mqg.reshape(b * t, 512), mem_kv_p[1], mix.reshape(b * t, MIX_WIDTH), w_out_b[1],
                          tm=tm, seg=t, final_g=final_norm_g).reshape(b, t, d)
    w_keep_p = min(WINDOW, t)

    ms = bd * td
    xs = x_sample.reshape(ms, d)
    st = state_conv[0]
    zrow = jnp.zeros((bd, td - 1, MIX_WIDTH), F32)
    s1 = jnp.concatenate([st[:, 1:2], zrow], axis=1).reshape(ms, MIX_WIDTH)
    s2 = jnp.concatenate([st, zrow[:, 1:]], axis=1).reshape(ms, MIX_WIDTH)
    mqg, mix, u_s = _proj_conv(xs, norm_g[0], w_conv_b, conv_w[0], seg=td, tm=ms, state=(s1, s2))
    conv_s = u_s.reshape(bd, td, MIX_WIDTH)[:, td - 2:]
    tmo = 8 * td
    x1s = _layer_out(xs, mqg, cache_mem_kv[0].reshape(bd, n_mem, 2 * MEM_WIDTH), mix, w_out_b[0], tm=tmo, seg=td)
    tabs_s = _rope_tables(jnp.tile(past_len + jnp.arange(td, dtype=jnp.int32), bd))
    (mqg, q_s, kvc_s, kvs_s, kvw_s, gates_s, zgs_s) = _proj_nsa(
        x1s.reshape(1, ms, d), norm_g[1], w_nsa_b, tabs_s, tm=ms, prompt=False)
    pg = 16
    kca_s, vcb_s = _cmp_sample(_pages_feature_major(cache_cmp_kv[0]), page_table, pw0, pw1, phi_bd, pg)
    win_state = state_win_kv[0].reshape(bd, -1, 2 * KV_WIDTH)
    mix_s = _attn_sample(q_s, kca_s, vcb_s, kvs_s, kvw_s, win_state, gates_s, zgs_s,
                         _pages_feature_major(cache_sel_kv[0]), page_table, td, pg)
    y_sample = _layer_out(x1s, mqg.reshape(ms, 512), cache_mem_kv[1].reshape(bd, n_mem, 2 * MEM_WIDTH),
                          mix_s.reshape(ms, MIX_WIDTH), w_out_b[1], tm=tmo, seg=td, final_g=final_norm_g).reshape(bd, td, d)
    w_keep = win_state.shape[1]
    win_s = jnp.concatenate([win_state, kvw_s.reshape(bd, td, 2 * KV_WIDTH)], axis=1)[:, -w_keep:]

    return (y_prompt, y_sample, conv_p[None],
            kv6(kvc_p, b, t), kv6(kvs_p, b, t), kv6(kvw_p[:, t - w_keep_p:], b, w_keep_p),
            mem_kv_p.reshape(2, b, n_mem, 2, MEM_WIDTH // HEAD_DIM, HEAD_DIM),
            conv_s[None], kv6(kvc_s, bd, td), kv6(kvs_s, bd, td), kv6(win_s, bd, w_keep))
```

```python
import functools

import numpy as np
import jax
import jax.numpy as jnp
from jax import lax
from jax.experimental import pallas as pl
from jax.experimental.pallas import tpu as pltpu

F32 = jnp.float32
BF16 = jnp.bfloat16

D_MODEL = 1024
HEAD_DIM = 64
MIX_WIDTH = 768
MEM_WIDTH = 256
N_HEADS = 12
KV_HEADS = 4
GROUP = 3
KV_WIDTH = 256
CMP_BLOCK = 32
CMP_STRIDE = 16
SEL_BLOCK = 64
N_SEL = 16
WINDOW = 512
ROT_DIM = 16
ROPE_THETA = 500000.0
NORM_EPS = 1e-6
FORCE_BONUS = 1e4
PAGE_SIZE = 128
SCALE = HEAD_DIM ** -0.5
NEG = -(2.0 ** 100)
M_INIT = -1e30
LANES = 128
SUM_ROWS = 16
PAGES_PER_UPDATE = 8
VMEM_LIMIT = 56 * 2 ** 20


def _cparams(*sem):
    return pltpu.CompilerParams(dimension_semantics=sem, vmem_limit_bytes=VMEM_LIMIT)


def _dot(a, b):
    return jnp.dot(a, b, preferred_element_type=F32)


def _dot_nt(a, b):
    return lax.dot_general(a, b, (((1,), (1,)), ((), ())), preferred_element_type=F32)


def _rms(x, g):
    return x * lax.rsqrt(jnp.mean(x * x, axis=-1, keepdims=True) + NORM_EPS) * g


def _silu(x):
    return x * jax.nn.sigmoid(x)


def _split3(a):
    hi = a.astype(BF16)
    r1 = a - hi.astype(F32)
    mid = r1.astype(BF16)
    lo = (r1 - mid.astype(F32)).astype(BF16)
    return hi, mid, lo


def _lo_half(rows):
    return lax.broadcasted_iota(jnp.int32, (rows, LANES), 1) < HEAD_DIM


def _memkv_kernel(mem_ref, g_ref, w_ref, o_ref):
    h = _rms(mem_ref[0], g_ref[0]).astype(BF16)
    o_ref[0, 0] = _dot(h, w_ref[0])


def _memory_kv(mem, mem_norm_g, w_mem_kv_b):
    depth = w_mem_kv_b.shape[0]
    b, n_mem, d = mem.shape
    return pl.pallas_call(
        _memkv_kernel,
        grid=(depth, b),
        in_specs=[pl.BlockSpec((1, n_mem, d), lambda i, j: (j, 0, 0)),
                  pl.BlockSpec((1, 1, d), lambda i, j: (i, 0, 0)),
                  pl.BlockSpec((1, d, 2 * MEM_WIDTH), lambda i, j: (i, 0, 0))],
        out_specs=pl.BlockSpec((1, 1, n_mem, 2 * MEM_WIDTH), lambda i, j: (i, j, 0, 0)),
        out_shape=jax.ShapeDtypeStruct((depth, b, n_mem, 2 * MEM_WIDTH), F32),
        compiler_params=_cparams("arbitrary", "arbitrary"),
        name="memory_kv",
    )(mem, mem_norm_g.reshape(depth, 1, d), w_mem_kv_b)


def _proj_conv_kernel(*refs, tm, seg, has_state):
    if has_state:
        x_ref, g_ref, w_ref, cw_ref, s1_ref, s2_ref, mqg_ref, mix_ref, st_ref, ubuf = refs
    else:
        x_ref, g_ref, w_ref, cw_ref, mqg_ref, mix_ref, st_ref, ubuf = refs
    i = pl.program_id(0)
    h = _rms(x_ref[...], g_ref[...]).astype(BF16)
    mqg_ref[...] = _dot(h, w_ref[:, 0:512])
    bg = _dot(h, w_ref[:, 512:1280])
    cg = _dot(h, w_ref[:, 1280:2048])
    hin = _dot(h, w_ref[:, 2048:2816])
    zg = _dot(h, w_ref[:, 2816:3584])
    u = cg * hin

    @pl.when(i == 0)
    def _():
        ubuf[0:8, :] = jnp.zeros((8, MIX_WIDTH), F32)

    @pl.when(i > 0)
    def _():
        ubuf[0:8, :] = ubuf[tm:tm + 8, :]

    ubuf[8:8 + tm, :] = u
    u1 = ubuf[7:7 + tm, :]
    u2 = ubuf[6:6 + tm, :]
    rowpos = lax.rem(i * tm + lax.broadcasted_iota(jnp.int32, (tm, 1), 0), seg)
    if has_state:
        u1 = jnp.where(rowpos >= 1, u1, s1_ref[...])
        u2 = jnp.where(rowpos >= 2, u2, s2_ref[...])
        st_ref[...] = u
    else:
        u1 = jnp.where(rowpos >= 1, u1, 0.0)
        u2 = jnp.where(rowpos >= 2, u2, 0.0)
        st_ref[0] = ubuf[8 + tm - 2:8 + tm, :]
    y = cw_ref[0:1, :] * u2 + cw_ref[1:2, :] * u1 + cw_ref[2:3, :] * u
    mix_ref[...] = (_silu(zg) * (bg * y)).astype(mix_ref.dtype)


def _proj_conv(x2d, g, w_b, cw, seg, tm, state=None):
    m = x2d.shape[0]
    n_in = w_b.shape[1]
    has_state = state is not None
    in_specs = [pl.BlockSpec((tm, D_MODEL), lambda i: (i, 0)),
                pl.BlockSpec((1, D_MODEL), lambda i: (0, 0)),
                pl.BlockSpec((D_MODEL, n_in), lambda i: (0, 0)),
                pl.BlockSpec((3, MIX_WIDTH), lambda i: (0, 0))]
    args = [x2d, g.reshape(1, D_MODEL), w_b, cw]
    if has_state:
        in_specs += [pl.BlockSpec((tm, MIX_WIDTH), lambda i: (i, 0))] * 2
        args += list(state)
        st_shape = jax.ShapeDtypeStruct((m, MIX_WIDTH), F32)
        st_spec = pl.BlockSpec((tm, MIX_WIDTH), lambda i: (i, 0))
    else:
        st_shape = jax.ShapeDtypeStruct((m // seg, 2, MIX_WIDTH), F32)
        st_spec = pl.BlockSpec((1, 2, MIX_WIDTH), lambda i: ((i * tm) // seg, 0, 0))
    return pl.pallas_call(
        functools.partial(_proj_conv_kernel, tm=tm, seg=seg, has_state=has_state),
        grid=(m // tm,),
        in_specs=in_specs,
        out_specs=[pl.BlockSpec((tm, 512), lambda i: (i, 0)),
                   pl.BlockSpec((tm, MIX_WIDTH), lambda i: (i, 0)),
                   st_spec],
        out_shape=[jax.ShapeDtypeStruct((m, 512), F32),
                   jax.ShapeDtypeStruct((m, MIX_WIDTH), F32),
                   st_shape],
        scratch_shapes=[pltpu.VMEM((tm + 8, MIX_WIDTH), F32)],
        compiler_params=_cparams("arbitrary"),
        name="proj_conv",
    )(*args)


def _out_kernel(*refs, nseg, seg, final, feature_major):
    if final:
        x_ref, mqg_ref, memkv_ref, mix_ref, w_ref, g_ref, o_ref, mo_sc = refs
    else:
        x_ref, mqg_ref, memkv_ref, mix_ref, w_ref, o_ref, mo_sc = refs
    lo = _lo_half(seg)

    def seg_body(s, r0):
        for pair in range(2):
            cols = slice(pair * LANES, (pair + 1) * LANES)
            qp = mqg_ref[pl.ds(r0, seg), cols]
            vcols = slice(MEM_WIDTH + pair * LANES, MEM_WIDTH + (pair + 1) * LANES)
            if feature_major:
                kp, vp = memkv_ref[s, cols, :].astype(BF16), memkv_ref[s, vcols, :].astype(BF16)
            else:
                kp, vp = memkv_ref[s, :, cols].astype(BF16), memkv_ref[s, :, vcols].astype(BF16)
            outs = []
            for half in range(2):
                keep = lo if half == 0 else jnp.logical_not(lo)
                qm = jnp.where(keep, qp, 0.0).astype(BF16)
                sc = (_dot(qm, kp) if feature_major else _dot_nt(qm, kp)) * SCALE
                e = jnp.exp(sc - jnp.max(sc, axis=-1, keepdims=True))
                p = e / jnp.sum(e, axis=-1, keepdims=True)
                outs.append(_dot_nt(p.astype(BF16), vp) if feature_major else _dot(p.astype(BF16), vp))
            mo_sc[pl.ds(r0, seg), cols] = jnp.where(lo, outs[0], outs[1])

    if nseg == 1:
        seg_body(0, 0)
    else:
        def body(s, c):
            seg_body(s, pl.multiple_of(s * seg, seg))
            return c
        lax.fori_loop(0, nseg, body, 0)

    mg = mqg_ref[:, MEM_WIDTH:2 * MEM_WIDTH]
    a = (_silu(mg) * mo_sc[...]).astype(BF16)
    xn = x_ref[...] + (_dot(a, w_ref[0:MEM_WIDTH, :]) + _dot(mix_ref[...].astype(BF16), w_ref[MEM_WIDTH:, :]))
    if final:
        o_ref[...] = _rms(xn, g_ref[...])
    else:
        o_ref[...] = xn


def _layer_out(x2d, mqg, memkv, mix, w_out_b, tm, seg, final_g=None, feature_major=False, memkv_offset=0):
    m = x2d.shape[0]
    nseg = max(tm // seg, 1)
    seg_in = min(seg, tm)
    final = final_g is not None
    in_specs = [pl.BlockSpec((tm, D_MODEL), lambda i: (i, 0)),
                pl.BlockSpec((tm, 512), lambda i: (i, 0)),
                pl.BlockSpec((nseg,) + memkv.shape[1:], lambda i: (memkv_offset // nseg + (i * tm) // (seg * nseg), 0, 0)),
                pl.BlockSpec((tm, MIX_WIDTH), lambda i: (i, 0)),
                pl.BlockSpec((D_MODEL, D_MODEL), lambda i: (0, 0))]
    args = [x2d, mqg, memkv, mix, w_out_b]
    if final:
        in_specs.append(pl.BlockSpec((1, D_MODEL), lambda i: (0, 0)))
        args.append(final_g.reshape(1, D_MODEL))
    return pl.pallas_call(
        functools.partial(_out_kernel, nseg=nseg, seg=seg_in, final=final, feature_major=feature_major),
        grid=(m // tm,),
        in_specs=in_specs,
        out_specs=pl.BlockSpec((tm, D_MODEL), lambda i: (i, 0)),
        out_shape=jax.ShapeDtypeStruct((m, D_MODEL), F32),
        scratch_shapes=[pltpu.VMEM((tm, MEM_WIDTH), F32)],
        compiler_params=_cparams("arbitrary"),
        name="layer_out",
    )(*args)


_C_Q, _C_KC, _C_KS, _C_KW, _C_ZG, _C_GL, _C_END = 512, 1280, 1792, 2304, 2816, 3584, 3712


def _proj_nsa_kernel(*refs, tm, prompt):
    (x_ref, g_ref, w_ref, c_ref, sa_ref, sb_ref,
     mqg_ref, q_ref, kvc_ref, kvs_ref, kvw_ref, gates_ref, zgs_ref) = refs[:13]
    ti = pl.program_id(1)
    h = _rms(x_ref[0], g_ref[...]).astype(BF16)
    cos, sa, sb = c_ref[...], sa_ref[...], sb_ref[...]
    lo = _lo_half(tm)

    def rope(chunk):
        return chunk * cos + pltpu.roll(chunk, LANES - ROT_DIM // 2, axis=1) * sa + pltpu.roll(chunk, ROT_DIM // 2, axis=1) * sb

    mqg_ref[0] = _dot(h, w_ref[:, 0:_C_Q])
    qf = _dot(h, w_ref[:, _C_Q:_C_KC])
    for c in range(N_HEADS // 2):
        qc = rope(qf[:, c * LANES:(c + 1) * LANES]) * SCALE
        if prompt:
            q_ref[0, c * LANES:(c + 1) * LANES, :] = qc.T.astype(q_ref.dtype)
        else:
            q_ref[0, 2 * c] = jnp.where(lo, qc, 0.0).astype(q_ref.dtype)
            q_ref[0, 2 * c + 1] = jnp.where(lo, pltpu.roll(qc, HEAD_DIM, axis=1), 0.0).astype(q_ref.dtype)

    if prompt:
        ksa_ref, kwa_ref, vsb_ref, vwb_ref = refs[13:17]
        pos = ti * tm + lax.broadcasted_iota(jnp.int32, (tm, LANES), 0)
        lane = lax.broadcasted_iota(jnp.int32, (tm, LANES), 1)
        onehot = jnp.where(jnp.right_shift(pos, 6) == lane - HEAD_DIM, 1.0, 0.0)

    for name, off in (("c", _C_KC), ("s", _C_KS), ("w", _C_KW)):
        kv_ref = {"c": kvc_ref, "s": kvs_ref, "w": kvw_ref}[name]
        kk = _dot(h, w_ref[:, off:off + KV_WIDTH])
        vv = _dot(h, w_ref[:, off + KV_WIDTH:off + 2 * KV_WIDTH])
        kv_ref[0, :, KV_WIDTH:2 * KV_WIDTH] = vv
        for c in range(KV_HEADS // 2):
            kr = rope(kk[:, c * LANES:(c + 1) * LANES])
            kv_ref[0, :, c * LANES:(c + 1) * LANES] = kr
            if prompt and name != "c":
                pad = onehot if name == "s" else 0.0
                aug_ref = ksa_ref if name == "s" else kwa_ref
                aug_ref[0, 2 * c] = jnp.where(lo, kr, pad).astype(BF16)
                aug_ref[0, 2 * c + 1] = jnp.where(lo, pltpu.roll(kr, HEAD_DIM, axis=1), pad).astype(BF16)
        if prompt and name != "c":
            vt_ref = vsb_ref if name == "s" else vwb_ref
            for c in range(KV_WIDTH // LANES):
                vt_ref[0, c * LANES:(c + 1) * LANES, :] = vv[:, c * LANES:(c + 1) * LANES].T.astype(BF16)

    gates = jax.nn.sigmoid(_dot(h, w_ref[:, _C_GL:_C_END]))
    gates_ref[0] = gates.T if prompt else gates
    zgs_ref[0] = _silu(_dot(h, w_ref[:, _C_ZG:_C_GL]))


def _proj_nsa(x3d, g, w_b, tabs, tm, prompt):
    nb, t, _ = x3d.shape
    row = lambda w: pl.BlockSpec((1, tm, w), lambda b, i: (b, i, 0))
    head = lambda n: pl.BlockSpec((1, n, tm, LANES), lambda b, i: (b, 0, i, 0))
    tab = pl.BlockSpec((tm, LANES), lambda b, i: (i, 0))
    sds = jax.ShapeDtypeStruct
    col = lambda w: pl.BlockSpec((1, w, tm), lambda b, i: (b, 0, i))
    if prompt:
        q_spec, q_shape = col(MIX_WIDTH), sds((nb, MIX_WIDTH, t), BF16)
        g_spec, g_shape = col(LANES), sds((nb, LANES, t), F32)
    else:
        q_spec, q_shape = head(N_HEADS), sds((nb, N_HEADS, t, LANES), F32)
        g_spec, g_shape = row(LANES), sds((nb, t, LANES), F32)
    out_specs = [row(512), q_spec, row(512), row(512), row(512), g_spec, row(MIX_WIDTH)]
    out_shape = [sds((nb, t, 512), F32), q_shape,
                 sds((nb, t, 512), F32), sds((nb, t, 512), F32), sds((nb, t, 512), F32),
                 g_shape, sds((nb, t, MIX_WIDTH), F32)]
    if prompt:
        out_specs += [head(KV_HEADS), head(KV_HEADS), col(KV_WIDTH), col(KV_WIDTH)]
        out_shape += [sds((nb, KV_HEADS, t, LANES), BF16), sds((nb, KV_HEADS, t, LANES), BF16),
                      sds((nb, KV_WIDTH, t), BF16), sds((nb, KV_WIDTH, t), BF16)]
    return pl.pallas_call(
        functools.partial(_proj_nsa_kernel, tm=tm, prompt=prompt),
        grid=(nb, t // tm),
        in_specs=[pl.BlockSpec((1, tm, D_MODEL), lambda b, i: (b, i, 0)),
                  pl.BlockSpec((1, D_MODEL), lambda b, i: (0, 0)),
                  pl.BlockSpec((D_MODEL, _C_END), lambda b, i: (0, 0)),
                  tab, tab, tab],
        out_specs=out_specs,
        out_shape=out_shape,
        compiler_params=_cparams("arbitrary", "arbitrary"),
        name="proj_nsa",
    )(x3d, g.reshape(1, D_MODEL), w_b, *tabs)


def _rope_tables(pos):
    half = ROT_DIM // 2
    inv = ROPE_THETA ** (-jnp.arange(half, dtype=F32) * 2.0 / ROT_DIM)
    ang = pos.astype(F32)[:, None] * inv[None, :]
    cos, sin = jnp.cos(ang), jnp.sin(ang)
    n = pos.shape[0]
    one = jnp.ones((n, HEAD_DIM - ROT_DIM), F32)
    zero = jnp.zeros((n, HEAD_DIM - ROT_DIM), F32)
    zh = jnp.zeros((n, half), F32)
    c = jnp.concatenate([cos, cos, one], axis=1)
    sa = jnp.concatenate([-sin, zh, zero], axis=1)
    sb = jnp.concatenate([zh, sin, zero], axis=1)
    return tuple(jnp.tile(a, (1, LANES // HEAD_DIM)) for a in (c, sa, sb))


def _emit_cmp_blocks(out, kca_ref, v_ref, transpose_v):
    n = out.shape[0]
    lo = _lo_half(n)
    for c in range(KV_HEADS // 2):
        kc = out[:, c * LANES:(c + 1) * LANES]
        kca_ref[0, 2 * c] = jnp.where(lo, kc, 0.0).astype(BF16)
        kca_ref[0, 2 * c + 1] = jnp.where(lo, pltpu.roll(kc, HEAD_DIM, axis=1), 0.0).astype(BF16)
        if transpose_v:
            v_ref[0, c * LANES:(c + 1) * LANES, :] = out[:, KV_WIDTH + c * LANES:KV_WIDTH + (c + 1) * LANES].T.astype(BF16)
    if not transpose_v:
        v_ref[0] = out[:, KV_WIDTH:].astype(BF16)


def _cmp_prompt_kernel(kvc_ref, pw0_ref, pw1_ref, phi_ref, kca_ref, vcb_ref, p0_sc, p1_sc, *, rows):
    t = kvc_ref.shape[1]
    cpr = rows // CMP_STRIDE
    pw0 = pw0_ref[...][None]
    pw1 = pw1_ref[...][None]
    for i in range(t // rows):
        x = kvc_ref[0, i * rows:(i + 1) * rows, :].reshape(cpr, CMP_STRIDE, 2 * KV_WIDTH)
        p0_sc[i * cpr:(i + 1) * cpr, :] = jnp.sum(x * pw0, axis=1)
        p1_sc[i * cpr:(i + 1) * cpr, :] = jnp.sum(x * pw1, axis=1)
    nck = t // CMP_STRIDE
    blk = p0_sc[...] + pltpu.roll(p1_sc[...], nck - 1, axis=0)
    _emit_cmp_blocks(_dot(blk.astype(BF16), phi_ref[...]), kca_ref, vcb_ref, transpose_v=True)


def _cmp_prompt(kvc, pw0, pw1, phi_bd):
    b, t, _ = kvc.shape
    nck = t // CMP_STRIDE
    full = lambda s: pl.BlockSpec(s, lambda i: (0,) * len(s))
    return pl.pallas_call(
        functools.partial(_cmp_prompt_kernel, rows=512),
        grid=(b,),
        in_specs=[pl.BlockSpec((1, t, 2 * KV_WIDTH), lambda i: (i, 0, 0)),
                  full((CMP_STRIDE, 2 * KV_WIDTH)), full((CMP_STRIDE, 2 * KV_WIDTH)),
                  full((2 * KV_WIDTH, 2 * KV_WIDTH))],
        out_specs=[pl.BlockSpec((1, KV_HEADS, nck, LANES), lambda i: (i, 0, 0, 0)),
                   pl.BlockSpec((1, KV_WIDTH, nck), lambda i: (i, 0, 0))],
        out_shape=[jax.ShapeDtypeStruct((b, KV_HEADS, nck, LANES), BF16),
                   jax.ShapeDtypeStruct((b, KV_WIDTH, nck), BF16)],
        scratch_shapes=[pltpu.VMEM((nck, 2 * KV_WIDTH), F32), pltpu.VMEM((nck, 2 * KV_WIDTH), F32)],
        compiler_params=_cparams("arbitrary"),
        name="cmp_prompt",
    )(kvc, pw0, pw1, phi_bd)


def _cmp_sample_kernel(pt_ref, w0_ref, w1_ref, seg_ref, phi_ref, *refs, pg):
    pages = refs[:pg]
    kca_ref, vcb_ref, p0_sc, p1_sc = refs[pg:]
    step = pl.program_id(1)
    cpp = PAGE_SIZE // CMP_STRIDE
    for k in range(pg):
        xt = pages[k][0]
        r0 = pl.multiple_of((step * pg + k) * cpp, cpp)
        for w_ref, p_sc in ((w0_ref, p0_sc), (w1_ref, p1_sc)):
            y = (xt * w_ref[...]).astype(BF16)
            p_sc[pl.ds(r0, cpp), :] = _dot_nt(seg_ref[...], y)[0:cpp]

    @pl.when(step == pl.num_programs(1) - 1)
    def _():
        nck = p0_sc.shape[0]
        blk = p0_sc[...] + pltpu.roll(p1_sc[...], nck - 1, axis=0)
        _emit_cmp_blocks(_dot(blk.astype(BF16), phi_ref[...]), kca_ref, vcb_ref, transpose_v=False)


def _page_specs(pg):
    return [pl.BlockSpec((1, 2 * KV_WIDTH, PAGE_SIZE), functools.partial(
        lambda b, s, pt, k: (pt[b, s * pg + k], 0, 0), k=k)) for k in range(pg)]


def _cmp_sample(cache_t, page_table, pw0, pw1, phi_bd, pg):
    bd, n_pages = page_table.shape
    nck = n_pages * PAGE_SIZE // CMP_STRIDE
    cpp = PAGE_SIZE // CMP_STRIDE
    w0, w1 = (jnp.tile(pw.T, (1, cpp)) for pw in (pw0, pw1))
    seg = np.zeros((max(cpp, 16), PAGE_SIZE), np.float32)
    seg[np.arange(PAGE_SIZE) // CMP_STRIDE, np.arange(PAGE_SIZE)] = 1.0
    full = lambda s: pl.BlockSpec(s, lambda b, i, pt: (0,) * len(s))
    grid_spec = pltpu.PrefetchScalarGridSpec(
        num_scalar_prefetch=1,
        grid=(bd, n_pages // pg),
        in_specs=[full(w0.shape), full(w1.shape), full(seg.shape),
                  full((2 * KV_WIDTH, 2 * KV_WIDTH))] + _page_specs(pg),
        out_specs=[pl.BlockSpec((1, KV_HEADS, nck, LANES), lambda b, i, pt: (b, 0, 0, 0)),
                   pl.BlockSpec((1, nck, KV_WIDTH), lambda b, i, pt: (b, 0, 0))],
        scratch_shapes=[pltpu.VMEM((nck, 2 * KV_WIDTH), F32), pltpu.VMEM((nck, 2 * KV_WIDTH), F32)],
    )
    return pl.pallas_call(
        functools.partial(_cmp_sample_kernel, pg=pg),
        grid_spec=grid_spec,
        out_shape=[jax.ShapeDtypeStruct((bd, KV_HEADS, nck, LANES), BF16),
                   jax.ShapeDtypeStruct((bd, nck, KV_WIDTH), BF16)],
        compiler_params=_cparams("arbitrary", "arbitrary"),
        name="cmp_sample",
    )(page_table, w0, w1, jnp.asarray(seg, BF16), phi_bd, *([cache_t] * pg))


def _imp_matrix(nck, n_sel_blocks):
    cps = SEL_BLOCK // CMP_STRIDE
    c = np.arange(nck)[:, None]
    j = np.arange(n_sel_blocks)[None, :]
    m = ((c >= cps * j) & (c <= cps * j + cps - 1)).astype(np.float32)
    m += ((c + 1 >= cps * j) & (c + 1 <= cps * j + cps - 1)).astype(np.float32)
    return m


def _softmax_masked(s, mask, axis=-1):
    s = jnp.where(mask, s, -1e30)
    p = jnp.exp(s - jnp.max(s, axis=axis, keepdims=True)) * mask.astype(F32)
    return p / jnp.maximum(jnp.sum(p, axis=axis, keepdims=True), 1e-30)


def _online_update_t(s, vt_ones, m_sc, acc_sc):
    m_old = m_sc[...]
    m_new = jnp.maximum(m_old, jnp.max(s, axis=0, keepdims=True))
    p = jnp.exp((s - m_new).astype(BF16))
    acc_sc[...] = jnp.exp(m_old - m_new) * acc_sc[...] + _dot(vt_ones, p)
    m_sc[...] = m_new


def _reset_online(m_sc, l_sc, acc_sc):
    m_sc[...] = jnp.full(m_sc.shape, M_INIT, F32)
    l_sc[...] = jnp.zeros(l_sc.shape, F32)
    acc_sc[...] = jnp.zeros(acc_sc.shape, F32)


def _attn_prompt_kernel(q_ref, kca_ref, vcb_ref, ksa_ref, vsb_ref, kwa_ref, vwb_ref, gates_ref, zgs_ref,
                        m2t_ref, o_ref, m_sc, acc_sc, score_sc, sa_sc, sb_sc, *, tq, n_sel):
    qi = pl.program_id(1)
    s0 = qi * tq
    cols = GROUP * tq
    nsb = m2t_ref.shape[0]
    qpos = s0 + lax.broadcasted_iota(jnp.int32, (1, tq), 1)
    qpos3 = jnp.concatenate([qpos] * GROUP, axis=1)
    gates = gates_ref[0]
    zero_h = jnp.zeros((HEAD_DIM, tq), BF16)
    mix_t = []
    for g in range(KV_HEADS):
        vrows = slice(g * HEAD_DIM, (g + 1) * HEAD_DIM)
        qh = [q_ref[0, (GROUP * g + r) * HEAD_DIM:(GROUP * g + r + 1) * HEAD_DIM, :] for r in range(GROUP)]
        qc = jnp.concatenate([jnp.concatenate([x, zero_h], axis=0) for x in qh], axis=1)

        sc = _dot(kca_ref[0, g], qc)
        ncp = sc.shape[0]
        blk_end = lax.broadcasted_iota(jnp.int32, (ncp, 1), 0) * CMP_STRIDE + (CMP_BLOCK - 1)
        pc = _softmax_masked(sc, blk_end <= qpos3, axis=0)
        oc = _dot(vcb_ref[0, vrows, :], pc.astype(BF16))
        imp = pc[:, 0:tq] + pc[:, tq:2 * tq] + pc[:, 2 * tq:3 * tq]

        imp_t = None
        for part in _split3(imp):
            d = _dot(m2t_ref[...], part)
            imp_t = d if imp_t is None else imp_t + d
        j = lax.broadcasted_iota(jnp.int32, (nsb, tq), 0)
        cur = jnp.right_shift(s0 + lax.broadcasted_iota(jnp.int32, (nsb, tq), 1), 6)
        valid = j <= cur
        forced = (j == 0) | (j == cur) | (j == cur - 1)
        score = jnp.where(valid, imp_t + jnp.where(forced, FORCE_BONUS, 0.0), -jnp.inf)
        score_sc[...] = score
        bpt = tq // SEL_BLOCK

        def rank_body(it, cnt):
            for u in range(bpt):
                i = it * bpt + u
                row = score_sc[pl.ds(i, 1), :]
                tie = jnp.where(j > i, 1.0, 0.0)
                cnt = cnt + jnp.where(row > score, 1.0, jnp.where(row == score, tie, 0.0))
            return cnt

        cnt = lax.fori_loop(0, jnp.minimum(qi + 1, nsb // bpt), rank_body, jnp.zeros((nsb, tq), F32))
        sel = (cnt < float(n_sel)) & valid
        pen = jnp.where(sel, 0.0, NEG).astype(BF16)
        qa = jnp.concatenate([jnp.concatenate([x, pen], axis=0) for x in qh], axis=1)

        def tile_start(kt):
            return pl.multiple_of(kt * tq, tq)

        def key_pos(kt):
            return tile_start(kt) + lax.broadcasted_iota(jnp.int32, (tq, 1), 0)

        def run_branch(scores, values, first, last_mask):
            m_sc[...] = jnp.full(m_sc.shape, M_INIT, F32)
            acc_sc[...] = jnp.zeros(acc_sc.shape, F32)
            ones = jnp.ones((SUM_ROWS, tq), BF16)

            def update(s_ref, kt, mask=None):
                s = s_ref[...] if mask is None else mask(s_ref[...])
                _online_update_t(s, jnp.concatenate([values(kt), ones], axis=0), m_sc, acc_sc)

            def pair(i, c):
                kt = first + 2 * i
                sb_sc[...] = scores(kt + 1)
                update(sa_sc, kt)
                sa_sc[...] = scores(kt + 2)
                update(sb_sc, kt + 1)
                return c

            ahead = qi - first
            sa_sc[...] = scores(first)
            lax.fori_loop(0, ahead // 2, pair, 0)

            @pl.when(ahead % 2 == 1)
            def _():
                sb_sc[...] = scores(qi)
                update(sa_sc, qi - 1)
                update(sb_sc, qi, last_mask)

            @pl.when(ahead % 2 == 0)
            def _():
                update(sa_sc, qi, last_mask)

            return acc_sc[0:HEAD_DIM, :] / acc_sc[HEAD_DIM:HEAD_DIM + 1, :]

        o_s = run_branch(
            lambda kt: _dot(ksa_ref[0, g, pl.ds(tile_start(kt), tq), :], qa),
            lambda kt: vsb_ref[0, vrows, pl.ds(tile_start(kt), tq)],
            0, lambda s: jnp.where(key_pos(qi) <= qpos3, s, NEG))

        def win_scores(kt):
            kpos = key_pos(kt)
            s = _dot(kwa_ref[0, g, pl.ds(tile_start(kt), tq), :], qc)
            return jnp.where((kpos <= qpos3) & (kpos > qpos3 - WINDOW), s, NEG)

        o_w = run_branch(win_scores, lambda kt: vwb_ref[0, vrows, pl.ds(tile_start(kt), tq)],
                         jnp.maximum(qi - WINDOW // tq, 0), lambda s: s)

        for r in range(GROUP):
            hd = GROUP * g + r
            cs = slice(r * tq, (r + 1) * tq)
            mix_t.append(gates[hd:hd + 1, :] * oc[:, cs] + gates[N_HEADS + hd:N_HEADS + hd + 1, :] * o_s[:, cs]
                         + gates[2 * N_HEADS + hd:2 * N_HEADS + hd + 1, :] * o_w[:, cs])
    for c in range(N_HEADS // 2):
        lanes = slice(c * LANES, (c + 1) * LANES)
        mix = jnp.concatenate([mix_t[2 * c], mix_t[2 * c + 1]], axis=0).T
        o_ref[0, :, lanes] = (zgs_ref[0, :, lanes] * mix).astype(o_ref.dtype)


def _attn_prompt(q, kca, vcb, ksa, vsb, kwa, vwb, gates, zgs, tq):
    b, _, t = q.shape
    nck = kca.shape[2]
    nsb = -(-t // SEL_BLOCK)
    slots = LANES - HEAD_DIM
    assert nsb <= slots
    m2t = np.zeros((slots, nck), np.float32)
    m2t[:nsb] = _imp_matrix(nck, nsb).T
    per_b4 = lambda n, r: pl.BlockSpec((1, n, r, LANES), lambda i, j: (i, 0, 0, 0))
    per_b3 = lambda r, w: pl.BlockSpec((1, r, w), lambda i, j: (i, 0, 0))
    cols = GROUP * tq
    return pl.pallas_call(
        functools.partial(_attn_prompt_kernel, tq=tq, n_sel=min(N_SEL, nsb)),
        grid=(b, t // tq),
        in_specs=[pl.BlockSpec((1, MIX_WIDTH, tq), lambda i, j: (i, 0, j)),
                  per_b4(KV_HEADS, nck), per_b3(KV_WIDTH, nck),
                  per_b4(KV_HEADS, t), per_b3(KV_WIDTH, t),
                  per_b4(KV_HEADS, t), per_b3(KV_WIDTH, t),
                  pl.BlockSpec((1, LANES, tq), lambda i, j: (i, 0, j)),
                  pl.BlockSpec((1, tq, MIX_WIDTH), lambda i, j: (i, j, 0)),
                  pl.BlockSpec(m2t.shape, lambda i, j: (0, 0))],
        out_specs=pl.BlockSpec((1, tq, MIX_WIDTH), lambda i, j: (i, j, 0)),
        out_shape=jax.ShapeDtypeStruct((b, t, MIX_WIDTH), F32),
        scratch_shapes=[pltpu.VMEM((1, cols), F32), pltpu.VMEM((HEAD_DIM + SUM_ROWS, cols), F32),
                        pltpu.VMEM((slots, tq), F32), pltpu.VMEM((tq, cols), F32), pltpu.VMEM((tq, cols), F32)],
        compiler_params=_cparams("arbitrary", "arbitrary"),
        name="attn_prompt",
    )(q, kca, vcb, ksa, vsb, kwa, vwb, gates, zgs, jnp.asarray(m2t, BF16))


def _online_update(s, vt, m_sc, l_sc, acc_sc):
    m_old = m_sc[...]
    m_new = jnp.maximum(m_old, jnp.max(s, axis=-1, keepdims=True))
    alpha = jnp.exp(m_old - m_new)
    p = jnp.exp(s - m_new)
    l_sc[...] = alpha * l_sc[...] + jnp.sum(p, axis=-1, keepdims=True)
    acc_sc[...] = alpha * acc_sc[...] + _dot_nt(p.astype(BF16), vt)
    m_sc[...] = m_new


def _attn_sample_kernel(pt_ref, q_ref, kca_ref, vcb_ref, ksn_ref, kwn_ref, win_ref, gates_ref, zgs_ref, mmat_ref,
                        *refs, pg, past_len):
    pages = refs[:pg]
    o_ref, qbd_sc, pen_sc, m_sc, l_sc, acc_sc, oc_sc, ow_sc = refs[pg:]
    step = pl.program_id(1)
    t = q_ref.shape[2]
    rows = N_HEADS * t
    rpad = qbd_sc.shape[0]
    npb = mmat_ref.shape[1]
    tok = jnp.bitwise_and(lax.broadcasted_iota(jnp.int32, (rpad, 1), 0), t - 1)
    qpos = past_len + tok

    @pl.when(step == 0)
    def _():
        zero_t = jnp.zeros((t, LANES), F32)
        qrows = []
        for hd in range(N_HEADS):
            g = hd // GROUP
            qh = q_ref[0, hd]
            if g % 2 == 1:
                qh = pltpu.roll(qh, HEAD_DIM, axis=1)
            qrows.append(jnp.concatenate([qh, zero_t] if g // 2 == 0 else [zero_t, qh], axis=1))
        qbd = jnp.concatenate(qrows + [jnp.zeros((rpad - rows, 2 * LANES), F32)], axis=0).astype(BF16)

        imps = []
        rg = GROUP * t
        qpos_g = qpos[0:rg]
        for g in range(KV_HEADS):
            qg = jnp.concatenate([q_ref[0, GROUP * g + r] for r in range(GROUP)], axis=0).astype(BF16)
            sc = _dot_nt(qg, kca_ref[0, g])
            ncp = sc.shape[1]
            blk_end = lax.broadcasted_iota(jnp.int32, (1, ncp), 1) * CMP_STRIDE + (CMP_BLOCK - 1)
            pc = _softmax_masked(sc, blk_end <= qpos_g)
            oc_sc[g * rg:(g + 1) * rg, :] = _dot(pc.astype(BF16), vcb_ref[0])
            imps.append(pc[0:t] + pc[t:2 * t] + pc[2 * t:3 * t])
        oc_sc[rows:, :] = jnp.zeros((rpad - rows, 2 * LANES), F32)
        imp = jnp.concatenate(imps, axis=0)
        imp_s = None
        for part in _split3(imp):
            d = _dot(part, mmat_ref[...])
            imp_s = d if imp_s is None else imp_s + d
        ngt = KV_HEADS * t
        width = npb + LANES
        base = jnp.concatenate([imp_s, jnp.zeros((ngt, LANES), F32)], axis=1)
        j = lax.broadcasted_iota(jnp.int32, (ngt, width), 1)
        cur = jnp.right_shift(past_len + jnp.bitwise_and(lax.broadcasted_iota(jnp.int32, (ngt, width), 0), t - 1), 6)
        valid = j <= cur
        forced = (j == 0) | (j == cur) | (j == cur - 1)
        score = jnp.where(valid, base + jnp.where(forced, FORCE_BONUS, 0.0), -jnp.inf)
        picked = jnp.zeros((ngt, width), F32)
        jf = j.astype(F32)
        for _ in range(N_SEL):
            mx = jnp.max(score, axis=-1, keepdims=True)
            idx = jnp.min(jnp.where(score == mx, jf, float(width)), axis=-1, keepdims=True)
            hit = jf == idx
            picked = jnp.where(hit, 1.0, picked)
            score = jnp.where(hit, -jnp.inf, score)
        pen_gt = jnp.where((picked > 0.5) & valid, 0.0, NEG)
        pen = jnp.concatenate([pen_gt[(hd // GROUP) * t:(hd // GROUP + 1) * t] for hd in range(N_HEADS)]
                              + [jnp.zeros((rpad - rows, width), F32)], axis=0)
        qbd_sc[...] = qbd
        for c in range(width // LANES):
            pen_sc[c * LANES:(c + 1) * LANES, :] = pen[:, c * LANES:(c + 1) * LANES].T.astype(BF16)
        pen_new = pen[:, npb:npb + 1]

        def padded(ref, cols):
            return jnp.concatenate([ref[0, :, cols], jnp.zeros((LANES - t, KV_WIDTH), F32)], axis=0).astype(BF16)

        kcols, vcols = slice(0, KV_WIDTH), slice(KV_WIDTH, 2 * KV_WIDTH)
        inew = lax.broadcasted_iota(jnp.int32, (1, LANES), 1)
        new_ok = (inew < t) & (past_len + inew <= qpos)

        s = jnp.where(new_ok, _dot_nt(qbd, padded(ksn_ref, kcols)) + pen_new, NEG)
        m0 = jnp.maximum(jnp.max(s, axis=-1, keepdims=True), M_INIT)
        p = jnp.exp(s - m0)
        m_sc[...] = m0
        l_sc[...] = jnp.sum(p, axis=-1, keepdims=True)
        acc_sc[...] = _dot(p.astype(BF16), padded(ksn_ref, vcols))

        wk = win_ref.shape[2]
        kw_pos = past_len - wk + lax.broadcasted_iota(jnp.int32, (1, wk), 1)
        mask1 = (kw_pos <= qpos) & (kw_pos > qpos - WINDOW) & (kw_pos >= 0)
        mask2 = new_ok & (past_len + inew > qpos - WINDOW)
        s1 = jnp.where(mask1, _dot(qbd, win_ref[0, kcols, :].astype(BF16)), -1e30)
        s2 = jnp.where(mask2, _dot_nt(qbd, padded(kwn_ref, kcols)), -1e30)
        mw = jnp.maximum(jnp.max(s1, axis=-1, keepdims=True), jnp.max(s2, axis=-1, keepdims=True))
        p1 = jnp.exp(s1 - mw) * mask1.astype(F32)
        p2 = jnp.exp(s2 - mw) * mask2.astype(F32)
        lw = jnp.sum(p1, axis=-1, keepdims=True) + jnp.sum(p2, axis=-1, keepdims=True)
        ow = _dot_nt(p1.astype(BF16), win_ref[0, vcols, :].astype(BF16)) + _dot(p2.astype(BF16), padded(kwn_ref, vcols))
        ow_sc[...] = ow / jnp.maximum(lw, 1e-30)

    bpu = PAGES_PER_UPDATE * (PAGE_SIZE // SEL_BLOCK)
    keys = PAGES_PER_UPDATE * PAGE_SIZE
    onehot = jnp.where(jnp.right_shift(lax.broadcasted_iota(jnp.int32, (bpu, keys), 1), 6)
                       == lax.broadcasted_iota(jnp.int32, (bpu, keys), 0), 1.0, 0.0).astype(BF16)
    for u in range(pg // PAGES_PER_UPDATE):
        blocks = [pages[u * PAGES_PER_UPDATE + i][0] for i in range(PAGES_PER_UPDATE)]
        kt = jnp.concatenate([x[0:KV_WIDTH, :] for x in blocks], axis=1).astype(BF16)
        vt = jnp.concatenate([x[KV_WIDTH:, :] for x in blocks], axis=1).astype(BF16)
        b0 = pl.multiple_of((step * (pg // PAGES_PER_UPDATE) + u) * bpu, bpu)
        bias = lax.dot_general(pen_sc[pl.ds(b0, bpu), :], onehot, (((0,), (0,)), ((), ())), preferred_element_type=F32)
        _online_update(_dot(qbd_sc[...], kt) + bias, vt, m_sc, l_sc, acc_sc)

    @pl.when(step == pl.num_programs(1) - 1)
    def _():
        o_s = acc_sc[...] / l_sc[...]
        lo = _lo_half(t)
        gates = gates_ref[0]
        placed = []
        for hd in range(N_HEADS):
            g = hd // GROUP
            rs = slice(hd * t, (hd + 1) * t)
            cols = slice((g // 2) * LANES, (g // 2 + 1) * LANES)
            mix = (gates[:, hd:hd + 1] * oc_sc[rs, cols] + gates[:, N_HEADS + hd:N_HEADS + hd + 1] * o_s[rs, cols]
                   + gates[:, 2 * N_HEADS + hd:2 * N_HEADS + hd + 1] * ow_sc[rs, cols])
            placed.append(mix if g % 2 == hd % 2 else pltpu.roll(mix, HEAD_DIM, axis=1))
        for c in range(N_HEADS // 2):
            cols = slice(c * LANES, (c + 1) * LANES)
            o_ref[0, :, cols] = zgs_ref[0, :, cols] * jnp.where(lo, placed[2 * c], placed[2 * c + 1])


def _attn_sample(q, kca, vcb, kvs_new, kvw_new, win_state, gates, zgs, cache_sel, page_table, t, pg):
    bd, n_pages = page_table.shape
    past_len = n_pages * PAGE_SIZE
    nck = kca.shape[2]
    npb = past_len // SEL_BLOCK
    mmat = jnp.asarray(_imp_matrix(nck, npb), BF16)
    wk = win_state.shape[2]
    rpad = -(-N_HEADS * t // LANES) * LANES
    tok = lambda w: pl.BlockSpec((1, t, w), lambda b, s, pt: (0, b, 0))
    grid_spec = pltpu.PrefetchScalarGridSpec(
        num_scalar_prefetch=1,
        grid=(bd, n_pages // pg),
        in_specs=[pl.BlockSpec((1, N_HEADS, t, LANES), lambda b, s, pt: (0, 0, b, 0)),
                  pl.BlockSpec((1, KV_HEADS, nck, LANES), lambda b, s, pt: (b, 0, 0, 0)),
                  pl.BlockSpec((1, nck, KV_WIDTH), lambda b, s, pt: (b, 0, 0)),
                  tok(2 * KV_WIDTH), tok(2 * KV_WIDTH),
                  pl.BlockSpec((1, 2 * KV_WIDTH, wk), lambda b, s, pt: (b, 0, 0)),
                  tok(LANES), tok(MIX_WIDTH),
                  pl.BlockSpec(mmat.shape, lambda b, s, pt: (0, 0))] + _page_specs(pg),
        out_specs=tok(MIX_WIDTH),
        scratch_shapes=[pltpu.VMEM((rpad, 2 * LANES), BF16), pltpu.VMEM((npb + LANES, rpad), BF16),
                        pltpu.VMEM((rpad, 1), F32), pltpu.VMEM((rpad, 1), F32),
                        pltpu.VMEM((rpad, 2 * LANES), F32), pltpu.VMEM((rpad, 2 * LANES), F32),
                        pltpu.VMEM((rpad, 2 * LANES), F32)],
    )
    return pl.pallas_call(
        functools.partial(_attn_sample_kernel, pg=pg, past_len=past_len),
        grid_spec=grid_spec,
        out_shape=jax.ShapeDtypeStruct((1, bd * t, MIX_WIDTH), F32),
        compiler_params=_cparams("arbitrary", "arbitrary"),
        name="attn_sample",
    )(page_table, q, kca, vcb, kvs_new, kvw_new, win_state, gates, zgs, mmat, *([cache_sel] * pg))


def _reorder_nsa_weight(w):
    d = w.shape[0]
    n_gate = 3 * N_HEADS
    zg0 = _C_ZG + n_gate
    return jnp.concatenate([w[:, :_C_ZG], w[:, zg0:zg0 + MIX_WIDTH], w[:, _C_ZG:zg0],
                            jnp.zeros((d, LANES - n_gate), w.dtype)], axis=1).astype(BF16)


def _cmp_weights(cmp_pos_w, cmp_phi):
    ratio = CMP_BLOCK // CMP_STRIDE
    pw = cmp_pos_w.reshape(2, ratio, CMP_STRIDE, HEAD_DIM)
    tiles = [jnp.concatenate([jnp.tile(pw[s, m], (1, KV_HEADS)) for s in range(2)], axis=1) for m in range(ratio)]
    eye = jnp.eye(KV_HEADS, dtype=cmp_phi.dtype)
    z = jnp.zeros((KV_WIDTH, KV_WIDTH), cmp_phi.dtype)
    phi_bd = jnp.concatenate([jnp.concatenate([jnp.kron(eye, cmp_phi[0]), z], axis=1),
                              jnp.concatenate([z, jnp.kron(eye, cmp_phi[1])], axis=1)], axis=0).astype(BF16)
    return tiles[0], tiles[1], phi_bd


def _feature_major(cache):
    n, rows = cache.shape[:2]
    return jnp.transpose(cache, (0, 2, 3, 4, 1)).reshape(n, -1, rows)


def kernel(x_prompt, x_sample, mem_prompt, cache_mem_kv, cache_cmp_kv, cache_sel_kv, page_table, state_conv, state_win_kv, norm_g, final_norm_g, mem_norm_g, w_mem_kv, w_in_conv, conv_w, w_in_nsa, cmp_pos_w, cmp_phi, w_out):
    b, t, d = x_prompt.shape
    bd, td, _ = x_sample.shape
    n_mem = mem_prompt.shape[1]
    n_pages = page_table.shape[1]
    past_len = n_pages * PAGE_SIZE
    assert w_in_conv.shape[0] == 1 and w_in_nsa.shape[0] == 1 and w_out.shape[0] == 2
    assert CMP_BLOCK == 2 * CMP_STRIDE and td & (td - 1) == 0 and td < CMP_STRIDE

    w_conv_b = w_in_conv[0].astype(BF16)
    w_nsa_b = _reorder_nsa_weight(w_in_nsa[0])
    w_out_b = w_out.astype(BF16)
    w_mem_b = w_mem_kv.astype(BF16)
    pw0, pw1, phi_bd = _cmp_weights(cmp_pos_w[0], cmp_phi[0])
    kv6 = lambda a, n, r: a.reshape(1, n, r, 2, KV_HEADS, HEAD_DIM)

    mem_kv_p = _memory_kv(mem_prompt, mem_norm_g, w_mem_b)
    xp = x_prompt.reshape(b * t, d)
    tm = min(512, t)
    mqg, mix, conv_p = _proj_conv(xp, norm_g[0], w_conv_b, conv_w[0], seg=t, tm=tm)
    x1 = _layer_out(xp, mqg, mem_kv_p[0], mix, w_out_b[0], tm=tm, seg=t)
    tabs = _rope_tables(jnp.arange(t, dtype=jnp.int32))
    tq = min(256, t)
    (mqg, q, kvc_p, kvs_p, kvw_p, gates, zgs, ksa, kwa, vsb, vwb) = _proj_nsa(
        x1.reshape(b, t, d), norm_g[1], w_nsa_b, tabs, tm=tq, prompt=True)
    kca, vcb = _cmp_prompt(kvc_p, pw0, pw1, phi_bd)
    mix = _attn_prompt(q, kca, vcb, ksa, vsb, kwa, vwb, gates, zgs, tq=tq)
    y_prompt = _layer_out(x1, mqg.reshape(b * t, 512), mem_kv_p[1], mix.reshape(b * t, MIX_WIDTH), w_out_b[1],
                          tm=tm, seg=t, final_g=final_norm_g).reshape(b, t, d)
    w_keep_p = min(WINDOW, t)

    ms = bd * td
    xs = x_sample.reshape(ms, d)
    st = state_conv[0]
    zrow = jnp.zeros((bd, td - 1, MIX_WIDTH), F32)
    s1 = jnp.concatenate([st[:, 1:2], zrow], axis=1).reshape(ms, MIX_WIDTH)
    s2 = jnp.concatenate([st, zrow[:, 1:]], axis=1).reshape(ms, MIX_WIDTH)
    mqg, mix, u_s = _proj_conv(xs, norm_g[0], w_conv_b, conv_w[0], seg=td, tm=ms, state=(s1, s2))
    conv_s = u_s.reshape(bd, td, MIX_WIDTH)[:, td - 2:]
    tmo = 8 * td
    mem_s = _feature_major(cache_mem_kv.reshape((-1,) + cache_mem_kv.shape[2:]))
    x1s = _layer_out(xs, mqg, mem_s, mix, w_out_b[0], tm=tmo, seg=td, feature_major=True)
    tabs_s = _rope_tables(jnp.tile(past_len + jnp.arange(td, dtype=jnp.int32), bd))
    (mqg, q_s, kvc_s, kvs_s, kvw_s, gates_s, zgs_s) = _proj_nsa(
        x1s.reshape(1, ms, d), norm_g[1], w_nsa_b, tabs_s, tm=ms, prompt=False)
    pg = min(32, n_pages)
    kca_s, vcb_s = _cmp_sample(_feature_major(cache_cmp_kv[0]), page_table, pw0, pw1, phi_bd, pg)
    mix_s = _attn_sample(q_s, kca_s, vcb_s, kvs_s, kvw_s, _feature_major(state_win_kv[0]), gates_s, zgs_s,
                         _feature_major(cache_sel_kv[0]), page_table, td, pg)
    y_sample = _layer_out(x1s, mqg.reshape(ms, 512), mem_s, mix_s.reshape(ms, MIX_WIDTH), w_out_b[1], tm=tmo, seg=td,
                          final_g=final_norm_g, feature_major=True, memkv_offset=bd).reshape(bd, td, d)
    w_keep = state_win_kv.shape[2]
    win_s = jnp.concatenate([state_win_kv[0], kvw_s.reshape(bd, td, 2, KV_HEADS, HEAD_DIM)], axis=1)[:, -w_keep:]

    return (y_prompt, y_sample, conv_p[None],
            kv6(kvc_p, b, t), kv6(kvs_p, b, t), kv6(kvw_p[:, t - w_keep_p:], b, w_keep_p),
            mem_kv_p.reshape(2, b, n_mem, 2, MEM_WIDTH // HEAD_DIM, HEAD_DIM),
            conv_s[None], kv6(kvc_s, bd, td), kv6(kvs_s, bd, td), win_s[None])
```

```python
import functools

import numpy as np
import jax
import jax.numpy as jnp
from jax import lax
from jax.experimental import pallas as pl
from jax.experimental.pallas import tpu as pltpu

F32 = jnp.float32
BF16 = jnp.bfloat16

D_MODEL = 1024
HEAD_DIM = 64
MIX_WIDTH = 768
MEM_WIDTH = 256
N_HEADS = 12
KV_HEADS = 4
GROUP = 3
KV_WIDTH = 256
CMP_BLOCK = 32
CMP_STRIDE = 16
SEL_BLOCK = 64
N_SEL = 16
WINDOW = 512
ROT_DIM = 16
ROPE_THETA = 500000.0
NORM_EPS = 1e-6
FORCE_BONUS = 1e4
PAGE_SIZE = 128
SCALE = HEAD_DIM ** -0.5
NEG = -(2.0 ** 100)
M_INIT = -1e30
LANES = 128
SUM_ROWS = 16
PAGES_PER_UPDATE = 8
VMEM_LIMIT = 56 * 2 ** 20


def _cparams(*sem):
    return pltpu.CompilerParams(dimension_semantics=sem, vmem_limit_bytes=VMEM_LIMIT)


def _dot(a, b):
    return jnp.dot(a, b, preferred_element_type=F32)


def _dot_nt(a, b):
    return lax.dot_general(a, b, (((1,), (1,)), ((), ())), preferred_element_type=F32)


def _rms(x, g):
    return x * lax.rsqrt(jnp.mean(x * x, axis=-1, keepdims=True) + NORM_EPS) * g


def _silu(x):
    return x * jax.nn.sigmoid(x)


def _split3(a):
    hi = a.astype(BF16)
    r1 = a - hi.astype(F32)
    mid = r1.astype(BF16)
    lo = (r1 - mid.astype(F32)).astype(BF16)
    return hi, mid, lo


def _lo_half(rows):
    return lax.broadcasted_iota(jnp.int32, (rows, LANES), 1) < HEAD_DIM


def _memkv_kernel(mem_ref, g_ref, w_ref, o_ref):
    h = _rms(mem_ref[0], g_ref[0]).astype(BF16)
    o_ref[0, 0] = _dot(h, w_ref[0])


def _memory_kv(mem, mem_norm_g, w_mem_kv_b):
    depth = w_mem_kv_b.shape[0]
    b, n_mem, d = mem.shape
    return pl.pallas_call(
        _memkv_kernel,
        grid=(depth, b),
        in_specs=[pl.BlockSpec((1, n_mem, d), lambda i, j: (j, 0, 0)),
                  pl.BlockSpec((1, 1, d), lambda i, j: (i, 0, 0)),
                  pl.BlockSpec((1, d, 2 * MEM_WIDTH), lambda i, j: (i, 0, 0))],
        out_specs=pl.BlockSpec((1, 1, n_mem, 2 * MEM_WIDTH), lambda i, j: (i, j, 0, 0)),
        out_shape=jax.ShapeDtypeStruct((depth, b, n_mem, 2 * MEM_WIDTH), F32),
        compiler_params=_cparams("arbitrary", "arbitrary"),
        name="memory_kv",
    )(mem, mem_norm_g.reshape(depth, 1, d), w_mem_kv_b)


def _proj_conv_kernel(*refs, tm, seg, has_state):
    if has_state:
        x_ref, g_ref, w_ref, cw_ref, s1_ref, s2_ref, mqg_ref, mix_ref, st_ref, ubuf = refs
    else:
        x_ref, g_ref, w_ref, cw_ref, mqg_ref, mix_ref, st_ref, ubuf = refs
    i = pl.program_id(0)
    h = _rms(x_ref[...], g_ref[...]).astype(BF16)
    mqg_ref[...] = _dot(h, w_ref[:, 0:512])
    bg = _dot(h, w_ref[:, 512:1280])
    cg = _dot(h, w_ref[:, 1280:2048])
    hin = _dot(h, w_ref[:, 2048:2816])
    zg = _dot(h, w_ref[:, 2816:3584])
    u = cg * hin

    @pl.when(i == 0)
    def _():
        ubuf[0:8, :] = jnp.zeros((8, MIX_WIDTH), F32)

    @pl.when(i > 0)
    def _():
        ubuf[0:8, :] = ubuf[tm:tm + 8, :]

    ubuf[8:8 + tm, :] = u
    u1 = ubuf[7:7 + tm, :]
    u2 = ubuf[6:6 + tm, :]
    rowpos = lax.rem(i * tm + lax.broadcasted_iota(jnp.int32, (tm, 1), 0), seg)
    if has_state:
        u1 = jnp.where(rowpos >= 1, u1, s1_ref[...])
        u2 = jnp.where(rowpos >= 2, u2, s2_ref[...])
        st_ref[...] = u
    else:
        u1 = jnp.where(rowpos >= 1, u1, 0.0)
        u2 = jnp.where(rowpos >= 2, u2, 0.0)
        st_ref[0] = ubuf[8 + tm - 2:8 + tm, :]
    y = cw_ref[0:1, :] * u2 + cw_ref[1:2, :] * u1 + cw_ref[2:3, :] * u
    mix_ref[...] = (_silu(zg) * (bg * y)).astype(mix_ref.dtype)


def _proj_conv(x2d, g, w_b, cw, seg, tm, state=None):
    m = x2d.shape[0]
    n_in = w_b.shape[1]
    has_state = state is not None
    in_specs = [pl.BlockSpec((tm, D_MODEL), lambda i: (i, 0)),
                pl.BlockSpec((1, D_MODEL), lambda i: (0, 0)),
                pl.BlockSpec((D_MODEL, n_in), lambda i: (0, 0)),
                pl.BlockSpec((3, MIX_WIDTH), lambda i: (0, 0))]
    args = [x2d, g.reshape(1, D_MODEL), w_b, cw]
    if has_state:
        in_specs += [pl.BlockSpec((tm, MIX_WIDTH), lambda i: (i, 0))] * 2
        args += list(state)
        st_shape = jax.ShapeDtypeStruct((m, MIX_WIDTH), F32)
        st_spec = pl.BlockSpec((tm, MIX_WIDTH), lambda i: (i, 0))
    else:
        st_shape = jax.ShapeDtypeStruct((m // seg, 2, MIX_WIDTH), F32)
        st_spec = pl.BlockSpec((1, 2, MIX_WIDTH), lambda i: ((i * tm) // seg, 0, 0))
    return pl.pallas_call(
        functools.partial(_proj_conv_kernel, tm=tm, seg=seg, has_state=has_state),
        grid=(m // tm,),
        in_specs=in_specs,
        out_specs=[pl.BlockSpec((tm, 512), lambda i: (i, 0)),
                   pl.BlockSpec((tm, MIX_WIDTH), lambda i: (i, 0)),
                   st_spec],
        out_shape=[jax.ShapeDtypeStruct((m, 512), F32),
                   jax.ShapeDtypeStruct((m, MIX_WIDTH), F32),
                   st_shape],
        scratch_shapes=[pltpu.VMEM((tm + 8, MIX_WIDTH), F32)],
        compiler_params=_cparams("arbitrary"),
        name="proj_conv",
    )(*args)


def _out_kernel(*refs, nseg, seg, final, feature_major):
    if final:
        x_ref, mqg_ref, memkv_ref, mix_ref, w_ref, g_ref, o_ref, mo_sc = refs
    else:
        x_ref, mqg_ref, memkv_ref, mix_ref, w_ref, o_ref, mo_sc = refs
    lo = _lo_half(seg)

    def seg_body(s, r0):
        for pair in range(2):
            cols = slice(pair * LANES, (pair + 1) * LANES)
            qp = mqg_ref[pl.ds(r0, seg), cols]
            vcols = slice(MEM_WIDTH + pair * LANES, MEM_WIDTH + (pair + 1) * LANES)
            if feature_major:
                kp, vp = memkv_ref[s, cols, :].astype(BF16), memkv_ref[s, vcols, :].astype(BF16)
            else:
                kp, vp = memkv_ref[s, :, cols].astype(BF16), memkv_ref[s, :, vcols].astype(BF16)
            outs = []
            for half in range(2):
                keep = lo if half == 0 else jnp.logical_not(lo)
                qm = jnp.where(keep, qp, 0.0).astype(BF16)
                sc = (_dot(qm, kp) if feature_major else _dot_nt(qm, kp)) * SCALE
                e = jnp.exp(sc - jnp.max(sc, axis=-1, keepdims=True))
                p = e / jnp.sum(e, axis=-1, keepdims=True)
                outs.append(_dot_nt(p.astype(BF16), vp) if feature_major else _dot(p.astype(BF16), vp))
            mo_sc[pl.ds(r0, seg), cols] = jnp.where(lo, outs[0], outs[1])

    if nseg == 1:
        seg_body(0, 0)
    else:
        def body(s, c):
            seg_body(s, pl.multiple_of(s * seg, seg))
            return c
        lax.fori_loop(0, nseg, body, 0)

    mg = mqg_ref[:, MEM_WIDTH:2 * MEM_WIDTH]
    a = (_silu(mg) * mo_sc[...]).astype(BF16)
    xn = x_ref[...] + (_dot(a, w_ref[0:MEM_WIDTH, :]) + _dot(mix_ref[...].astype(BF16), w_ref[MEM_WIDTH:, :]))
    if final:
        o_ref[...] = _rms(xn, g_ref[...])
    else:
        o_ref[...] = xn


def _layer_out(x2d, mqg, memkv, mix, w_out_b, tm, seg, final_g=None, feature_major=False, memkv_offset=0):
    m = x2d.shape[0]
    nseg = max(tm // seg, 1)
    seg_in = min(seg, tm)
    final = final_g is not None
    in_specs = [pl.BlockSpec((tm, D_MODEL), lambda i: (i, 0)),
                pl.BlockSpec((tm, 512), lambda i: (i, 0)),
                pl.BlockSpec((nseg,) + memkv.shape[1:], lambda i: (memkv_offset // nseg + (i * tm) // (seg * nseg), 0, 0)),
                pl.BlockSpec((tm, MIX_WIDTH), lambda i: (i, 0)),
                pl.BlockSpec((D_MODEL, D_MODEL), lambda i: (0, 0))]
    args = [x2d, mqg, memkv, mix, w_out_b]
    if final:
        in_specs.append(pl.BlockSpec((1, D_MODEL), lambda i: (0, 0)))
        args.append(final_g.reshape(1, D_MODEL))
    return pl.pallas_call(
        functools.partial(_out_kernel, nseg=nseg, seg=seg_in, final=final, feature_major=feature_major),
        grid=(m // tm,),
        in_specs=in_specs,
        out_specs=pl.BlockSpec((tm, D_MODEL), lambda i: (i, 0)),
        out_shape=jax.ShapeDtypeStruct((m, D_MODEL), F32),
        scratch_shapes=[pltpu.VMEM((tm, MEM_WIDTH), F32)],
        compiler_params=_cparams("arbitrary"),
        name="layer_out",
    )(*args)


_C_Q, _C_KC, _C_KS, _C_KW, _C_ZG, _C_GL, _C_END = 512, 1280, 1792, 2304, 2816, 3584, 3712


def _proj_nsa_kernel(*refs, tm, prompt):
    (x_ref, g_ref, w_ref, c_ref, sa_ref, sb_ref,
     mqg_ref, q_ref, kvc_ref, kvs_ref, kvw_ref, gates_ref, zgs_ref) = refs[:13]
    ti = pl.program_id(1)
    h = _rms(x_ref[0], g_ref[...]).astype(BF16)
    cos, sa, sb = c_ref[...], sa_ref[...], sb_ref[...]
    lo = _lo_half(tm)

    def rope(chunk):
        return chunk * cos + pltpu.roll(chunk, LANES - ROT_DIM // 2, axis=1) * sa + pltpu.roll(chunk, ROT_DIM // 2, axis=1) * sb

    mqg_ref[0] = _dot(h, w_ref[:, 0:_C_Q])
    qf = _dot(h, w_ref[:, _C_Q:_C_KC])
    for c in range(N_HEADS // 2):
        qc = rope(qf[:, c * LANES:(c + 1) * LANES]) * SCALE
        if prompt:
            q_ref[0, c * LANES:(c + 1) * LANES, :] = qc.T.astype(q_ref.dtype)
        else:
            q_ref[0, 2 * c] = jnp.where(lo, qc, 0.0).astype(q_ref.dtype)
            q_ref[0, 2 * c + 1] = jnp.where(lo, pltpu.roll(qc, HEAD_DIM, axis=1), 0.0).astype(q_ref.dtype)

    if prompt:
        ksa_ref, kwa_ref, vsb_ref, vwb_ref = refs[13:17]
        pos = ti * tm + lax.broadcasted_iota(jnp.int32, (tm, LANES), 0)
        lane = lax.broadcasted_iota(jnp.int32, (tm, LANES), 1)
        onehot = jnp.where(jnp.right_shift(pos, 6) == lane - HEAD_DIM, 1.0, 0.0)

    for name, off in (("c", _C_KC), ("s", _C_KS), ("w", _C_KW)):
        kv_ref = {"c": kvc_ref, "s": kvs_ref, "w": kvw_ref}[name]
        kk = _dot(h, w_ref[:, off:off + KV_WIDTH])
        vv = _dot(h, w_ref[:, off + KV_WIDTH:off + 2 * KV_WIDTH])
        kv_ref[0, :, KV_WIDTH:2 * KV_WIDTH] = vv
        for c in range(KV_HEADS // 2):
            kr = rope(kk[:, c * LANES:(c + 1) * LANES])
            kv_ref[0, :, c * LANES:(c + 1) * LANES] = kr
            if prompt and name != "c":
                pad = onehot if name == "s" else 0.0
                aug_ref = ksa_ref if name == "s" else kwa_ref
                aug_ref[0, 2 * c] = jnp.where(lo, kr, pad).astype(BF16)
                aug_ref[0, 2 * c + 1] = jnp.where(lo, pltpu.roll(kr, HEAD_DIM, axis=1), pad).astype(BF16)
        if prompt and name != "c":
            vt_ref = vsb_ref if name == "s" else vwb_ref
            for c in range(KV_WIDTH // LANES):
                vt_ref[0, c * LANES:(c + 1) * LANES, :] = vv[:, c * LANES:(c + 1) * LANES].T.astype(BF16)

    gates = jax.nn.sigmoid(_dot(h, w_ref[:, _C_GL:_C_END]))
    gates_ref[0] = gates.T if prompt else gates
    zgs_ref[0] = _silu(_dot(h, w_ref[:, _C_ZG:_C_GL]))


def _proj_nsa(x3d, g, w_b, tabs, tm, prompt):
    nb, t, _ = x3d.shape
    row = lambda w: pl.BlockSpec((1, tm, w), lambda b, i: (b, i, 0))
    head = lambda n: pl.BlockSpec((1, n, tm, LANES), lambda b, i: (b, 0, i, 0))
    tab = pl.BlockSpec((tm, LANES), lambda b, i: (i, 0))
    sds = jax.ShapeDtypeStruct
    col = lambda w: pl.BlockSpec((1, w, tm), lambda b, i: (b, 0, i))
    if prompt:
        q_spec, q_shape = col(MIX_WIDTH), sds((nb, MIX_WIDTH, t), BF16)
        g_spec, g_shape = col(LANES), sds((nb, LANES, t), F32)
    else:
        q_spec, q_shape = head(N_HEADS), sds((nb, N_HEADS, t, LANES), F32)
        g_spec, g_shape = row(LANES), sds((nb, t, LANES), F32)
    out_specs = [row(512), q_spec, row(512), row(512), row(512), g_spec, row(MIX_WIDTH)]
    out_shape = [sds((nb, t, 512), F32), q_shape,
                 sds((nb, t, 512), F32), sds((nb, t, 512), F32), sds((nb, t, 512), F32),
                 g_shape, sds((nb, t, MIX_WIDTH), F32)]
    if prompt:
        out_specs += [head(KV_HEADS), head(KV_HEADS), col(KV_WIDTH), col(KV_WIDTH)]
        out_shape += [sds((nb, KV_HEADS, t, LANES), BF16), sds((nb, KV_HEADS, t, LANES), BF16),
                      sds((nb, KV_WIDTH, t), BF16), sds((nb, KV_WIDTH, t), BF16)]
    return pl.pallas_call(
        functools.partial(_proj_nsa_kernel, tm=tm, prompt=prompt),
        grid=(nb, t // tm),
        in_specs=[pl.BlockSpec((1, tm, D_MODEL), lambda b, i: (b, i, 0)),
                  pl.BlockSpec((1, D_MODEL), lambda b, i: (0, 0)),
                  pl.BlockSpec((D_MODEL, _C_END), lambda b, i: (0, 0)),
                  tab, tab, tab],
        out_specs=out_specs,
        out_shape=out_shape,
        compiler_params=_cparams("arbitrary", "arbitrary"),
        name="proj_nsa",
    )(x3d, g.reshape(1, D_MODEL), w_b, *tabs)


def _rope_tables(pos):
    half = ROT_DIM // 2
    inv = ROPE_THETA ** (-jnp.arange(half, dtype=F32) * 2.0 / ROT_DIM)
    ang = pos.astype(F32)[:, None] * inv[None, :]
    cos, sin = jnp.cos(ang), jnp.sin(ang)
    n = pos.shape[0]
    one = jnp.ones((n, HEAD_DIM - ROT_DIM), F32)
    zero = jnp.zeros((n, HEAD_DIM - ROT_DIM), F32)
    zh = jnp.zeros((n, half), F32)
    c = jnp.concatenate([cos, cos, one], axis=1)
    sa = jnp.concatenate([-sin, zh, zero], axis=1)
    sb = jnp.concatenate([zh, sin, zero], axis=1)
    return tuple(jnp.tile(a, (1, LANES // HEAD_DIM)) for a in (c, sa, sb))


def _emit_cmp_blocks(out, kca_ref, v_ref, transpose_v):
    n = out.shape[0]
    lo = _lo_half(n)
    for c in range(KV_HEADS // 2):
        kc = out[:, c * LANES:(c + 1) * LANES]
        kca_ref[0, 2 * c] = jnp.where(lo, kc, 0.0).astype(BF16)
        kca_ref[0, 2 * c + 1] = jnp.where(lo, pltpu.roll(kc, HEAD_DIM, axis=1), 0.0).astype(BF16)
        if transpose_v:
            v_ref[0, c * LANES:(c + 1) * LANES, :] = out[:, KV_WIDTH + c * LANES:KV_WIDTH + (c + 1) * LANES].T.astype(BF16)
    if not transpose_v:
        v_ref[0] = out[:, KV_WIDTH:].astype(BF16)


def _cmp_prompt_kernel(kvc_ref, pw0_ref, pw1_ref, phi_ref, kca_ref, vcb_ref, p0_sc, p1_sc, *, rows):
    t = kvc_ref.shape[1]
    cpr = rows // CMP_STRIDE
    pw0 = pw0_ref[...][None]
    pw1 = pw1_ref[...][None]
    for i in range(t // rows):
        x = kvc_ref[0, i * rows:(i + 1) * rows, :].reshape(cpr, CMP_STRIDE, 2 * KV_WIDTH)
        p0_sc[i * cpr:(i + 1) * cpr, :] = jnp.sum(x * pw0, axis=1)
        p1_sc[i * cpr:(i + 1) * cpr, :] = jnp.sum(x * pw1, axis=1)
    nck = t // CMP_STRIDE
    blk = p0_sc[...] + pltpu.roll(p1_sc[...], nck - 1, axis=0)
    _emit_cmp_blocks(_dot(blk.astype(BF16), phi_ref[...]), kca_ref, vcb_ref, transpose_v=True)


def _cmp_prompt(kvc, pw0, pw1, phi_bd):
    b, t, _ = kvc.shape
    nck = t // CMP_STRIDE
    full = lambda s: pl.BlockSpec(s, lambda i: (0,) * len(s))
    return pl.pallas_call(
        functools.partial(_cmp_prompt_kernel, rows=512),
        grid=(b,),
        in_specs=[pl.BlockSpec((1, t, 2 * KV_WIDTH), lambda i: (i, 0, 0)),
                  full((CMP_STRIDE, 2 * KV_WIDTH)), full((CMP_STRIDE, 2 * KV_WIDTH)),
                  full((2 * KV_WIDTH, 2 * KV_WIDTH))],
        out_specs=[pl.BlockSpec((1, KV_HEADS, nck, LANES), lambda i: (i, 0, 0, 0)),
                   pl.BlockSpec((1, KV_WIDTH, nck), lambda i: (i, 0, 0))],
        out_shape=[jax.ShapeDtypeStruct((b, KV_HEADS, nck, LANES), BF16),
                   jax.ShapeDtypeStruct((b, KV_WIDTH, nck), BF16)],
        scratch_shapes=[pltpu.VMEM((nck, 2 * KV_WIDTH), F32), pltpu.VMEM((nck, 2 * KV_WIDTH), F32)],
        compiler_params=_cparams("arbitrary"),
        name="cmp_prompt",
    )(kvc, pw0, pw1, phi_bd)


def _cmp_sample_kernel(pt_ref, w0_ref, w1_ref, seg_ref, phi_ref, *refs, pg):
    pages = refs[:pg]
    kca_ref, vcb_ref, p0_sc, p1_sc = refs[pg:]
    step = pl.program_id(1)
    cpp = PAGE_SIZE // CMP_STRIDE
    for k in range(pg):
        xt = pages[k][0]
        r0 = pl.multiple_of((step * pg + k) * cpp, cpp)
        for w_ref, p_sc in ((w0_ref, p0_sc), (w1_ref, p1_sc)):
            y = (xt * w_ref[...]).astype(BF16)
            p_sc[pl.ds(r0, cpp), :] = _dot_nt(seg_ref[...], y)[0:cpp]

    @pl.when(step == pl.num_programs(1) - 1)
    def _():
        nck = p0_sc.shape[0]
        blk = p0_sc[...] + pltpu.roll(p1_sc[...], nck - 1, axis=0)
        _emit_cmp_blocks(_dot(blk.astype(BF16), phi_ref[...]), kca_ref, vcb_ref, transpose_v=False)


def _page_specs(pg):
    return [pl.BlockSpec((1, 2 * KV_WIDTH, PAGE_SIZE), functools.partial(
        lambda b, s, pt, k: (pt[b, s * pg + k], 0, 0), k=k)) for k in range(pg)]


def _cmp_sample(cache_t, page_table, pw0, pw1, phi_bd, pg):
    bd, n_pages = page_table.shape
    nck = n_pages * PAGE_SIZE // CMP_STRIDE
    cpp = PAGE_SIZE // CMP_STRIDE
    w0, w1 = (jnp.tile(pw.T, (1, cpp)) for pw in (pw0, pw1))
    seg = np.zeros((max(cpp, 16), PAGE_SIZE), np.float32)
    seg[np.arange(PAGE_SIZE) // CMP_STRIDE, np.arange(PAGE_SIZE)] = 1.0
    full = lambda s: pl.BlockSpec(s, lambda b, i, pt: (0,) * len(s))
    grid_spec = pltpu.PrefetchScalarGridSpec(
        num_scalar_prefetch=1,
        grid=(bd, n_pages // pg),
        in_specs=[full(w0.shape), full(w1.shape), full(seg.shape),
                  full((2 * KV_WIDTH, 2 * KV_WIDTH))] + _page_specs(pg),
        out_specs=[pl.BlockSpec((1, KV_HEADS, nck, LANES), lambda b, i, pt: (b, 0, 0, 0)),
                   pl.BlockSpec((1, nck, KV_WIDTH), lambda b, i, pt: (b, 0, 0))],
        scratch_shapes=[pltpu.VMEM((nck, 2 * KV_WIDTH), F32), pltpu.VMEM((nck, 2 * KV_WIDTH), F32)],
    )
    return pl.pallas_call(
        functools.partial(_cmp_sample_kernel, pg=pg),
        grid_spec=grid_spec,
        out_shape=[jax.ShapeDtypeStruct((bd, KV_HEADS, nck, LANES), BF16),
                   jax.ShapeDtypeStruct((bd, nck, KV_WIDTH), BF16)],
        compiler_params=_cparams("arbitrary", "arbitrary"),
        name="cmp_sample",
    )(page_table, w0, w1, jnp.asarray(seg, BF16), phi_bd, *([cache_t] * pg))


def _imp_matrix(nck, n_sel_blocks):
    cps = SEL_BLOCK // CMP_STRIDE
    c = np.arange(nck)[:, None]
    j = np.arange(n_sel_blocks)[None, :]
    m = ((c >= cps * j) & (c <= cps * j + cps - 1)).astype(np.float32)
    m += ((c + 1 >= cps * j) & (c + 1 <= cps * j + cps - 1)).astype(np.float32)
    return m


def _softmax_masked(s, mask, axis=-1):
    s = jnp.where(mask, s, -1e30)
    p = jnp.exp(s - jnp.max(s, axis=axis, keepdims=True)) * mask.astype(F32)
    return p / jnp.maximum(jnp.sum(p, axis=axis, keepdims=True), 1e-30)


def _online_update_t(s, vt_ones, m_sc, acc_sc):
    m_old = m_sc[...]
    m_new = jnp.maximum(m_old, jnp.max(s, axis=0, keepdims=True))
    p = jnp.exp((s - m_new).astype(BF16))
    acc_sc[...] = jnp.exp(m_old - m_new) * acc_sc[...] + _dot(vt_ones, p)
    m_sc[...] = m_new


def _reset_online(m_sc, l_sc, acc_sc):
    m_sc[...] = jnp.full(m_sc.shape, M_INIT, F32)
    l_sc[...] = jnp.zeros(l_sc.shape, F32)
    acc_sc[...] = jnp.zeros(acc_sc.shape, F32)


def _attn_prompt_kernel(q_ref, kca_ref, vcb_ref, ksa_ref, vsb_ref, kwa_ref, vwb_ref, gates_ref, zgs_ref,
                        m2t_ref, o_ref, m_sc, acc_sc, score_sc, sa_sc, sb_sc, *, tq, n_sel):
    qi = pl.program_id(1)
    s0 = qi * tq
    cols = GROUP * tq
    nsb = m2t_ref.shape[0]
    qpos = s0 + lax.broadcasted_iota(jnp.int32, (1, tq), 1)
    qpos3 = jnp.concatenate([qpos] * GROUP, axis=1)
    gates = gates_ref[0]
    zero_h = jnp.zeros((HEAD_DIM, tq), BF16)
    mix_t = []
    for g in range(KV_HEADS):
        vrows = slice(g * HEAD_DIM, (g + 1) * HEAD_DIM)
        qh = [q_ref[0, (GROUP * g + r) * HEAD_DIM:(GROUP * g + r + 1) * HEAD_DIM, :] for r in range(GROUP)]
        qc = jnp.concatenate([jnp.concatenate([x, zero_h], axis=0) for x in qh], axis=1)

        sc = _dot(kca_ref[0, g], qc)
        ncp = sc.shape[0]
        blk_end = lax.broadcasted_iota(jnp.int32, (ncp, 1), 0) * CMP_STRIDE + (CMP_BLOCK - 1)
        pc = _softmax_masked(sc, blk_end <= qpos3, axis=0)
        oc = _dot(vcb_ref[0, vrows, :], pc.astype(BF16))
        imp = pc[:, 0:tq] + pc[:, tq:2 * tq] + pc[:, 2 * tq:3 * tq]

        imp_t = None
        for part in _split3(imp):
            d = _dot(m2t_ref[...], part)
            imp_t = d if imp_t is None else imp_t + d
        j = lax.broadcasted_iota(jnp.int32, (nsb, tq), 0)
        cur = jnp.right_shift(s0 + lax.broadcasted_iota(jnp.int32, (nsb, tq), 1), 6)
        valid = j <= cur
        forced = (j == 0) | (j == cur) | (j == cur - 1)
        score = jnp.where(valid, imp_t + jnp.where(forced, FORCE_BONUS, 0.0), -jnp.inf)
        score_sc[...] = score
        bpt = tq // SEL_BLOCK

        def rank_body(it, cnt):
            for u in range(bpt):
                i = it * bpt + u
                row = score_sc[pl.ds(i, 1), :]
                tie = jnp.where(j > i, 1.0, 0.0)
                cnt = cnt + jnp.where(row > score, 1.0, jnp.where(row == score, tie, 0.0))
            return cnt

        cnt = lax.fori_loop(0, jnp.minimum(qi + 1, nsb // bpt), rank_body, jnp.zeros((nsb, tq), F32))
        sel = (cnt < float(n_sel)) & valid
        pen = jnp.where(sel, 0.0, NEG).astype(BF16)
        qa = jnp.concatenate([jnp.concatenate([x, pen], axis=0) for x in qh], axis=1)

        def tile_start(kt):
            return pl.multiple_of(kt * tq, tq)

        def key_pos(kt):
            return tile_start(kt) + lax.broadcasted_iota(jnp.int32, (tq, 1), 0)

        def run_branch(scores, values, first, last_mask, pairs_per_trip):
            m_sc[...] = jnp.full(m_sc.shape, M_INIT, F32)
            acc_sc[...] = jnp.zeros(acc_sc.shape, F32)
            ones = jnp.ones((SUM_ROWS, tq), BF16)

            def update(s_ref, kt, mask=None):
                s = s_ref[...] if mask is None else mask(s_ref[...])
                _online_update_t(s, jnp.concatenate([values(kt), ones], axis=0), m_sc, acc_sc)

            def pair(kt):
                sb_sc[...] = scores(kt + 1)
                update(sa_sc, kt)
                sa_sc[...] = scores(kt + 2)
                update(sb_sc, kt + 1)

            def trip(i, c):
                for u in range(pairs_per_trip):
                    pair(first + 2 * (pairs_per_trip * i + u))
                return c

            ahead = qi - first
            n_pairs = ahead // 2
            sa_sc[...] = scores(first)
            lax.fori_loop(0, n_pairs // pairs_per_trip, trip, 0)
            if pairs_per_trip == 2:
                @pl.when(n_pairs % 2 == 1)
                def _():
                    pair(first + 2 * (n_pairs - 1))

            @pl.when(ahead % 2 == 1)
            def _():
                sb_sc[...] = scores(qi)
                update(sa_sc, qi - 1)
                update(sb_sc, qi, last_mask)

            @pl.when(ahead % 2 == 0)
            def _():
                update(sa_sc, qi, last_mask)

            return acc_sc[0:HEAD_DIM, :] / acc_sc[HEAD_DIM:HEAD_DIM + 1, :]

        o_s = run_branch(
            lambda kt: _dot(ksa_ref[0, g, pl.ds(tile_start(kt), tq), :], qa),
            lambda kt: vsb_ref[0, vrows, pl.ds(tile_start(kt), tq)],
            0, lambda s: jnp.where(key_pos(qi) <= qpos3, s, NEG), pairs_per_trip=2)

        def win_scores(kt):
            kpos = key_pos(kt)
            s = _dot(kwa_ref[0, g, pl.ds(tile_start(kt), tq), :], qc)
            return jnp.where((kpos <= qpos3) & (kpos > qpos3 - WINDOW), s, NEG)

        o_w = run_branch(win_scores, lambda kt: vwb_ref[0, vrows, pl.ds(tile_start(kt), tq)],
                         jnp.maximum(qi - WINDOW // tq, 0), lambda s: s, pairs_per_trip=1)

        for r in range(GROUP):
            hd = GROUP * g + r
            cs = slice(r * tq, (r + 1) * tq)
            mix_t.append(gates[hd:hd + 1, :] * oc[:, cs] + gates[N_HEADS + hd:N_HEADS + hd + 1, :] * o_s[:, cs]
                         + gates[2 * N_HEADS + hd:2 * N_HEADS + hd + 1, :] * o_w[:, cs])
    for c in range(N_HEADS // 2):
        lanes = slice(c * LANES, (c + 1) * LANES)
        mix = jnp.concatenate([mix_t[2 * c], mix_t[2 * c + 1]], axis=0).T
        o_ref[0, :, lanes] = (zgs_ref[0, :, lanes] * mix).astype(o_ref.dtype)


def _attn_prompt(q, kca, vcb, ksa, vsb, kwa, vwb, gates, zgs, tq):
    b, _, t = q.shape
    nck = kca.shape[2]
    nsb = -(-t // SEL_BLOCK)
    slots = LANES - HEAD_DIM
    assert nsb <= slots
    m2t = np.zeros((slots, nck), np.float32)
    m2t[:nsb] = _imp_matrix(nck, nsb).T
    per_b4 = lambda n, r: pl.BlockSpec((1, n, r, LANES), lambda i, j: (i, 0, 0, 0))
    per_b3 = lambda r, w: pl.BlockSpec((1, r, w), lambda i, j: (i, 0, 0))
    cols = GROUP * tq
    return pl.pallas_call(
        functools.partial(_attn_prompt_kernel, tq=tq, n_sel=min(N_SEL, nsb)),
        grid=(b, t // tq),
        in_specs=[pl.BlockSpec((1, MIX_WIDTH, tq), lambda i, j: (i, 0, j)),
                  per_b4(KV_HEADS, nck), per_b3(KV_WIDTH, nck),
                  per_b4(KV_HEADS, t), per_b3(KV_WIDTH, t),
                  per_b4(KV_HEADS, t), per_b3(KV_WIDTH, t),
                  pl.BlockSpec((1, LANES, tq), lambda i, j: (i, 0, j)),
                  pl.BlockSpec((1, tq, MIX_WIDTH), lambda i, j: (i, j, 0)),
                  pl.BlockSpec(m2t.shape, lambda i, j: (0, 0))],
        out_specs=pl.BlockSpec((1, tq, MIX_WIDTH), lambda i, j: (i, j, 0)),
        out_shape=jax.ShapeDtypeStruct((b, t, MIX_WIDTH), F32),
        scratch_shapes=[pltpu.VMEM((1, cols), F32), pltpu.VMEM((HEAD_DIM + SUM_ROWS, cols), F32),
                        pltpu.VMEM((slots, tq), F32), pltpu.VMEM((tq, cols), F32), pltpu.VMEM((tq, cols), F32)],
        compiler_params=_cparams("arbitrary", "arbitrary"),
        name="attn_prompt",
    )(q, kca, vcb, ksa, vsb, kwa, vwb, gates, zgs, jnp.asarray(m2t, BF16))


def _online_update(s, vt, m_sc, l_sc, acc_sc):
    m_old = m_sc[...]
    m_new = jnp.maximum(m_old, jnp.max(s, axis=-1, keepdims=True))
    alpha = jnp.exp(m_old - m_new)
    p = jnp.exp(s - m_new)
    l_sc[...] = alpha * l_sc[...] + jnp.sum(p, axis=-1, keepdims=True)
    acc_sc[...] = alpha * acc_sc[...] + _dot_nt(p.astype(BF16), vt)
    m_sc[...] = m_new


def _attn_sample_kernel(pt_ref, q_ref, kca_ref, vcb_ref, ksn_ref, kwn_ref, win_ref, gates_ref, zgs_ref, mmat_ref,
                        *refs, pg, past_len):
    pages = refs[:pg]
    o_ref, qbd_sc, pen_sc, m_sc, l_sc, acc_sc, oc_sc, ow_sc = refs[pg:]
    step = pl.program_id(1)
    t = q_ref.shape[2]
    rows = N_HEADS * t
    rpad = qbd_sc.shape[0]
    npb = mmat_ref.shape[1]
    tok = jnp.bitwise_and(lax.broadcasted_iota(jnp.int32, (rpad, 1), 0), t - 1)
    qpos = past_len + tok

    @pl.when(step == 0)
    def _():
        zero_t = jnp.zeros((t, LANES), F32)
        qrows = []
        for hd in range(N_HEADS):
            g = hd // GROUP
            qh = q_ref[0, hd]
            if g % 2 == 1:
                qh = pltpu.roll(qh, HEAD_DIM, axis=1)
            qrows.append(jnp.concatenate([qh, zero_t] if g // 2 == 0 else [zero_t, qh], axis=1))
        qbd = jnp.concatenate(qrows + [jnp.zeros((rpad - rows, 2 * LANES), F32)], axis=0).astype(BF16)

        imps = []
        rg = GROUP * t
        qpos_g = qpos[0:rg]
        for g in range(KV_HEADS):
            qg = jnp.concatenate([q_ref[0, GROUP * g + r] for r in range(GROUP)], axis=0).astype(BF16)
            sc = _dot_nt(qg, kca_ref[0, g])
            ncp = sc.shape[1]
            blk_end = lax.broadcasted_iota(jnp.int32, (1, ncp), 1) * CMP_STRIDE + (CMP_BLOCK - 1)
            pc = _softmax_masked(sc, blk_end <= qpos_g)
            oc_sc[g * rg:(g + 1) * rg, :] = _dot(pc.astype(BF16), vcb_ref[0])
            imps.append(pc[0:t] + pc[t:2 * t] + pc[2 * t:3 * t])
        oc_sc[rows:, :] = jnp.zeros((rpad - rows, 2 * LANES), F32)
        imp = jnp.concatenate(imps, axis=0)
        imp_s = None
        for part in _split3(imp):
            d = _dot(part, mmat_ref[...])
            imp_s = d if imp_s is None else imp_s + d
        ngt = KV_HEADS * t
        width = npb + LANES
        base = jnp.concatenate([imp_s, jnp.zeros((ngt, LANES), F32)], axis=1)
        j = lax.broadcasted_iota(jnp.int32, (ngt, width), 1)
        cur = jnp.right_shift(past_len + jnp.bitwise_and(lax.broadcasted_iota(jnp.int32, (ngt, width), 0), t - 1), 6)
        valid = j <= cur
        forced = (j == 0) | (j == cur) | (j == cur - 1)
        score = jnp.where(valid, base + jnp.where(forced, FORCE_BONUS, 0.0), -jnp.inf)
        picked = jnp.zeros((ngt, width), F32)
        jf = j.astype(F32)
        for _ in range(N_SEL):
            mx = jnp.max(score, axis=-1, keepdims=True)
            idx = jnp.min(jnp.where(score == mx, jf, float(width)), axis=-1, keepdims=True)
            hit = jf == idx
            picked = jnp.where(hit, 1.0, picked)
            score = jnp.where(hit, -jnp.inf, score)
        pen_gt = jnp.where((picked > 0.5) & valid, 0.0, NEG)
        pen = jnp.concatenate([pen_gt[(hd // GROUP) * t:(hd // GROUP + 1) * t] for hd in range(N_HEADS)]
                              + [jnp.zeros((rpad - rows, width), F32)], axis=0)
        qbd_sc[...] = qbd
        for c in range(width // LANES):
            pen_sc[c * LANES:(c + 1) * LANES, :] = pen[:, c * LANES:(c + 1) * LANES].T.astype(BF16)
        pen_new = pen[:, npb:npb + 1]

        def padded(ref, cols):
            return jnp.concatenate([ref[0, :, cols], jnp.zeros((LANES - t, KV_WIDTH), F32)], axis=0).astype(BF16)

        kcols, vcols = slice(0, KV_WIDTH), slice(KV_WIDTH, 2 * KV_WIDTH)
        inew = lax.broadcasted_iota(jnp.int32, (1, LANES), 1)
        new_ok = (inew < t) & (past_len + inew <= qpos)

        s = jnp.where(new_ok, _dot_nt(qbd, padded(ksn_ref, kcols)) + pen_new, NEG)
        m0 = jnp.maximum(jnp.max(s, axis=-1, keepdims=True), M_INIT)
        p = jnp.exp(s - m0)
        m_sc[...] = m0
        l_sc[...] = jnp.sum(p, axis=-1, keepdims=True)
        acc_sc[...] = _dot(p.astype(BF16), padded(ksn_ref, vcols))

        wk = win_ref.shape[2]
        kw_pos = past_len - wk + lax.broadcasted_iota(jnp.int32, (1, wk), 1)
        mask1 = (kw_pos <= qpos) & (kw_pos > qpos - WINDOW) & (kw_pos >= 0)
        mask2 = new_ok & (past_len + inew > qpos - WINDOW)
        s1 = jnp.where(mask1, _dot(qbd, win_ref[0, kcols, :].astype(BF16)), -1e30)
        s2 = jnp.where(mask2, _dot_nt(qbd, padded(kwn_ref, kcols)), -1e30)
        mw = jnp.maximum(jnp.max(s1, axis=-1, keepdims=True), jnp.max(s2, axis=-1, keepdims=True))
        p1 = jnp.exp(s1 - mw) * mask1.astype(F32)
        p2 = jnp.exp(s2 - mw) * mask2.astype(F32)
        lw = jnp.sum(p1, axis=-1, keepdims=True) + jnp.sum(p2, axis=-1, keepdims=True)
        ow = _dot_nt(p1.astype(BF16), win_ref[0, vcols, :].astype(BF16)) + _dot(p2.astype(BF16), padded(kwn_ref, vcols))
        ow_sc[...] = ow / jnp.maximum(lw, 1e-30)

    bpu = PAGES_PER_UPDATE * (PAGE_SIZE // SEL_BLOCK)
    keys = PAGES_PER_UPDATE * PAGE_SIZE
    onehot = jnp.where(jnp.right_shift(lax.broadcasted_iota(jnp.int32, (bpu, keys), 1), 6)
                       == lax.broadcasted_iota(jnp.int32, (bpu, keys), 0), 1.0, 0.0).astype(BF16)
    for u in range(pg // PAGES_PER_UPDATE):
        blocks = [pages[u * PAGES_PER_UPDATE + i][0] for i in range(PAGES_PER_UPDATE)]
        kt = jnp.concatenate([x[0:KV_WIDTH, :] for x in blocks], axis=1).astype(BF16)
        vt = jnp.concatenate([x[KV_WIDTH:, :] for x in blocks], axis=1).astype(BF16)
        b0 = pl.multiple_of((step * (pg // PAGES_PER_UPDATE) + u) * bpu, bpu)
        bias = lax.dot_general(pen_sc[pl.ds(b0, bpu), :], onehot, (((0,), (0,)), ((), ())), preferred_element_type=F32)
        _online_update(_dot(qbd_sc[...], kt) + bias, vt, m_sc, l_sc, acc_sc)

    @pl.when(step == pl.num_programs(1) - 1)
    def _():
        o_s = acc_sc[...] / l_sc[...]
        lo = _lo_half(t)
        gates = gates_ref[0]
        placed = []
        for hd in range(N_HEADS):
            g = hd // GROUP
            rs = slice(hd * t, (hd + 1) * t)
            cols = slice((g // 2) * LANES, (g // 2 + 1) * LANES)
            mix = (gates[:, hd:hd + 1] * oc_sc[rs, cols] + gates[:, N_HEADS + hd:N_HEADS + hd + 1] * o_s[rs, cols]
                   + gates[:, 2 * N_HEADS + hd:2 * N_HEADS + hd + 1] * ow_sc[rs, cols])
            placed.append(mix if g % 2 == hd % 2 else pltpu.roll(mix, HEAD_DIM, axis=1))
        for c in range(N_HEADS // 2):
            cols = slice(c * LANES, (c + 1) * LANES)
            o_ref[0, :, cols] = zgs_ref[0, :, cols] * jnp.where(lo, placed[2 * c], placed[2 * c + 1])


def _attn_sample(q, kca, vcb, kvs_new, kvw_new, win_state, gates, zgs, cache_sel, page_table, t, pg):
    bd, n_pages = page_table.shape
    past_len = n_pages * PAGE_SIZE
    nck = kca.shape[2]
    npb = past_len // SEL_BLOCK
    mmat = jnp.asarray(_imp_matrix(nck, npb), BF16)
    wk = win_state.shape[2]
    rpad = -(-N_HEADS * t // LANES) * LANES
    tok = lambda w: pl.BlockSpec((1, t, w), lambda b, s, pt: (0, b, 0))
    grid_spec = pltpu.PrefetchScalarGridSpec(
        num_scalar_prefetch=1,
        grid=(bd, n_pages // pg),
        in_specs=[pl.BlockSpec((1, N_HEADS, t, LANES), lambda b, s, pt: (0, 0, b, 0)),
                  pl.BlockSpec((1, KV_HEADS, nck, LANES), lambda b, s, pt: (b, 0, 0, 0)),
                  pl.BlockSpec((1, nck, KV_WIDTH), lambda b, s, pt: (b, 0, 0)),
                  tok(2 * KV_WIDTH), tok(2 * KV_WIDTH),
                  pl.BlockSpec((1, 2 * KV_WIDTH, wk), lambda b, s, pt: (b, 0, 0)),
                  tok(LANES), tok(MIX_WIDTH),
                  pl.BlockSpec(mmat.shape, lambda b, s, pt: (0, 0))] + _page_specs(pg),
        out_specs=tok(MIX_WIDTH),
        scratch_shapes=[pltpu.VMEM((rpad, 2 * LANES), BF16), pltpu.VMEM((npb + LANES, rpad), BF16),
                        pltpu.VMEM((rpad, 1), F32), pltpu.VMEM((rpad, 1), F32),
                        pltpu.VMEM((rpad, 2 * LANES), F32), pltpu.VMEM((rpad, 2 * LANES), F32),
                        pltpu.VMEM((rpad, 2 * LANES), F32)],
    )
    return pl.pallas_call(
        functools.partial(_attn_sample_kernel, pg=pg, past_len=past_len),
        grid_spec=grid_spec,
        out_shape=jax.ShapeDtypeStruct((1, bd * t, MIX_WIDTH), F32),
        compiler_params=_cparams("arbitrary", "arbitrary"),
        name="attn_sample",
    )(page_table, q, kca, vcb, kvs_new, kvw_new, win_state, gates, zgs, mmat, *([cache_sel] * pg))


def _reorder_nsa_weight(w):
    d = w.shape[0]
    n_gate = 3 * N_HEADS
    zg0 = _C_ZG + n_gate
    return jnp.concatenate([w[:, :_C_ZG], w[:, zg0:zg0 + MIX_WIDTH], w[:, _C_ZG:zg0],
                            jnp.zeros((d, LANES - n_gate), w.dtype)], axis=1).astype(BF16)


def _cmp_weights(cmp_pos_w, cmp_phi):
    ratio = CMP_BLOCK // CMP_STRIDE
    pw = cmp_pos_w.reshape(2, ratio, CMP_STRIDE, HEAD_DIM)
    tiles = [jnp.concatenate([jnp.tile(pw[s, m], (1, KV_HEADS)) for s in range(2)], axis=1) for m in range(ratio)]
    eye = jnp.eye(KV_HEADS, dtype=cmp_phi.dtype)
    z = jnp.zeros((KV_WIDTH, KV_WIDTH), cmp_phi.dtype)
    phi_bd = jnp.concatenate([jnp.concatenate([jnp.kron(eye, cmp_phi[0]), z], axis=1),
                              jnp.concatenate([z, jnp.kron(eye, cmp_phi[1])], axis=1)], axis=0).astype(BF16)
    return tiles[0], tiles[1], phi_bd


def _feature_major(cache):
    n, rows = cache.shape[:2]
    return jnp.transpose(cache, (0, 2, 3, 4, 1)).reshape(n, -1, rows)


def kernel(x_prompt, x_sample, mem_prompt, cache_mem_kv, cache_cmp_kv, cache_sel_kv, page_table, state_conv, state_win_kv, norm_g, final_norm_g, mem_norm_g, w_mem_kv, w_in_conv, conv_w, w_in_nsa, cmp_pos_w, cmp_phi, w_out):
    b, t, d = x_prompt.shape
    bd, td, _ = x_sample.shape
    n_mem = mem_prompt.shape[1]
    n_pages = page_table.shape[1]
    past_len = n_pages * PAGE_SIZE
    assert w_in_conv.shape[0] == 1 and w_in_nsa.shape[0] == 1 and w_out.shape[0] == 2
    assert CMP_BLOCK == 2 * CMP_STRIDE and td & (td - 1) == 0 and td < CMP_STRIDE

    w_conv_b = w_in_conv[0].astype(BF16)
    w_nsa_b = _reorder_nsa_weight(w_in_nsa[0])
    w_out_b = w_out.astype(BF16)
    w_mem_b = w_mem_kv.astype(BF16)
    pw0, pw1, phi_bd = _cmp_weights(cmp_pos_w[0], cmp_phi[0])
    kv6 = lambda a, n, r: a.reshape(1, n, r, 2, KV_HEADS, HEAD_DIM)

    mem_kv_p = _memory_kv(mem_prompt, mem_norm_g, w_mem_b)
    xp = x_prompt.reshape(b * t, d)
    tm = min(512, t)
    tmo_p = min(1024, t)
    mqg, mix, conv_p = _proj_conv(xp, norm_g[0], w_conv_b, conv_w[0], seg=t, tm=tm)
    x1 = _layer_out(xp, mqg, mem_kv_p[0], mix, w_out_b[0], tm=tmo_p, seg=t)
    tabs = _rope_tables(jnp.arange(t, dtype=jnp.int32))
    tq = min(256, t)
    (mqg, q, kvc_p, kvs_p, kvw_p, gates, zgs, ksa, kwa, vsb, vwb) = _proj_nsa(
        x1.reshape(b, t, d), norm_g[1], w_nsa_b, tabs, tm=tm, prompt=True)
    kca, vcb = _cmp_prompt(kvc_p, pw0, pw1, phi_bd)
    mix = _attn_prompt(q, kca, vcb, ksa, vsb, kwa, vwb, gates, zgs, tq=tq)
    y_prompt = _layer_out(x1, mqg.reshape(b * t, 512), mem_kv_p[1], mix.reshape(b * t, MIX_WIDTH), w_out_b[1],
                          tm=tmo_p, seg=t, final_g=final_norm_g).reshape(b, t, d)
    w_keep_p = min(WINDOW, t)

    ms = bd * td
    xs = x_sample.reshape(ms, d)
    st = state_conv[0]
    zrow = jnp.zeros((bd, td - 1, MIX_WIDTH), F32)
    s1 = jnp.concatenate([st[:, 1:2], zrow], axis=1).reshape(ms, MIX_WIDTH)
    s2 = jnp.concatenate([st, zrow[:, 1:]], axis=1).reshape(ms, MIX_WIDTH)
    mqg, mix, u_s = _proj_conv(xs, norm_g[0], w_conv_b, conv_w[0], seg=td, tm=ms, state=(s1, s2))
    conv_s = u_s.reshape(bd, td, MIX_WIDTH)[:, td - 2:]
    tmo = 8 * td
    mem_s = _feature_major(cache_mem_kv.reshape((-1,) + cache_mem_kv.shape[2:]))
    x1s = _layer_out(xs, mqg, mem_s, mix, w_out_b[0], tm=tmo, seg=td, feature_major=True)
    tabs_s = _rope_tables(jnp.tile(past_len + jnp.arange(td, dtype=jnp.int32), bd))
    (mqg, q_s, kvc_s, kvs_s, kvw_s, gates_s, zgs_s) = _proj_nsa(
        x1s.reshape(1, ms, d), norm_g[1], w_nsa_b, tabs_s, tm=ms, prompt=False)
    pg = min(32, n_pages)
    kca_s, vcb_s = _cmp_sample(_feature_major(cache_cmp_kv[0]), page_table, pw0, pw1, phi_bd, pg)
    mix_s = _attn_sample(q_s, kca_s, vcb_s, kvs_s, kvw_s, _feature_major(state_win_kv[0]), gates_s, zgs_s,
                         _feature_major(cache_sel_kv[0]), page_table, td, pg)
    y_sample = _layer_out(x1s, mqg.reshape(ms, 512), mem_s, mix_s.reshape(ms, MIX_WIDTH), w_out_b[1], tm=tmo, seg=td,
                          final_g=final_norm_g, feature_major=True, memkv_offset=bd).reshape(bd, td, d)
    w_keep = state_win_kv.shape[2]
    win_s = jnp.concatenate([state_win_kv[0], kvw_s.reshape(bd, td, 2, KV_HEADS, HEAD_DIM)], axis=1)[:, -w_keep:]

    return (y_prompt, y_sample, conv_p[None],
            kv6(kvc_p, b, t), kv6(kvs_p, b, t), kv6(kvw_p[:, t - w_keep_p:], b, w_keep_p),
            mem_kv_p.reshape(2, b, n_mem, 2, MEM_WIDTH // HEAD_DIM, HEAD_DIM),
            conv_s[None], kv6(kvc_s, bd, td), kv6(kvs_s, bd, td), win_s[None])
```

```python
import functools

import numpy as np
import jax
import jax.numpy as jnp
from jax import lax
from jax.experimental import pallas as pl
from jax.experimental.pallas import tpu as pltpu

F32 = jnp.float32
BF16 = jnp.bfloat16

D_MODEL = 1024
HEAD_DIM = 64
MIX_WIDTH = 768
MEM_WIDTH = 256
N_HEADS = 12
KV_HEADS = 4
GROUP = 3
KV_WIDTH = 256
CMP_BLOCK = 32
CMP_STRIDE = 16
SEL_BLOCK = 64
N_SEL = 16
WINDOW = 512
ROT_DIM = 16
ROPE_THETA = 500000.0
NORM_EPS = 1e-6
FORCE_BONUS = 1e4
PAGE_SIZE = 128
SCALE = HEAD_DIM ** -0.5
NEG = -(2.0 ** 100)
M_INIT = -1e30
LANES = 128
VISIBILITY_VARIANTS = 4
SUM_ROWS = 16
PAGES_PER_UPDATE = 8
VMEM_LIMIT = 56 * 2 ** 20


def _cparams(*sem):
    return pltpu.CompilerParams(dimension_semantics=sem, vmem_limit_bytes=VMEM_LIMIT)


def _dot(a, b):
    return jnp.dot(a, b, preferred_element_type=F32)


def _dot_nt(a, b):
    return lax.dot_general(a, b, (((1,), (1,)), ((), ())), preferred_element_type=F32)


def _rms(x, g):
    return x * lax.rsqrt(jnp.mean(x * x, axis=-1, keepdims=True) + NORM_EPS) * g


def _silu(x):
    return x * jax.nn.sigmoid(x)


def _split3(a):
    hi = a.astype(BF16)
    r1 = a - hi.astype(F32)
    mid = r1.astype(BF16)
    lo = (r1 - mid.astype(F32)).astype(BF16)
    return hi, mid, lo


def _lo_half(rows):
    return lax.broadcasted_iota(jnp.int32, (rows, LANES), 1) < HEAD_DIM


def _memkv_kernel(mem_ref, g_ref, w_ref, o_ref):
    h = _rms(mem_ref[0], g_ref[0]).astype(BF16)
    o_ref[0, 0] = _dot(h, w_ref[0])


def _memory_kv(mem, mem_norm_g, w_mem_kv_b):
    depth = w_mem_kv_b.shape[0]
    b, n_mem, d = mem.shape
    return pl.pallas_call(
        _memkv_kernel,
        grid=(depth, b),
        in_specs=[pl.BlockSpec((1, n_mem, d), lambda i, j: (j, 0, 0)),
                  pl.BlockSpec((1, 1, d), lambda i, j: (i, 0, 0)),
                  pl.BlockSpec((1, d, 2 * MEM_WIDTH), lambda i, j: (i, 0, 0))],
        out_specs=pl.BlockSpec((1, 1, n_mem, 2 * MEM_WIDTH), lambda i, j: (i, j, 0, 0)),
        out_shape=jax.ShapeDtypeStruct((depth, b, n_mem, 2 * MEM_WIDTH), F32),
        compiler_params=_cparams("arbitrary", "arbitrary"),
        name="memory_kv",
    )(mem, mem_norm_g.reshape(depth, 1, d), w_mem_kv_b)


def _proj_conv_kernel(*refs, tm, seg, has_state):
    if has_state:
        x_ref, g_ref, w_ref, cw_ref, s1_ref, s2_ref, mqg_ref, mix_ref, st_ref, ubuf = refs
    else:
        x_ref, g_ref, w_ref, cw_ref, mqg_ref, mix_ref, st_ref, ubuf = refs
    i = pl.program_id(0)
    h = _rms(x_ref[...], g_ref[...]).astype(BF16)
    mqg_ref[...] = _dot(h, w_ref[:, 0:512])
    bg = _dot(h, w_ref[:, 512:1280])
    cg = _dot(h, w_ref[:, 1280:2048])
    hin = _dot(h, w_ref[:, 2048:2816])
    zg = _dot(h, w_ref[:, 2816:3584])
    u = cg * hin

    @pl.when(i == 0)
    def _():
        ubuf[0:8, :] = jnp.zeros((8, MIX_WIDTH), F32)

    @pl.when(i > 0)
    def _():
        ubuf[0:8, :] = ubuf[tm:tm + 8, :]

    ubuf[8:8 + tm, :] = u
    u1 = ubuf[7:7 + tm, :]
    u2 = ubuf[6:6 + tm, :]
    rowpos = lax.rem(i * tm + lax.broadcasted_iota(jnp.int32, (tm, 1), 0), seg)
    if has_state:
        u1 = jnp.where(rowpos >= 1, u1, s1_ref[...])
        u2 = jnp.where(rowpos >= 2, u2, s2_ref[...])
        st_ref[...] = u
    else:
        u1 = jnp.where(rowpos >= 1, u1, 0.0)
        u2 = jnp.where(rowpos >= 2, u2, 0.0)
        st_ref[0] = ubuf[8 + tm - 2:8 + tm, :]
    y = cw_ref[0:1, :] * u2 + cw_ref[1:2, :] * u1 + cw_ref[2:3, :] * u
    mix_ref[...] = (_silu(zg) * (bg * y)).astype(mix_ref.dtype)


def _proj_conv(x2d, g, w_b, cw, seg, tm, state=None):
    m = x2d.shape[0]
    n_in = w_b.shape[1]
    has_state = state is not None
    in_specs = [pl.BlockSpec((tm, D_MODEL), lambda i: (i, 0)),
                pl.BlockSpec((1, D_MODEL), lambda i: (0, 0)),
                pl.BlockSpec((D_MODEL, n_in), lambda i: (0, 0)),
                pl.BlockSpec((3, MIX_WIDTH), lambda i: (0, 0))]
    args = [x2d, g.reshape(1, D_MODEL), w_b, cw]
    if has_state:
        in_specs += [pl.BlockSpec((tm, MIX_WIDTH), lambda i: (i, 0))] * 2
        args += list(state)
        st_shape = jax.ShapeDtypeStruct((m, MIX_WIDTH), F32)
        st_spec = pl.BlockSpec((tm, MIX_WIDTH), lambda i: (i, 0))
    else:
        st_shape = jax.ShapeDtypeStruct((m // seg, 2, MIX_WIDTH), F32)
        st_spec = pl.BlockSpec((1, 2, MIX_WIDTH), lambda i: ((i * tm) // seg, 0, 0))
    return pl.pallas_call(
        functools.partial(_proj_conv_kernel, tm=tm, seg=seg, has_state=has_state),
        grid=(m // tm,),
        in_specs=in_specs,
        out_specs=[pl.BlockSpec((tm, 512), lambda i: (i, 0)),
                   pl.BlockSpec((tm, MIX_WIDTH), lambda i: (i, 0)),
                   st_spec],
        out_shape=[jax.ShapeDtypeStruct((m, 512), F32),
                   jax.ShapeDtypeStruct((m, MIX_WIDTH), F32),
                   st_shape],
        scratch_shapes=[pltpu.VMEM((tm + 8, MIX_WIDTH), F32)],
        compiler_params=_cparams("arbitrary"),
        name="proj_conv",
    )(*args)


def _out_kernel(*refs, nseg, seg, final, feature_major):
    if final:
        x_ref, mqg_ref, memkv_ref, mix_ref, w_ref, g_ref, o_ref, mo_sc = refs
    else:
        x_ref, mqg_ref, memkv_ref, mix_ref, w_ref, o_ref, mo_sc = refs
    lo = _lo_half(seg)

    def seg_body(s, r0):
        for pair in range(2):
            cols = slice(pair * LANES, (pair + 1) * LANES)
            qp = mqg_ref[pl.ds(r0, seg), cols]
            vcols = slice(MEM_WIDTH + pair * LANES, MEM_WIDTH + (pair + 1) * LANES)
            if feature_major:
                kp, vp = memkv_ref[s, cols, :].astype(BF16), memkv_ref[s, vcols, :].astype(BF16)
            else:
                kp, vp = memkv_ref[s, :, cols].astype(BF16), memkv_ref[s, :, vcols].astype(BF16)
            outs = []
            for half in range(2):
                keep = lo if half == 0 else jnp.logical_not(lo)
                qm = jnp.where(keep, qp, 0.0).astype(BF16)
                sc = (_dot(qm, kp) if feature_major else _dot_nt(qm, kp)) * SCALE
                e = jnp.exp(sc - jnp.max(sc, axis=-1, keepdims=True))
                p = e / jnp.sum(e, axis=-1, keepdims=True)
                outs.append(_dot_nt(p.astype(BF16), vp) if feature_major else _dot(p.astype(BF16), vp))
            mo_sc[pl.ds(r0, seg), cols] = jnp.where(lo, outs[0], outs[1])

    if nseg == 1:
        seg_body(0, 0)
    else:
        def body(s, c):
            seg_body(s, pl.multiple_of(s * seg, seg))
            return c
        lax.fori_loop(0, nseg, body, 0)

    mg = mqg_ref[:, MEM_WIDTH:2 * MEM_WIDTH]
    a = (_silu(mg) * mo_sc[...]).astype(BF16)
    xn = x_ref[...] + (_dot(a, w_ref[0:MEM_WIDTH, :]) + _dot(mix_ref[...].astype(BF16), w_ref[MEM_WIDTH:, :]))
    if final:
        o_ref[...] = _rms(xn, g_ref[...])
    else:
        o_ref[...] = xn


def _layer_out(x2d, mqg, memkv, mix, w_out_b, tm, seg, final_g=None, feature_major=False, memkv_offset=0):
    m = x2d.shape[0]
    nseg = max(tm // seg, 1)
    seg_in = min(seg, tm)
    final = final_g is not None
    in_specs = [pl.BlockSpec((tm, D_MODEL), lambda i: (i, 0)),
                pl.BlockSpec((tm, 512), lambda i: (i, 0)),
                pl.BlockSpec((nseg,) + memkv.shape[1:], lambda i: (memkv_offset // nseg + (i * tm) // (seg * nseg), 0, 0)),
                pl.BlockSpec((tm, MIX_WIDTH), lambda i: (i, 0)),
                pl.BlockSpec((D_MODEL, D_MODEL), lambda i: (0, 0))]
    args = [x2d, mqg, memkv, mix, w_out_b]
    if final:
        in_specs.append(pl.BlockSpec((1, D_MODEL), lambda i: (0, 0)))
        args.append(final_g.reshape(1, D_MODEL))
    return pl.pallas_call(
        functools.partial(_out_kernel, nseg=nseg, seg=seg_in, final=final, feature_major=feature_major),
        grid=(m // tm,),
        in_specs=in_specs,
        out_specs=pl.BlockSpec((tm, D_MODEL), lambda i: (i, 0)),
        out_shape=jax.ShapeDtypeStruct((m, D_MODEL), F32),
        scratch_shapes=[pltpu.VMEM((tm, MEM_WIDTH), F32)],
        compiler_params=_cparams("arbitrary"),
        name="layer_out",
    )(*args)


_C_Q, _C_KC, _C_KS, _C_KW, _C_ZG, _C_GL, _C_END = 512, 1280, 1792, 2304, 2816, 3584, 3712


def _proj_nsa_kernel(*refs, tm, prompt):
    (x_ref, g_ref, w_ref, c_ref, sa_ref, sb_ref,
     mqg_ref, q_ref, kvc_ref, kvs_ref, kvw_ref, gates_ref, zgs_ref) = refs[:13]
    ti = pl.program_id(1)
    h = _rms(x_ref[0], g_ref[...]).astype(BF16)
    cos, sa, sb = c_ref[...], sa_ref[...], sb_ref[...]
    lo = _lo_half(tm)

    def rope(chunk):
        return chunk * cos + pltpu.roll(chunk, LANES - ROT_DIM // 2, axis=1) * sa + pltpu.roll(chunk, ROT_DIM // 2, axis=1) * sb

    mqg_ref[0] = _dot(h, w_ref[:, 0:_C_Q])
    qf = _dot(h, w_ref[:, _C_Q:_C_KC])
    for c in range(N_HEADS // 2):
        qc = rope(qf[:, c * LANES:(c + 1) * LANES]) * SCALE
        if prompt:
            q_ref[0, c * LANES:(c + 1) * LANES, :] = qc.T.astype(q_ref.dtype)
        else:
            q_ref[0, 2 * c] = jnp.where(lo, qc, 0.0).astype(q_ref.dtype)
            q_ref[0, 2 * c + 1] = jnp.where(lo, pltpu.roll(qc, HEAD_DIM, axis=1), 0.0).astype(q_ref.dtype)

    if prompt:
        ksa_ref, kwa_ref, vsb_ref, vwb_ref = refs[13:17]
        pos = ti * tm + lax.broadcasted_iota(jnp.int32, (tm, LANES), 0)
        lane = lax.broadcasted_iota(jnp.int32, (tm, LANES), 1)
        onehot = jnp.where(jnp.right_shift(pos, 6) == lane - HEAD_DIM, 1.0, 0.0)

    for name, off in (("c", _C_KC), ("s", _C_KS), ("w", _C_KW)):
        kv_ref = {"c": kvc_ref, "s": kvs_ref, "w": kvw_ref}[name]
        kk = _dot(h, w_ref[:, off:off + KV_WIDTH])
        vv = _dot(h, w_ref[:, off + KV_WIDTH:off + 2 * KV_WIDTH])
        kv_ref[0, :, KV_WIDTH:2 * KV_WIDTH] = vv
        for c in range(KV_HEADS // 2):
            kr = rope(kk[:, c * LANES:(c + 1) * LANES])
            kv_ref[0, :, c * LANES:(c + 1) * LANES] = kr
            if prompt and name != "c":
                pad = onehot if name == "s" else 0.0
                aug_ref = ksa_ref if name == "s" else kwa_ref
                aug_ref[0, 2 * c] = jnp.where(lo, kr, pad).astype(BF16)
                aug_ref[0, 2 * c + 1] = jnp.where(lo, pltpu.roll(kr, HEAD_DIM, axis=1), pad).astype(BF16)
        if prompt and name != "c":
            vt_ref = vsb_ref if name == "s" else vwb_ref
            for c in range(KV_WIDTH // LANES):
                vt_ref[0, c * LANES:(c + 1) * LANES, :] = vv[:, c * LANES:(c + 1) * LANES].T.astype(BF16)

    gates = jax.nn.sigmoid(_dot(h, w_ref[:, _C_GL:_C_END]))
    gates_ref[0] = gates.T if prompt else gates
    zgs_ref[0] = _silu(_dot(h, w_ref[:, _C_ZG:_C_GL]))


def _proj_nsa(x3d, g, w_b, tabs, tm, prompt):
    nb, t, _ = x3d.shape
    row = lambda w: pl.BlockSpec((1, tm, w), lambda b, i: (b, i, 0))
    head = lambda n: pl.BlockSpec((1, n, tm, LANES), lambda b, i: (b, 0, i, 0))
    tab = pl.BlockSpec((tm, LANES), lambda b, i: (i, 0))
    sds = jax.ShapeDtypeStruct
    col = lambda w: pl.BlockSpec((1, w, tm), lambda b, i: (b, 0, i))
    if prompt:
        q_spec, q_shape = col(MIX_WIDTH), sds((nb, MIX_WIDTH, t), BF16)
        g_spec, g_shape = col(LANES), sds((nb, LANES, t), F32)
    else:
        q_spec, q_shape = head(N_HEADS), sds((nb, N_HEADS, t, LANES), F32)
        g_spec, g_shape = row(LANES), sds((nb, t, LANES), F32)
    out_specs = [row(512), q_spec, row(512), row(512), row(512), g_spec, row(MIX_WIDTH)]
    out_shape = [sds((nb, t, 512), F32), q_shape,
                 sds((nb, t, 512), F32), sds((nb, t, 512), F32), sds((nb, t, 512), F32),
                 g_shape, sds((nb, t, MIX_WIDTH), F32)]
    if prompt:
        out_specs += [head(KV_HEADS), head(KV_HEADS), col(KV_WIDTH), col(KV_WIDTH)]
        out_shape += [sds((nb, KV_HEADS, t, LANES), BF16), sds((nb, KV_HEADS, t, LANES), BF16),
                      sds((nb, KV_WIDTH, t), BF16), sds((nb, KV_WIDTH, t), BF16)]
    return pl.pallas_call(
        functools.partial(_proj_nsa_kernel, tm=tm, prompt=prompt),
        grid=(nb, t // tm),
        in_specs=[pl.BlockSpec((1, tm, D_MODEL), lambda b, i: (b, i, 0)),
                  pl.BlockSpec((1, D_MODEL), lambda b, i: (0, 0)),
                  pl.BlockSpec((D_MODEL, _C_END), lambda b, i: (0, 0)),
                  tab, tab, tab],
        out_specs=out_specs,
        out_shape=out_shape,
        compiler_params=_cparams("arbitrary", "arbitrary"),
        name="proj_nsa",
    )(x3d, g.reshape(1, D_MODEL), w_b, *tabs)


def _rope_tables(pos):
    half = ROT_DIM // 2
    inv = ROPE_THETA ** (-jnp.arange(half, dtype=F32) * 2.0 / ROT_DIM)
    ang = pos.astype(F32)[:, None] * inv[None, :]
    cos, sin = jnp.cos(ang), jnp.sin(ang)
    n = pos.shape[0]
    one = jnp.ones((n, HEAD_DIM - ROT_DIM), F32)
    zero = jnp.zeros((n, HEAD_DIM - ROT_DIM), F32)
    zh = jnp.zeros((n, half), F32)
    c = jnp.concatenate([cos, cos, one], axis=1)
    sa = jnp.concatenate([-sin, zh, zero], axis=1)
    sb = jnp.concatenate([zh, sin, zero], axis=1)
    return tuple(jnp.tile(a, (1, LANES // HEAD_DIM)) for a in (c, sa, sb))


def _emit_cmp_blocks(out, kca_ref, v_ref, transpose_v):
    n = out.shape[0]
    lo = _lo_half(n)
    for c in range(KV_HEADS // 2):
        kc = out[:, c * LANES:(c + 1) * LANES]
        kca_ref[0, 2 * c] = jnp.where(lo, kc, 0.0).astype(BF16)
        kca_ref[0, 2 * c + 1] = jnp.where(lo, pltpu.roll(kc, HEAD_DIM, axis=1), 0.0).astype(BF16)
        if transpose_v:
            v_ref[0, c * LANES:(c + 1) * LANES, :] = out[:, KV_WIDTH + c * LANES:KV_WIDTH + (c + 1) * LANES].T.astype(BF16)
    if not transpose_v:
        v_ref[0] = out[:, KV_WIDTH:].astype(BF16)


def _cmp_prompt_kernel(kvc_ref, pw0_ref, pw1_ref, phi_ref, kca_ref, vcb_ref, p0_sc, p1_sc, *, rows):
    t = kvc_ref.shape[1]
    cpr = rows // CMP_STRIDE
    pw0 = pw0_ref[...][None]
    pw1 = pw1_ref[...][None]
    for i in range(t // rows):
        x = kvc_ref[0, i * rows:(i + 1) * rows, :].reshape(cpr, CMP_STRIDE, 2 * KV_WIDTH)
        p0_sc[i * cpr:(i + 1) * cpr, :] = jnp.sum(x * pw0, axis=1)
        p1_sc[i * cpr:(i + 1) * cpr, :] = jnp.sum(x * pw1, axis=1)
    nck = t // CMP_STRIDE
    blk = p0_sc[...] + pltpu.roll(p1_sc[...], nck - 1, axis=0)
    _emit_cmp_blocks(_dot(blk.astype(BF16), phi_ref[...]), kca_ref, vcb_ref, transpose_v=True)


def _cmp_prompt(kvc, pw0, pw1, phi_bd):
    b, t, _ = kvc.shape
    nck = t // CMP_STRIDE
    full = lambda s: pl.BlockSpec(s, lambda i: (0,) * len(s))
    return pl.pallas_call(
        functools.partial(_cmp_prompt_kernel, rows=512),
        grid=(b,),
        in_specs=[pl.BlockSpec((1, t, 2 * KV_WIDTH), lambda i: (i, 0, 0)),
                  full((CMP_STRIDE, 2 * KV_WIDTH)), full((CMP_STRIDE, 2 * KV_WIDTH)),
                  full((2 * KV_WIDTH, 2 * KV_WIDTH))],
        out_specs=[pl.BlockSpec((1, KV_HEADS, nck, LANES), lambda i: (i, 0, 0, 0)),
                   pl.BlockSpec((1, KV_WIDTH, nck), lambda i: (i, 0, 0))],
        out_shape=[jax.ShapeDtypeStruct((b, KV_HEADS, nck, LANES), BF16),
                   jax.ShapeDtypeStruct((b, KV_WIDTH, nck), BF16)],
        scratch_shapes=[pltpu.VMEM((nck, 2 * KV_WIDTH), F32), pltpu.VMEM((nck, 2 * KV_WIDTH), F32)],
        compiler_params=_cparams("arbitrary"),
        name="cmp_prompt",
    )(kvc, pw0, pw1, phi_bd)


def _cmp_sample_kernel(pt_ref, w0_ref, w1_ref, seg_ref, phi_ref, *refs, pg):
    pages = refs[:pg]
    kca_ref, vcb_ref, p0_sc, p1_sc = refs[pg:]
    step = pl.program_id(1)
    cpp = PAGE_SIZE // CMP_STRIDE
    for k in range(pg):
        xt = pages[k][0]
        r0 = pl.multiple_of((step * pg + k) * cpp, cpp)
        for w_ref, p_sc in ((w0_ref, p0_sc), (w1_ref, p1_sc)):
            y = (xt * w_ref[...]).astype(BF16)
            p_sc[pl.ds(r0, cpp), :] = _dot_nt(seg_ref[...], y)[0:cpp]

    @pl.when(step == pl.num_programs(1) - 1)
    def _():
        nck = p0_sc.shape[0]
        blk = p0_sc[...] + pltpu.roll(p1_sc[...], nck - 1, axis=0)
        _emit_cmp_blocks(_dot(blk.astype(BF16), phi_ref[...]), kca_ref, vcb_ref, transpose_v=False)


def _page_specs(pg):
    return [pl.BlockSpec((1, 2 * KV_WIDTH, PAGE_SIZE), functools.partial(
        lambda b, s, pt, k: (pt[b, s * pg + k], 0, 0), k=k)) for k in range(pg)]


def _cmp_sample(cache_t, page_table, pw0, pw1, phi_bd, pg):
    bd, n_pages = page_table.shape
    nck = n_pages * PAGE_SIZE // CMP_STRIDE
    cpp = PAGE_SIZE // CMP_STRIDE
    w0, w1 = (jnp.tile(pw.T, (1, cpp)) for pw in (pw0, pw1))
    seg = np.zeros((max(cpp, 16), PAGE_SIZE), np.float32)
    seg[np.arange(PAGE_SIZE) // CMP_STRIDE, np.arange(PAGE_SIZE)] = 1.0
    full = lambda s: pl.BlockSpec(s, lambda b, i, pt: (0,) * len(s))
    grid_spec = pltpu.PrefetchScalarGridSpec(
        num_scalar_prefetch=1,
        grid=(bd, n_pages // pg),
        in_specs=[full(w0.shape), full(w1.shape), full(seg.shape),
                  full((2 * KV_WIDTH, 2 * KV_WIDTH))] + _page_specs(pg),
        out_specs=[pl.BlockSpec((1, KV_HEADS, nck, LANES), lambda b, i, pt: (b, 0, 0, 0)),
                   pl.BlockSpec((1, nck, KV_WIDTH), lambda b, i, pt: (b, 0, 0))],
        scratch_shapes=[pltpu.VMEM((nck, 2 * KV_WIDTH), F32), pltpu.VMEM((nck, 2 * KV_WIDTH), F32)],
    )
    return pl.pallas_call(
        functools.partial(_cmp_sample_kernel, pg=pg),
        grid_spec=grid_spec,
        out_shape=[jax.ShapeDtypeStruct((bd, KV_HEADS, nck, LANES), BF16),
                   jax.ShapeDtypeStruct((bd, nck, KV_WIDTH), BF16)],
        compiler_params=_cparams("arbitrary", "arbitrary"),
        name="cmp_sample",
    )(page_table, w0, w1, jnp.asarray(seg, BF16), phi_bd, *([cache_t] * pg))


def _imp_matrix(nck, n_sel_blocks):
    cps = SEL_BLOCK // CMP_STRIDE
    c = np.arange(nck)[:, None]
    j = np.arange(n_sel_blocks)[None, :]
    m = ((c >= cps * j) & (c <= cps * j + cps - 1)).astype(np.float32)
    m += ((c + 1 >= cps * j) & (c + 1 <= cps * j + cps - 1)).astype(np.float32)
    return m


def _softmax_masked(s, mask, axis=-1):
    s = jnp.where(mask, s, -1e30)
    p = jnp.exp(s - jnp.max(s, axis=axis, keepdims=True)) * mask.astype(F32)
    return p / jnp.maximum(jnp.sum(p, axis=axis, keepdims=True), 1e-30)


def _online_update_t(s, vt_ones, m_sc, acc_sc):
    m_old = m_sc[...]
    m_new = jnp.maximum(m_old, jnp.max(s, axis=0, keepdims=True))
    p = jnp.exp((s - m_new).astype(BF16))
    acc_sc[...] = jnp.exp(m_old - m_new) * acc_sc[...] + _dot(vt_ones, p)
    m_sc[...] = m_new


def _reset_online(m_sc, l_sc, acc_sc):
    m_sc[...] = jnp.full(m_sc.shape, M_INIT, F32)
    l_sc[...] = jnp.zeros(l_sc.shape, F32)
    acc_sc[...] = jnp.zeros(acc_sc.shape, F32)


def _attn_prompt_kernel(q_ref, kca_ref, vcb_ref, ksa_ref, vsb_ref, kwa_ref, vwb_ref, gates_ref, zgs_ref,
                        m2t_ref, o_ref, m_sc, acc_sc, score_sc, sa_sc, sb_sc, oc_sc, pen_sc, *, tq, n_sel):
    qi = pl.program_id(1)
    s0 = qi * tq
    cols = GROUP * tq
    nsb = m2t_ref.shape[0]
    qpos = s0 + lax.broadcasted_iota(jnp.int32, (1, tq), 1)
    qpos3 = jnp.concatenate([qpos] * GROUP, axis=1)
    gates = gates_ref[0]
    zero_h = jnp.zeros((HEAD_DIM, tq), BF16)
    mix_t = []
    for g in range(KV_HEADS):
        vrows = slice(g * HEAD_DIM, (g + 1) * HEAD_DIM)
        qh = [q_ref[0, (GROUP * g + r) * HEAD_DIM:(GROUP * g + r + 1) * HEAD_DIM, :] for r in range(GROUP)]
        qc = jnp.concatenate([jnp.concatenate([x, zero_h], axis=0) for x in qh], axis=1)

        def compressed_and_select(rows, slots):
            sc = _dot(kca_ref[0, g, 0:rows, :], qc)
            blk_end = lax.broadcasted_iota(jnp.int32, (rows, 1), 0) * CMP_STRIDE + (CMP_BLOCK - 1)
            pc = _softmax_masked(sc, blk_end <= qpos3, axis=0)
            oc_sc[...] = _dot(vcb_ref[0, vrows, 0:rows], pc.astype(BF16))
            imp = pc[:, 0:tq] + pc[:, tq:2 * tq] + pc[:, 2 * tq:3 * tq]

            imp_t = None
            for part in _split3(imp):
                d = _dot(m2t_ref[0:slots, 0:rows], part)
                imp_t = d if imp_t is None else imp_t + d
            j = lax.broadcasted_iota(jnp.int32, (slots, tq), 0)
            cur = jnp.right_shift(s0 + lax.broadcasted_iota(jnp.int32, (slots, tq), 1), 6)
            valid = j <= cur
            forced = (j == 0) | (j == cur) | (j == cur - 1)
            score = jnp.where(valid, imp_t + jnp.where(forced, FORCE_BONUS, 0.0), -jnp.inf)
            score_sc[0:slots, :] = score
            bpt = tq // SEL_BLOCK

            def rank_body(it, cnt):
                for u in range(bpt):
                    i = it * bpt + u
                    row = score_sc[pl.ds(i, 1), :]
                    tie = jnp.where(j > i, 1.0, 0.0)
                    cnt = cnt + jnp.where(row > score, 1.0, jnp.where(row == score, tie, 0.0))
                return cnt

            cnt = lax.fori_loop(0, jnp.minimum(qi + 1, slots // bpt), rank_body, jnp.zeros((slots, tq), F32))
            pen = jnp.where((cnt < float(n_sel)) & valid, 0.0, NEG)
            if slots < nsb:
                pen = jnp.concatenate([pen, jnp.full((nsb - slots, tq), NEG, F32)], axis=0)
            pen_sc[...] = pen.astype(BF16)

        ncp = kca_ref.shape[2]
        n_var = VISIBILITY_VARIANTS if ncp % (VISIBILITY_VARIANTS * 4 * 16) == 0 and nsb * 4 == ncp else 1
        chunk = ncp // n_var
        visible = (s0 + tq - CMP_BLOCK) // CMP_STRIDE + 1
        variant = jnp.clip((visible + chunk - 1) // chunk, 1, n_var)
        for k in range(1, n_var + 1):
            pl.when(variant == k)(functools.partial(compressed_and_select, chunk * k, nsb * k // n_var))
        oc = oc_sc[...]
        pen = pen_sc[...]
        qa = jnp.concatenate([jnp.concatenate([x, pen], axis=0) for x in qh], axis=1)

        def tile_start(kt):
            return pl.multiple_of(kt * tq, tq)

        def key_pos(kt):
            return tile_start(kt) + lax.broadcasted_iota(jnp.int32, (tq, 1), 0)

        def run_branch(scores, values, first, last_mask):
            m_sc[...] = jnp.full(m_sc.shape, M_INIT, F32)
            acc_sc[...] = jnp.zeros(acc_sc.shape, F32)
            ones = jnp.ones((SUM_ROWS, tq), BF16)

            def update(s_ref, kt, mask=None):
                s = s_ref[...] if mask is None else mask(s_ref[...])
                _online_update_t(s, jnp.concatenate([values(kt), ones], axis=0), m_sc, acc_sc)

            def pair(kt):
                sb_sc[...] = scores(kt + 1)
                update(sa_sc, kt)
                sa_sc[...] = scores(kt + 2)
                update(sb_sc, kt + 1)

            def trip(i, c):
                pair(first + 4 * i)
                pair(first + 4 * i + 2)
                return c

            ahead = qi - first
            n_pairs = ahead // 2
            sa_sc[...] = scores(first)
            lax.fori_loop(0, n_pairs // 2, trip, 0)

            @pl.when(n_pairs % 2 == 1)
            def _():
                pair(first + 2 * (n_pairs - 1))

            @pl.when(ahead % 2 == 1)
            def _():
                sb_sc[...] = scores(qi)
                update(sa_sc, qi - 1)
                update(sb_sc, qi, last_mask)

            @pl.when(ahead % 2 == 0)
            def _():
                update(sa_sc, qi, last_mask)

            return acc_sc[0:HEAD_DIM, :] / acc_sc[HEAD_DIM:HEAD_DIM + 1, :]

        o_s = run_branch(
            lambda kt: _dot(ksa_ref[0, g, pl.ds(tile_start(kt), tq), :], qa),
            lambda kt: vsb_ref[0, vrows, pl.ds(tile_start(kt), tq)],
            0, lambda s: jnp.where(key_pos(qi) <= qpos3, s, NEG))

        def win_scores(kt, lower, causal):
            s = _dot(kwa_ref[0, g, pl.ds(tile_start(kt), tq), :], qc)
            if lower:
                s = jnp.where(key_pos(kt) > qpos3 - WINDOW, s, NEG)
            if causal:
                s = jnp.where(key_pos(kt) <= qpos3, s, NEG)
            return s

        def win_update(s_ref, kt):
            vt_ones = jnp.concatenate([vwb_ref[0, vrows, pl.ds(tile_start(kt), tq)], jnp.ones((SUM_ROWS, tq), BF16)], axis=0)
            _online_update_t(s_ref[...], vt_ones, m_sc, acc_sc)

        m_sc[...] = jnp.full(m_sc.shape, M_INIT, F32)
        acc_sc[...] = jnp.zeros(acc_sc.shape, F32)

        @pl.when(qi >= 2)
        def _():
            sa_sc[...] = win_scores(qi - 2, True, False)
            sb_sc[...] = win_scores(qi - 1, False, False)
            win_update(sa_sc, qi - 2)
            sa_sc[...] = win_scores(qi, False, True)
            win_update(sb_sc, qi - 1)
            win_update(sa_sc, qi)

        @pl.when(qi == 1)
        def _():
            sa_sc[...] = win_scores(0, False, False)
            sb_sc[...] = win_scores(1, False, True)
            win_update(sa_sc, 0)
            win_update(sb_sc, 1)

        @pl.when(qi == 0)
        def _():
            sa_sc[...] = win_scores(0, False, True)
            win_update(sa_sc, 0)

        o_w = acc_sc[0:HEAD_DIM, :] / acc_sc[HEAD_DIM:HEAD_DIM + 1, :]

        for r in range(GROUP):
            hd = GROUP * g + r
            cs = slice(r * tq, (r + 1) * tq)
            mix_t.append(gates[hd:hd + 1, :] * oc[:, cs] + gates[N_HEADS + hd:N_HEADS + hd + 1, :] * o_s[:, cs]
                         + gates[2 * N_HEADS + hd:2 * N_HEADS + hd + 1, :] * o_w[:, cs])
    for c in range(N_HEADS // 2):
        lanes = slice(c * LANES, (c + 1) * LANES)
        mix = jnp.concatenate([mix_t[2 * c], mix_t[2 * c + 1]], axis=0).T
        o_ref[0, :, lanes] = (zgs_ref[0, :, lanes] * mix).astype(o_ref.dtype)


def _attn_prompt(q, kca, vcb, ksa, vsb, kwa, vwb, gates, zgs, tq):
    b, _, t = q.shape
    nck = kca.shape[2]
    nsb = -(-t // SEL_BLOCK)
    assert WINDOW == 2 * tq or t <= tq
    slots = LANES - HEAD_DIM
    assert nsb <= slots
    m2t = np.zeros((slots, nck), np.float32)
    m2t[:nsb] = _imp_matrix(nck, nsb).T
    per_b4 = lambda n, r: pl.BlockSpec((1, n, r, LANES), lambda i, j: (i, 0, 0, 0))
    per_b3 = lambda r, w: pl.BlockSpec((1, r, w), lambda i, j: (i, 0, 0))
    cols = GROUP * tq
    return pl.pallas_call(
        functools.partial(_attn_prompt_kernel, tq=tq, n_sel=min(N_SEL, nsb)),
        grid=(b, t // tq),
        in_specs=[pl.BlockSpec((1, MIX_WIDTH, tq), lambda i, j: (i, 0, j)),
                  per_b4(KV_HEADS, nck), per_b3(KV_WIDTH, nck),
                  per_b4(KV_HEADS, t), per_b3(KV_WIDTH, t),
                  per_b4(KV_HEADS, t), per_b3(KV_WIDTH, t),
                  pl.BlockSpec((1, LANES, tq), lambda i, j: (i, 0, j)),
                  pl.BlockSpec((1, tq, MIX_WIDTH), lambda i, j: (i, j, 0)),
                  pl.BlockSpec(m2t.shape, lambda i, j: (0, 0))],
        out_specs=pl.BlockSpec((1, tq, MIX_WIDTH), lambda i, j: (i, j, 0)),
        out_shape=jax.ShapeDtypeStruct((b, t, MIX_WIDTH), F32),
        scratch_shapes=[pltpu.VMEM((1, cols), F32), pltpu.VMEM((HEAD_DIM + SUM_ROWS, cols), F32),
                        pltpu.VMEM((slots, tq), F32), pltpu.VMEM((tq, cols), F32), pltpu.VMEM((tq, cols), F32),
                        pltpu.VMEM((HEAD_DIM, cols), F32), pltpu.VMEM((slots, tq), BF16)],
        compiler_params=_cparams("arbitrary", "arbitrary"),
        name="attn_prompt",
    )(q, kca, vcb, ksa, vsb, kwa, vwb, gates, zgs, jnp.asarray(m2t, BF16))


def _online_update(s, vt, m_sc, l_sc, acc_sc):
    m_old = m_sc[...]
    m_new = jnp.maximum(m_old, jnp.max(s, axis=-1, keepdims=True))
    alpha = jnp.exp(m_old - m_new)
    p = jnp.exp(s - m_new)
    l_sc[...] = alpha * l_sc[...] + jnp.sum(p, axis=-1, keepdims=True)
    acc_sc[...] = alpha * acc_sc[...] + _dot_nt(p.astype(BF16), vt)
    m_sc[...] = m_new


def _attn_sample_kernel(pt_ref, q_ref, kca_ref, vcb_ref, ksn_ref, kwn_ref, win_ref, gates_ref, zgs_ref, mmat_ref,
                        *refs, pg, past_len):
    pages = refs[:pg]
    o_ref, qbd_sc, pen_sc, m_sc, l_sc, acc_sc, oc_sc, ow_sc = refs[pg:]
    step = pl.program_id(1)
    t = q_ref.shape[2]
    rows = N_HEADS * t
    rpad = qbd_sc.shape[0]
    npb = mmat_ref.shape[1]
    tok = jnp.bitwise_and(lax.broadcasted_iota(jnp.int32, (rpad, 1), 0), t - 1)
    qpos = past_len + tok

    @pl.when(step == 0)
    def _():
        zero_t = jnp.zeros((t, LANES), F32)
        qrows = []
        for hd in range(N_HEADS):
            g = hd // GROUP
            qh = q_ref[0, hd]
            if g % 2 == 1:
                qh = pltpu.roll(qh, HEAD_DIM, axis=1)
            qrows.append(jnp.concatenate([qh, zero_t] if g // 2 == 0 else [zero_t, qh], axis=1))
        qbd = jnp.concatenate(qrows + [jnp.zeros((rpad - rows, 2 * LANES), F32)], axis=0).astype(BF16)

        imps = []
        rg = GROUP * t
        qpos_g = qpos[0:rg]
        for g in range(KV_HEADS):
            qg = jnp.concatenate([q_ref[0, GROUP * g + r] for r in range(GROUP)], axis=0).astype(BF16)
            sc = _dot_nt(qg, kca_ref[0, g])
            ncp = sc.shape[1]
            blk_end = lax.broadcasted_iota(jnp.int32, (1, ncp), 1) * CMP_STRIDE + (CMP_BLOCK - 1)
            pc = _softmax_masked(sc, blk_end <= qpos_g)
            oc_sc[g * rg:(g + 1) * rg, :] = _dot(pc.astype(BF16), vcb_ref[0])
            imps.append(pc[0:t] + pc[t:2 * t] + pc[2 * t:3 * t])
        oc_sc[rows:, :] = jnp.zeros((rpad - rows, 2 * LANES), F32)
        imp = jnp.concatenate(imps, axis=0)
        imp_s = None
        for part in _split3(imp):
            d = _dot(part, mmat_ref[...])
            imp_s = d if imp_s is None else imp_s + d
        ngt = KV_HEADS * t
        width = npb + LANES
        base = jnp.concatenate([imp_s, jnp.zeros((ngt, LANES), F32)], axis=1)
        j = lax.broadcasted_iota(jnp.int32, (ngt, width), 1)
        cur = jnp.right_shift(past_len + jnp.bitwise_and(lax.broadcasted_iota(jnp.int32, (ngt, width), 0), t - 1), 6)
        valid = j <= cur
        forced = (j == 0) | (j == cur) | (j == cur - 1)
        score = jnp.where(valid, base + jnp.where(forced, FORCE_BONUS, 0.0), -jnp.inf)
        picked = jnp.zeros((ngt, width), F32)
        jf = j.astype(F32)
        for _ in range(N_SEL):
            mx = jnp.max(score, axis=-1, keepdims=True)
            idx = jnp.min(jnp.where(score == mx, jf, float(width)), axis=-1, keepdims=True)
            hit = jf == idx
            picked = jnp.where(hit, 1.0, picked)
            score = jnp.where(hit, -jnp.inf, score)
        pen_gt = jnp.where((picked > 0.5) & valid, 0.0, NEG)
        pen = jnp.concatenate([pen_gt[(hd // GROUP) * t:(hd // GROUP + 1) * t] for hd in range(N_HEADS)]
                              + [jnp.zeros((rpad - rows, width), F32)], axis=0)
        qbd_sc[...] = qbd
        for c in range(width // LANES):
            pen_sc[c * LANES:(c + 1) * LANES, :] = pen[:, c * LANES:(c + 1) * LANES].T.astype(BF16)
        pen_new = pen[:, npb:npb + 1]

        def padded(ref, cols):
            return jnp.concatenate([ref[0, :, cols], jnp.zeros((LANES - t, KV_WIDTH), F32)], axis=0).astype(BF16)

        kcols, vcols = slice(0, KV_WIDTH), slice(KV_WIDTH, 2 * KV_WIDTH)
        inew = lax.broadcasted_iota(jnp.int32, (1, LANES), 1)
        new_ok = (inew < t) & (past_len + inew <= qpos)

        s = jnp.where(new_ok, _dot_nt(qbd, padded(ksn_ref, kcols)) + pen_new, NEG)
        m0 = jnp.maximum(jnp.max(s, axis=-1, keepdims=True), M_INIT)
        p = jnp.exp(s - m0)
        m_sc[...] = m0
        l_sc[...] = jnp.sum(p, axis=-1, keepdims=True)
        acc_sc[...] = _dot(p.astype(BF16), padded(ksn_ref, vcols))

        wk = win_ref.shape[2]
        kw_pos = past_len - wk + lax.broadcasted_iota(jnp.int32, (1, wk), 1)
        mask1 = (kw_pos <= qpos) & (kw_pos > qpos - WINDOW) & (kw_pos >= 0)
        mask2 = new_ok & (past_len + inew > qpos - WINDOW)
        s1 = jnp.where(mask1, _dot(qbd, win_ref[0, kcols, :].astype(BF16)), -1e30)
        s2 = jnp.where(mask2, _dot_nt(qbd, padded(kwn_ref, kcols)), -1e30)
        mw = jnp.maximum(jnp.max(s1, axis=-1, keepdims=True), jnp.max(s2, axis=-1, keepdims=True))
        p1 = jnp.exp(s1 - mw) * mask1.astype(F32)
        p2 = jnp.exp(s2 - mw) * mask2.astype(F32)
        lw = jnp.sum(p1, axis=-1, keepdims=True) + jnp.sum(p2, axis=-1, keepdims=True)
        ow = _dot_nt(p1.astype(BF16), win_ref[0, vcols, :].astype(BF16)) + _dot(p2.astype(BF16), padded(kwn_ref, vcols))
        ow_sc[...] = ow / jnp.maximum(lw, 1e-30)

    bpu = PAGES_PER_UPDATE * (PAGE_SIZE // SEL_BLOCK)
    keys = PAGES_PER_UPDATE * PAGE_SIZE
    onehot = jnp.where(jnp.right_shift(lax.broadcasted_iota(jnp.int32, (bpu, keys), 1), 6)
                       == lax.broadcasted_iota(jnp.int32, (bpu, keys), 0), 1.0, 0.0).astype(BF16)
    for u in range(pg // PAGES_PER_UPDATE):
        blocks = [pages[u * PAGES_PER_UPDATE + i][0] for i in range(PAGES_PER_UPDATE)]
        kt = jnp.concatenate([x[0:KV_WIDTH, :] for x in blocks], axis=1).astype(BF16)
        vt = jnp.concatenate([x[KV_WIDTH:, :] for x in blocks], axis=1).astype(BF16)
        b0 = pl.multiple_of((step * (pg // PAGES_PER_UPDATE) + u) * bpu, bpu)
        bias = lax.dot_general(pen_sc[pl.ds(b0, bpu), :], onehot, (((0,), (0,)), ((), ())), preferred_element_type=F32)
        _online_update(_dot(qbd_sc[...], kt) + bias, vt, m_sc, l_sc, acc_sc)

    @pl.when(step == pl.num_programs(1) - 1)
    def _():
        o_s = acc_sc[...] / l_sc[...]
        lo = _lo_half(t)
        gates = gates_ref[0]
        placed = []
        for hd in range(N_HEADS):
            g = hd // GROUP
            rs = slice(hd * t, (hd + 1) * t)
            cols = slice((g // 2) * LANES, (g // 2 + 1) * LANES)
            mix = (gates[:, hd:hd + 1] * oc_sc[rs, cols] + gates[:, N_HEADS + hd:N_HEADS + hd + 1] * o_s[rs, cols]
                   + gates[:, 2 * N_HEADS + hd:2 * N_HEADS + hd + 1] * ow_sc[rs, cols])
            placed.append(mix if g % 2 == hd % 2 else pltpu.roll(mix, HEAD_DIM, axis=1))
        for c in range(N_HEADS // 2):
            cols = slice(c * LANES, (c + 1) * LANES)
            o_ref[0, :, cols] = zgs_ref[0, :, cols] * jnp.where(lo, placed[2 * c], placed[2 * c + 1])


def _attn_sample(q, kca, vcb, kvs_new, kvw_new, win_state, gates, zgs, cache_sel, page_table, t, pg):
    bd, n_pages = page_table.shape
    past_len = n_pages * PAGE_SIZE
    nck = kca.shape[2]
    npb = past_len // SEL_BLOCK
    mmat = jnp.asarray(_imp_matrix(nck, npb), BF16)
    wk = win_state.shape[2]
    rpad = -(-N_HEADS * t // LANES) * LANES
    tok = lambda w: pl.BlockSpec((1, t, w), lambda b, s, pt: (0, b, 0))
    grid_spec = pltpu.PrefetchScalarGridSpec(
        num_scalar_prefetch=1,
        grid=(bd, n_pages // pg),
        in_specs=[pl.BlockSpec((1, N_HEADS, t, LANES), lambda b, s, pt: (0, 0, b, 0)),
                  pl.BlockSpec((1, KV_HEADS, nck, LANES), lambda b, s, pt: (b, 0, 0, 0)),
                  pl.BlockSpec((1, nck, KV_WIDTH), lambda b, s, pt: (b, 0, 0)),
                  tok(2 * KV_WIDTH), tok(2 * KV_WIDTH),
                  pl.BlockSpec((1, 2 * KV_WIDTH, wk), lambda b, s, pt: (b, 0, 0)),
                  tok(LANES), tok(MIX_WIDTH),
                  pl.BlockSpec(mmat.shape, lambda b, s, pt: (0, 0))] + _page_specs(pg),
        out_specs=tok(MIX_WIDTH),
        scratch_shapes=[pltpu.VMEM((rpad, 2 * LANES), BF16), pltpu.VMEM((npb + LANES, rpad), BF16),
                        pltpu.VMEM((rpad, 1), F32), pltpu.VMEM((rpad, 1), F32),
                        pltpu.VMEM((rpad, 2 * LANES), F32), pltpu.VMEM((rpad, 2 * LANES), F32),
                        pltpu.VMEM((rpad, 2 * LANES), F32)],
    )
    return pl.pallas_call(
        functools.partial(_attn_sample_kernel, pg=pg, past_len=past_len),
        grid_spec=grid_spec,
        out_shape=jax.ShapeDtypeStruct((1, bd * t, MIX_WIDTH), F32),
        compiler_params=_cparams("arbitrary", "arbitrary"),
        name="attn_sample",
    )(page_table, q, kca, vcb, kvs_new, kvw_new, win_state, gates, zgs, mmat, *([cache_sel] * pg))


def _reorder_nsa_weight(w):
    d = w.shape[0]
    n_gate = 3 * N_HEADS
    zg0 = _C_ZG + n_gate
    return jnp.concatenate([w[:, :_C_ZG], w[:, zg0:zg0 + MIX_WIDTH], w[:, _C_ZG:zg0],
                            jnp.zeros((d, LANES - n_gate), w.dtype)], axis=1).astype(BF16)


def _cmp_weights(cmp_pos_w, cmp_phi):
    ratio = CMP_BLOCK // CMP_STRIDE
    pw = cmp_pos_w.reshape(2, ratio, CMP_STRIDE, HEAD_DIM)
    tiles = [jnp.concatenate([jnp.tile(pw[s, m], (1, KV_HEADS)) for s in range(2)], axis=1) for m in range(ratio)]
    eye = jnp.eye(KV_HEADS, dtype=cmp_phi.dtype)
    z = jnp.zeros((KV_WIDTH, KV_WIDTH), cmp_phi.dtype)
    phi_bd = jnp.concatenate([jnp.concatenate([jnp.kron(eye, cmp_phi[0]), z], axis=1),
                              jnp.concatenate([z, jnp.kron(eye, cmp_phi[1])], axis=1)], axis=0).astype(BF16)
    return tiles[0], tiles[1], phi_bd


def _feature_major(cache):
    n, rows = cache.shape[:2]
    return jnp.transpose(cache, (0, 2, 3, 4, 1)).reshape(n, -1, rows)


def kernel(x_prompt, x_sample, mem_prompt, cache_mem_kv, cache_cmp_kv, cache_sel_kv, page_table, state_conv, state_win_kv, norm_g, final_norm_g, mem_norm_g, w_mem_kv, w_in_conv, conv_w, w_in_nsa, cmp_pos_w, cmp_phi, w_out):
    b, t, d = x_prompt.shape
    bd, td, _ = x_sample.shape
    n_mem = mem_prompt.shape[1]
    n_pages = page_table.shape[1]
    past_len = n_pages * PAGE_SIZE
    assert w_in_conv.shape[0] == 1 and w_in_nsa.shape[0] == 1 and w_out.shape[0] == 2
    assert CMP_BLOCK == 2 * CMP_STRIDE and td & (td - 1) == 0 and td < CMP_STRIDE

    w_conv_b = w_in_conv[0].astype(BF16)
    w_nsa_b = _reorder_nsa_weight(w_in_nsa[0])
    w_out_b = w_out.astype(BF16)
    w_mem_b = w_mem_kv.astype(BF16)
    pw0, pw1, phi_bd = _cmp_weights(cmp_pos_w[0], cmp_phi[0])
    kv6 = lambda a, n, r: a.reshape(1, n, r, 2, KV_HEADS, HEAD_DIM)

    mem_kv_p = _memory_kv(mem_prompt, mem_norm_g, w_mem_b)
    xp = x_prompt.reshape(b * t, d)
    tm = min(512, t)
    tmo_p = min(1024, t)
    mqg, mix, conv_p = _proj_conv(xp, norm_g[0], w_conv_b, conv_w[0], seg=t, tm=tm)
    x1 = _layer_out(xp, mqg, mem_kv_p[0], mix, w_out_b[0], tm=tmo_p, seg=t)
    tabs = _rope_tables(jnp.arange(t, dtype=jnp.int32))
    tq = min(256, t)
    (mqg, q, kvc_p, kvs_p, kvw_p, gates, zgs, ksa, kwa, vsb, vwb) = _proj_nsa(
        x1.reshape(b, t, d), norm_g[1], w_nsa_b, tabs, tm=tm, prompt=True)
    kca, vcb = _cmp_prompt(kvc_p, pw0, pw1, phi_bd)
    mix = _attn_prompt(q, kca, vcb, ksa, vsb, kwa, vwb, gates, zgs, tq=tq)
    y_prompt = _layer_out(x1, mqg.reshape(b * t, 512), mem_kv_p[1], mix.reshape(b * t, MIX_WIDTH), w_out_b[1],
                          tm=tmo_p, seg=t, final_g=final_norm_g).reshape(b, t, d)
    w_keep_p = min(WINDOW, t)

    ms = bd * td
    xs = x_sample.reshape(ms, d)
    st = state_conv[0]
    zrow = jnp.zeros((bd, td - 1, MIX_WIDTH), F32)
    s1 = jnp.concatenate([st[:, 1:2], zrow], axis=1).reshape(ms, MIX_WIDTH)
    s2 = jnp.concatenate([st, zrow[:, 1:]], axis=1).reshape(ms, MIX_WIDTH)
    mqg, mix, u_s = _proj_conv(xs, norm_g[0], w_conv_b, conv_w[0], seg=td, tm=ms, state=(s1, s2))
    conv_s = u_s.reshape(bd, td, MIX_WIDTH)[:, td - 2:]
    tmo = 8 * td
    mem_s = _feature_major(cache_mem_kv.reshape((-1,) + cache_mem_kv.shape[2:]))
    x1s = _layer_out(xs, mqg, mem_s, mix, w_out_b[0], tm=tmo, seg=td, feature_major=True)
    tabs_s = _rope_tables(jnp.tile(past_len + jnp.arange(td, dtype=jnp.int32), bd))
    (mqg, q_s, kvc_s, kvs_s, kvw_s, gates_s, zgs_s) = _proj_nsa(
        x1s.reshape(1, ms, d), norm_g[1], w_nsa_b, tabs_s, tm=ms, prompt=False)
    pg = min(32, n_pages)
    kca_s, vcb_s = _cmp_sample(_feature_major(cache_cmp_kv[0]), page_table, pw0, pw1, phi_bd, pg)
    mix_s = _attn_sample(q_s, kca_s, vcb_s, kvs_s, kvw_s, _feature_major(state_win_kv[0]), gates_s, zgs_s,
                         _feature_major(cache_sel_kv[0]), page_table, td, pg)
    y_sample = _layer_out(x1s, mqg.reshape(ms, 512), mem_s, mix_s.reshape(ms, MIX_WIDTH), w_out_b[1], tm=tmo, seg=td,
                          final_g=final_norm_g, feature_major=True, memkv_offset=bd).reshape(bd, td, d)
    w_keep = state_win_kv.shape[2]
    win_s = jnp.concatenate([state_win_kv[0], kvw_s.reshape(bd, td, 2, KV_HEADS, HEAD_DIM)], axis=1)[:, -w_keep:]

    return (y_prompt, y_sample, conv_p[None],
            kv6(kvc_p, b, t), kv6(kvs_p, b, t), kv6(kvw_p[:, t - w_keep_p:], b, w_keep_p),
            mem_kv_p.reshape(2, b, n_mem, 2, MEM_WIDTH // HEAD_DIM, HEAD_DIM),
            conv_s[None], kv6(kvc_s, bd, td), kv6(kvs_s, bd, td), win_s[None])
```

```python
import functools

import numpy as np
import jax
import jax.numpy as jnp
from jax import lax
from jax.experimental import pallas as pl
from jax.experimental.pallas import tpu as pltpu

F32 = jnp.float32
BF16 = jnp.bfloat16

D_MODEL = 1024
HEAD_DIM = 64
MIX_WIDTH = 768
MEM_WIDTH = 256
N_HEADS = 12
KV_HEADS = 4
GROUP = 3
KV_WIDTH = 256
CMP_BLOCK = 32
CMP_STRIDE = 16
SEL_BLOCK = 64
N_SEL = 16
WINDOW = 512
ROT_DIM = 16
ROPE_THETA = 500000.0
NORM_EPS = 1e-6
FORCE_BONUS = 1e4
PAGE_SIZE = 128
SCALE = HEAD_DIM ** -0.5
MQG_WIDTH = 2 * MEM_WIDTH
KV2_WIDTH = 2 * KV_WIDTH
SEL_SHIFT = SEL_BLOCK.bit_length() - 1
_V_B, _V_C, _V_H, _V_Z, _V_END = (MQG_WIDTH + i * MIX_WIDTH for i in range(5))
NEG = -(2.0 ** 100)
M_INIT = -1e30
LANES = 128
VISIBILITY_VARIANTS = 4
SUM_ROWS = 16
PAGES_PER_UPDATE = 8
VMEM_LIMIT = 56 * 2 ** 20


def _cparams(*sem):
    return pltpu.CompilerParams(dimension_semantics=sem, vmem_limit_bytes=VMEM_LIMIT)


def _dot(a, b):
    return jnp.dot(a, b, preferred_element_type=F32)


def _dot_nt(a, b):
    return lax.dot_general(a, b, (((1,), (1,)), ((), ())), preferred_element_type=F32)


def _rms(x, g):
    return x * lax.rsqrt(jnp.mean(x * x, axis=-1, keepdims=True) + NORM_EPS) * g


def _silu(x):
    return x * jax.nn.sigmoid(x)


def _split3(a):
    hi = a.astype(BF16)
    r1 = a - hi.astype(F32)
    mid = r1.astype(BF16)
    lo = (r1 - mid.astype(F32)).astype(BF16)
    return hi, mid, lo


def _lo_half(rows):
    return lax.broadcasted_iota(jnp.int32, (rows, LANES), 1) < HEAD_DIM


def _memkv_kernel(mem_ref, g_ref, w_ref, o_ref):
    h = _rms(mem_ref[0], g_ref[0]).astype(BF16)
    o_ref[0, 0] = _dot(h, w_ref[0])


def _memory_kv(mem, mem_norm_g, w_mem_kv_b):
    depth = w_mem_kv_b.shape[0]
    b, n_mem, d = mem.shape
    return pl.pallas_call(
        _memkv_kernel,
        grid=(depth, b),
        in_specs=[pl.BlockSpec((1, n_mem, d), lambda i, j: (j, 0, 0)),
                  pl.BlockSpec((1, 1, d), lambda i, j: (i, 0, 0)),
                  pl.BlockSpec((1, d, 2 * MEM_WIDTH), lambda i, j: (i, 0, 0))],
        out_specs=pl.BlockSpec((1, 1, n_mem, 2 * MEM_WIDTH), lambda i, j: (i, j, 0, 0)),
        out_shape=jax.ShapeDtypeStruct((depth, b, n_mem, 2 * MEM_WIDTH), F32),
        compiler_params=_cparams("arbitrary", "arbitrary"),
        name="memory_kv",
    )(mem, mem_norm_g.reshape(depth, 1, d), w_mem_kv_b)


def _proj_conv_kernel(*refs, tm, seg, has_state):
    if has_state:
        x_ref, g_ref, w_ref, cw_ref, s1_ref, s2_ref, mqg_ref, mix_ref, st_ref, ubuf = refs
    else:
        x_ref, g_ref, w_ref, cw_ref, mqg_ref, mix_ref, st_ref, ubuf = refs
    i = pl.program_id(0)
    h = _rms(x_ref[...], g_ref[...]).astype(BF16)
    mqg_ref[...] = _dot(h, w_ref[:, 0:_V_B])
    bg = _dot(h, w_ref[:, _V_B:_V_C])
    cg = _dot(h, w_ref[:, _V_C:_V_H])
    hin = _dot(h, w_ref[:, _V_H:_V_Z])
    zg = _dot(h, w_ref[:, _V_Z:_V_END])
    u = cg * hin

    @pl.when(i == 0)
    def _():
        ubuf[0:8, :] = jnp.zeros((8, MIX_WIDTH), F32)

    @pl.when(i > 0)
    def _():
        ubuf[0:8, :] = ubuf[tm:tm + 8, :]

    ubuf[8:8 + tm, :] = u
    u1 = ubuf[7:7 + tm, :]
    u2 = ubuf[6:6 + tm, :]
    rowpos = lax.rem(i * tm + lax.broadcasted_iota(jnp.int32, (tm, 1), 0), seg)
    if has_state:
        u1 = jnp.where(rowpos >= 1, u1, s1_ref[...])
        u2 = jnp.where(rowpos >= 2, u2, s2_ref[...])
        st_ref[...] = u
    else:
        u1 = jnp.where(rowpos >= 1, u1, 0.0)
        u2 = jnp.where(rowpos >= 2, u2, 0.0)
        st_ref[0] = ubuf[8 + tm - 2:8 + tm, :]
    y = cw_ref[0:1, :] * u2 + cw_ref[1:2, :] * u1 + cw_ref[2:3, :] * u
    mix_ref[...] = (_silu(zg) * (bg * y)).astype(mix_ref.dtype)


def _proj_conv(x2d, g, w_b, cw, seg, tm, state=None, mix_dtype=F32):
    m = x2d.shape[0]
    n_in = w_b.shape[1]
    has_state = state is not None
    in_specs = [pl.BlockSpec((tm, D_MODEL), lambda i: (i, 0)),
                pl.BlockSpec((1, D_MODEL), lambda i: (0, 0)),
                pl.BlockSpec((D_MODEL, n_in), lambda i: (0, 0)),
                pl.BlockSpec((3, MIX_WIDTH), lambda i: (0, 0))]
    args = [x2d, g.reshape(1, D_MODEL), w_b, cw]
    if has_state:
        in_specs += [pl.BlockSpec((tm, MIX_WIDTH), lambda i: (i, 0))] * 2
        args += list(state)
        st_shape = jax.ShapeDtypeStruct((m, MIX_WIDTH), F32)
        st_spec = pl.BlockSpec((tm, MIX_WIDTH), lambda i: (i, 0))
    else:
        st_shape = jax.ShapeDtypeStruct((m // seg, 2, MIX_WIDTH), F32)
        st_spec = pl.BlockSpec((1, 2, MIX_WIDTH), lambda i: ((i * tm) // seg, 0, 0))
    return pl.pallas_call(
        functools.partial(_proj_conv_kernel, tm=tm, seg=seg, has_state=has_state),
        grid=(m // tm,),
        in_specs=in_specs,
        out_specs=[pl.BlockSpec((tm, MQG_WIDTH), lambda i: (i, 0)),
                   pl.BlockSpec((tm, MIX_WIDTH), lambda i: (i, 0)),
                   st_spec],
        out_shape=[jax.ShapeDtypeStruct((m, MQG_WIDTH), F32),
                   jax.ShapeDtypeStruct((m, MIX_WIDTH), mix_dtype),
                   st_shape],
        scratch_shapes=[pltpu.VMEM((tm + 8, MIX_WIDTH), F32)],
        compiler_params=_cparams("arbitrary"),
        name="proj_conv",
    )(*args)


def _out_kernel(*refs, nseg, seg, final, feature_major):
    if final:
        x_ref, mqg_ref, memkv_ref, mix_ref, w_ref, g_ref, o_ref, mo_sc = refs
    else:
        x_ref, mqg_ref, memkv_ref, mix_ref, w_ref, o_ref, mo_sc = refs
    lo = _lo_half(seg)

    def seg_body(s, r0):
        for pair in range(2):
            cols = slice(pair * LANES, (pair + 1) * LANES)
            qp = mqg_ref[pl.ds(r0, seg), cols]
            vcols = slice(MEM_WIDTH + pair * LANES, MEM_WIDTH + (pair + 1) * LANES)
            if feature_major:
                kp, vp = memkv_ref[s, cols, :].astype(BF16), memkv_ref[s, vcols, :].astype(BF16)
            else:
                kp, vp = memkv_ref[s, :, cols].astype(BF16), memkv_ref[s, :, vcols].astype(BF16)
            outs = []
            for half in range(2):
                keep = lo if half == 0 else jnp.logical_not(lo)
                qm = jnp.where(keep, qp, 0.0).astype(BF16)
                sc = (_dot(qm, kp) if feature_major else _dot_nt(qm, kp)) * SCALE
                e = jnp.exp(sc - jnp.max(sc, axis=-1, keepdims=True))
                p = e / jnp.sum(e, axis=-1, keepdims=True)
                outs.append(_dot_nt(p.astype(BF16), vp) if feature_major else _dot(p.astype(BF16), vp))
            mo_sc[pl.ds(r0, seg), cols] = jnp.where(lo, outs[0], outs[1])

    if nseg == 1:
        seg_body(0, 0)
    else:
        def body(s, c):
            seg_body(s, pl.multiple_of(s * seg, seg))
            return c
        lax.fori_loop(0, nseg, body, 0)

    mg = mqg_ref[:, MEM_WIDTH:2 * MEM_WIDTH]
    a = (_silu(mg) * mo_sc[...]).astype(BF16)
    xn = x_ref[...] + (_dot(a, w_ref[0:MEM_WIDTH, :]) + _dot(mix_ref[...].astype(BF16), w_ref[MEM_WIDTH:, :]))
    if final:
        o_ref[...] = _rms(xn, g_ref[...])
    else:
        o_ref[...] = xn


def _layer_out(x2d, mqg, memkv, mix, w_out_b, tm, seg, final_g=None, feature_major=False, memkv_offset=0):
    m = x2d.shape[0]
    nseg = max(tm // seg, 1)
    seg_in = min(seg, tm)
    final = final_g is not None
    in_specs = [pl.BlockSpec((tm, D_MODEL), lambda i: (i, 0)),
                pl.BlockSpec((tm, MQG_WIDTH), lambda i: (i, 0)),
                pl.BlockSpec((nseg,) + memkv.shape[1:], lambda i: (memkv_offset // nseg + (i * tm) // (seg * nseg), 0, 0)),
                pl.BlockSpec((tm, MIX_WIDTH), lambda i: (i, 0)),
                pl.BlockSpec((D_MODEL, D_MODEL), lambda i: (0, 0))]
    args = [x2d, mqg, memkv, mix, w_out_b]
    if final:
        in_specs.append(pl.BlockSpec((1, D_MODEL), lambda i: (0, 0)))
        args.append(final_g.reshape(1, D_MODEL))
    return pl.pallas_call(
        functools.partial(_out_kernel, nseg=nseg, seg=seg_in, final=final, feature_major=feature_major),
        grid=(m // tm,),
        in_specs=in_specs,
        out_specs=pl.BlockSpec((tm, D_MODEL), lambda i: (i, 0)),
        out_shape=jax.ShapeDtypeStruct((m, D_MODEL), F32),
        scratch_shapes=[pltpu.VMEM((tm, MEM_WIDTH), F32)],
        compiler_params=_cparams("arbitrary"),
        name="layer_out",
    )(*args)


_C_Q = MQG_WIDTH
_C_KC = _C_Q + MIX_WIDTH
_C_KS, _C_KW, _C_ZG = (_C_KC + i * KV2_WIDTH for i in (1, 2, 3))
_C_GL = _C_ZG + MIX_WIDTH
_C_END = _C_GL + LANES


def _proj_nsa_kernel(*refs, tm, prompt):
    (x_ref, g_ref, w_ref, c_ref, sa_ref, sb_ref,
     mqg_ref, q_ref, kvc_ref, kvs_ref, kvw_ref, gates_ref, zgs_ref) = refs[:13]
    ti = pl.program_id(1)
    h = _rms(x_ref[0], g_ref[...]).astype(BF16)
    cos, sa, sb = c_ref[...], sa_ref[...], sb_ref[...]
    lo = _lo_half(tm)

    def rope(chunk):
        return chunk * cos + pltpu.roll(chunk, LANES - ROT_DIM // 2, axis=1) * sa + pltpu.roll(chunk, ROT_DIM // 2, axis=1) * sb

    mqg_ref[0] = _dot(h, w_ref[:, 0:_C_Q])
    qf = _dot(h, w_ref[:, _C_Q:_C_KC])
    for c in range(N_HEADS // 2):
        qc = rope(qf[:, c * LANES:(c + 1) * LANES]) * SCALE
        if prompt:
            q_ref[0, c * LANES:(c + 1) * LANES, :] = qc.T.astype(q_ref.dtype)
        else:
            q_ref[0, 2 * c] = jnp.where(lo, qc, 0.0).astype(q_ref.dtype)
            q_ref[0, 2 * c + 1] = jnp.where(lo, pltpu.roll(qc, HEAD_DIM, axis=1), 0.0).astype(q_ref.dtype)

    if prompt:
        ksa_ref, kwa_ref, vsb_ref, vwb_ref = refs[13:17]
        pos = ti * tm + lax.broadcasted_iota(jnp.int32, (tm, LANES), 0)
        lane = lax.broadcasted_iota(jnp.int32, (tm, LANES), 1)
        onehot = jnp.where(jnp.right_shift(pos, SEL_SHIFT) == lane - HEAD_DIM, 1.0, 0.0)

    for name, off in (("c", _C_KC), ("s", _C_KS), ("w", _C_KW)):
        kv_ref = {"c": kvc_ref, "s": kvs_ref, "w": kvw_ref}[name]
        kk = _dot(h, w_ref[:, off:off + KV_WIDTH])
        vv = _dot(h, w_ref[:, off + KV_WIDTH:off + 2 * KV_WIDTH])
        kv_ref[0, :, KV_WIDTH:2 * KV_WIDTH] = vv
        for c in range(KV_HEADS // 2):
            kr = rope(kk[:, c * LANES:(c + 1) * LANES])
            kv_ref[0, :, c * LANES:(c + 1) * LANES] = kr
            if prompt and name != "c":
                pad = onehot if name == "s" else 0.0
                aug_ref = ksa_ref if name == "s" else kwa_ref
                aug_ref[0, 2 * c] = jnp.where(lo, kr, pad).astype(BF16)
                aug_ref[0, 2 * c + 1] = jnp.where(lo, pltpu.roll(kr, HEAD_DIM, axis=1), pad).astype(BF16)
        if prompt and name != "c":
            vt_ref = vsb_ref if name == "s" else vwb_ref
            for c in range(KV_WIDTH // LANES):
                vt_ref[0, c * LANES:(c + 1) * LANES, :] = vv[:, c * LANES:(c + 1) * LANES].T.astype(BF16)

    gates = jax.nn.sigmoid(_dot(h, w_ref[:, _C_GL:_C_END]))
    gates_ref[0] = gates.T if prompt else gates
    zgs_ref[0] = _silu(_dot(h, w_ref[:, _C_ZG:_C_GL]))


def _proj_nsa(x3d, g, w_b, tabs, tm, prompt):
    nb, t, _ = x3d.shape
    row = lambda w: pl.BlockSpec((1, tm, w), lambda b, i: (b, i, 0))
    head = lambda n: pl.BlockSpec((1, n, tm, LANES), lambda b, i: (b, 0, i, 0))
    tab = pl.BlockSpec((tm, LANES), lambda b, i: (i, 0))
    sds = jax.ShapeDtypeStruct
    col = lambda w: pl.BlockSpec((1, w, tm), lambda b, i: (b, 0, i))
    if prompt:
        q_spec, q_shape = col(MIX_WIDTH), sds((nb, MIX_WIDTH, t), BF16)
        g_spec, g_shape = col(LANES), sds((nb, LANES, t), F32)
    else:
        q_spec, q_shape = head(N_HEADS), sds((nb, N_HEADS, t, LANES), F32)
        g_spec, g_shape = row(LANES), sds((nb, t, LANES), F32)
    out_specs = [row(MQG_WIDTH), q_spec, row(KV2_WIDTH), row(KV2_WIDTH), row(KV2_WIDTH), g_spec, row(MIX_WIDTH)]
    out_shape = [sds((nb, t, MQG_WIDTH), F32), q_shape,
                 sds((nb, t, KV2_WIDTH), F32), sds((nb, t, KV2_WIDTH), F32), sds((nb, t, KV2_WIDTH), F32),
                 g_shape, sds((nb, t, MIX_WIDTH), F32)]
    if prompt:
        out_specs += [head(KV_HEADS), head(KV_HEADS), col(KV_WIDTH), col(KV_WIDTH)]
        out_shape += [sds((nb, KV_HEADS, t, LANES), BF16), sds((nb, KV_HEADS, t, LANES), BF16),
                      sds((nb, KV_WIDTH, t), BF16), sds((nb, KV_WIDTH, t), BF16)]
    return pl.pallas_call(
        functools.partial(_proj_nsa_kernel, tm=tm, prompt=prompt),
        grid=(nb, t // tm),
        in_specs=[pl.BlockSpec((1, tm, D_MODEL), lambda b, i: (b, i, 0)),
                  pl.BlockSpec((1, D_MODEL), lambda b, i: (0, 0)),
                  pl.BlockSpec((D_MODEL, _C_END), lambda b, i: (0, 0)),
                  tab, tab, tab],
        out_specs=out_specs,
        out_shape=out_shape,
        compiler_params=_cparams("arbitrary", "arbitrary"),
        name="proj_nsa",
    )(x3d, g.reshape(1, D_MODEL), w_b, *tabs)


def _rope_tables(pos):
    half = ROT_DIM // 2
    inv = ROPE_THETA ** (-jnp.arange(half, dtype=F32) * 2.0 / ROT_DIM)
    ang = pos.astype(F32)[:, None] * inv[None, :]
    cos, sin = jnp.cos(ang), jnp.sin(ang)
    n = pos.shape[0]
    one = jnp.ones((n, HEAD_DIM - ROT_DIM), F32)
    zero = jnp.zeros((n, HEAD_DIM - ROT_DIM), F32)
    zh = jnp.zeros((n, half), F32)
    c = jnp.concatenate([cos, cos, one], axis=1)
    sa = jnp.concatenate([-sin, zh, zero], axis=1)
    sb = jnp.concatenate([zh, sin, zero], axis=1)
    return tuple(jnp.tile(a, (1, LANES // HEAD_DIM)) for a in (c, sa, sb))


def _emit_cmp_blocks(out, kca_ref, v_ref, transpose_v):
    n = out.shape[0]
    lo = _lo_half(n)
    for c in range(KV_HEADS // 2):
        kc = out[:, c * LANES:(c + 1) * LANES]
        kca_ref[0, 2 * c] = jnp.where(lo, kc, 0.0).astype(BF16)
        kca_ref[0, 2 * c + 1] = jnp.where(lo, pltpu.roll(kc, HEAD_DIM, axis=1), 0.0).astype(BF16)
        if transpose_v:
            v_ref[0, c * LANES:(c + 1) * LANES, :] = out[:, KV_WIDTH + c * LANES:KV_WIDTH + (c + 1) * LANES].T.astype(BF16)
    if not transpose_v:
        v_ref[0] = out[:, KV_WIDTH:].astype(BF16)


def _cmp_prompt_kernel(kvc_ref, pw0_ref, pw1_ref, phi_ref, kca_ref, vcb_ref, p0_sc, p1_sc, *, rows):
    t = kvc_ref.shape[1]
    cpr = rows // CMP_STRIDE
    pw0 = pw0_ref[...][None]
    pw1 = pw1_ref[...][None]
    for i in range(t // rows):
        x = kvc_ref[0, i * rows:(i + 1) * rows, :].reshape(cpr, CMP_STRIDE, 2 * KV_WIDTH)
        p0_sc[i * cpr:(i + 1) * cpr, :] = jnp.sum(x * pw0, axis=1)
        p1_sc[i * cpr:(i + 1) * cpr, :] = jnp.sum(x * pw1, axis=1)
    nck = t // CMP_STRIDE
    blk = p0_sc[...] + pltpu.roll(p1_sc[...], nck - 1, axis=0)
    _emit_cmp_blocks(_dot(blk.astype(BF16), phi_ref[...]), kca_ref, vcb_ref, transpose_v=True)


def _cmp_prompt(kvc, pw0, pw1, phi_bd):
    b, t, _ = kvc.shape
    nck = t // CMP_STRIDE
    full = lambda s: pl.BlockSpec(s, lambda i: (0,) * len(s))
    return pl.pallas_call(
        functools.partial(_cmp_prompt_kernel, rows=min(t, 32 * CMP_STRIDE)),
        grid=(b,),
        in_specs=[pl.BlockSpec((1, t, 2 * KV_WIDTH), lambda i: (i, 0, 0)),
                  full((CMP_STRIDE, 2 * KV_WIDTH)), full((CMP_STRIDE, 2 * KV_WIDTH)),
                  full((2 * KV_WIDTH, 2 * KV_WIDTH))],
        out_specs=[pl.BlockSpec((1, KV_HEADS, nck, LANES), lambda i: (i, 0, 0, 0)),
                   pl.BlockSpec((1, KV_WIDTH, nck), lambda i: (i, 0, 0))],
        out_shape=[jax.ShapeDtypeStruct((b, KV_HEADS, nck, LANES), BF16),
                   jax.ShapeDtypeStruct((b, KV_WIDTH, nck), BF16)],
        scratch_shapes=[pltpu.VMEM((nck, 2 * KV_WIDTH), F32), pltpu.VMEM((nck, 2 * KV_WIDTH), F32)],
        compiler_params=_cparams("arbitrary"),
        name="cmp_prompt",
    )(kvc, pw0, pw1, phi_bd)


def _cmp_sample_kernel(pt_ref, w0_ref, w1_ref, seg_ref, phi_ref, *refs, pg):
    pages = refs[:pg]
    kca_ref, vcb_ref, p0_sc, p1_sc = refs[pg:]
    step = pl.program_id(1)
    cpp = PAGE_SIZE // CMP_STRIDE
    for k in range(pg):
        xt = pages[k][0]
        r0 = pl.multiple_of((step * pg + k) * cpp, cpp)
        for w_ref, p_sc in ((w0_ref, p0_sc), (w1_ref, p1_sc)):
            y = (xt * w_ref[...]).astype(BF16)
            p_sc[pl.ds(r0, cpp), :] = _dot_nt(seg_ref[...], y)[0:cpp]

    @pl.when(step == pl.num_programs(1) - 1)
    def _():
        nck = p0_sc.shape[0]
        blk = p0_sc[...] + pltpu.roll(p1_sc[...], nck - 1, axis=0)
        _emit_cmp_blocks(_dot(blk.astype(BF16), phi_ref[...]), kca_ref, vcb_ref, transpose_v=False)


def _page_specs(pg):
    return [pl.BlockSpec((1, 2 * KV_WIDTH, PAGE_SIZE), functools.partial(
        lambda b, s, pt, k: (pt[b, s * pg + k], 0, 0), k=k)) for k in range(pg)]


def _cmp_sample(cache_t, page_table, pw0, pw1, phi_bd, pg):
    bd, n_pages = page_table.shape
    nck = n_pages * PAGE_SIZE // CMP_STRIDE
    cpp = PAGE_SIZE // CMP_STRIDE
    w0, w1 = (jnp.tile(pw.T, (1, cpp)) for pw in (pw0, pw1))
    seg = np.zeros((max(cpp, 16), PAGE_SIZE), np.float32)
    seg[np.arange(PAGE_SIZE) // CMP_STRIDE, np.arange(PAGE_SIZE)] = 1.0
    full = lambda s: pl.BlockSpec(s, lambda b, i, pt: (0,) * len(s))
    grid_spec = pltpu.PrefetchScalarGridSpec(
        num_scalar_prefetch=1,
        grid=(bd, n_pages // pg),
        in_specs=[full(w0.shape), full(w1.shape), full(seg.shape),
                  full((2 * KV_WIDTH, 2 * KV_WIDTH))] + _page_specs(pg),
        out_specs=[pl.BlockSpec((1, KV_HEADS, nck, LANES), lambda b, i, pt: (b, 0, 0, 0)),
                   pl.BlockSpec((1, nck, KV_WIDTH), lambda b, i, pt: (b, 0, 0))],
        scratch_shapes=[pltpu.VMEM((nck, 2 * KV_WIDTH), F32), pltpu.VMEM((nck, 2 * KV_WIDTH), F32)],
    )
    return pl.pallas_call(
        functools.partial(_cmp_sample_kernel, pg=pg),
        grid_spec=grid_spec,
        out_shape=[jax.ShapeDtypeStruct((bd, KV_HEADS, nck, LANES), BF16),
                   jax.ShapeDtypeStruct((bd, nck, KV_WIDTH), BF16)],
        compiler_params=_cparams("arbitrary", "arbitrary"),
        name="cmp_sample",
    )(page_table, w0, w1, jnp.asarray(seg, BF16), phi_bd, *([cache_t] * pg))


def _imp_matrix(nck, n_sel_blocks):
    cps = SEL_BLOCK // CMP_STRIDE
    c = np.arange(nck)[:, None]
    j = np.arange(n_sel_blocks)[None, :]
    m = ((c >= cps * j) & (c <= cps * j + cps - 1)).astype(np.float32)
    m += ((c + 1 >= cps * j) & (c + 1 <= cps * j + cps - 1)).astype(np.float32)
    return m


def _softmax_masked(s, mask, axis=-1):
    s = jnp.where(mask, s, -1e30)
    p = jnp.exp(s - jnp.max(s, axis=axis, keepdims=True)) * mask.astype(F32)
    return p / jnp.maximum(jnp.sum(p, axis=axis, keepdims=True), 1e-30)


def _online_update_t(s, vt_ones, m_sc, acc_sc):
    m_old = m_sc[...]
    m_new = jnp.maximum(m_old, jnp.max(s, axis=0, keepdims=True))
    p = jnp.exp((s - m_new).astype(BF16))
    acc_sc[...] = jnp.exp(m_old - m_new) * acc_sc[...] + _dot(vt_ones, p)
    m_sc[...] = m_new


def _reset_online(m_sc, l_sc, acc_sc):
    m_sc[...] = jnp.full(m_sc.shape, M_INIT, F32)
    l_sc[...] = jnp.zeros(l_sc.shape, F32)
    acc_sc[...] = jnp.zeros(acc_sc.shape, F32)


def _attn_prompt_kernel(q_ref, kca_ref, vcb_ref, ksa_ref, vsb_ref, kwa_ref, vwb_ref, gates_ref, zgs_ref,
                        m2t_ref, o_ref, m_sc, acc_sc, score_sc, sa_sc, sb_sc, oc_sc, pen_sc, *, tq, n_sel):
    qi = pl.program_id(1)
    s0 = qi * tq
    cols = GROUP * tq
    nsb = m2t_ref.shape[0]
    qpos = s0 + lax.broadcasted_iota(jnp.int32, (1, tq), 1)
    qpos3 = jnp.concatenate([qpos] * GROUP, axis=1)
    gates = gates_ref[0]
    zero_h = jnp.zeros((HEAD_DIM, tq), BF16)
    mix_t = []
    for g in range(KV_HEADS):
        vrows = slice(g * HEAD_DIM, (g + 1) * HEAD_DIM)
        qh = [q_ref[0, (GROUP * g + r) * HEAD_DIM:(GROUP * g + r + 1) * HEAD_DIM, :] for r in range(GROUP)]
        qc = jnp.concatenate([jnp.concatenate([x, zero_h], axis=0) for x in qh], axis=1)

        def compressed_and_select(rows, slots):
            sc = _dot(kca_ref[0, g, 0:rows, :], qc)
            blk_end = lax.broadcasted_iota(jnp.int32, (rows, 1), 0) * CMP_STRIDE + (CMP_BLOCK - 1)
            pc = _softmax_masked(sc, blk_end <= qpos3, axis=0)
            oc_sc[...] = _dot(vcb_ref[0, vrows, 0:rows], pc.astype(BF16))
            imp = pc[:, 0:tq] + pc[:, tq:2 * tq] + pc[:, 2 * tq:3 * tq]

            imp_t = None
            for part in _split3(imp):
                d = _dot(m2t_ref[0:slots, 0:rows], part)
                imp_t = d if imp_t is None else imp_t + d
            j = lax.broadcasted_iota(jnp.int32, (slots, tq), 0)
            cur = jnp.right_shift(s0 + lax.broadcasted_iota(jnp.int32, (slots, tq), 1), SEL_SHIFT)
            valid = j <= cur
            forced = (j == 0) | (j == cur) | (j == cur - 1)
            score = jnp.where(valid, imp_t + jnp.where(forced, FORCE_BONUS, 0.0), -jnp.inf)
            score_sc[0:slots, :] = score
            bpt = tq // SEL_BLOCK

            def rank_body(it, cnt):
                for u in range(bpt):
                    i = it * bpt + u
                    row = score_sc[pl.ds(i, 1), :]
                    tie = jnp.where(j > i, 1.0, 0.0)
                    cnt = cnt + jnp.where(row > score, 1.0, jnp.where(row == score, tie, 0.0))
                return cnt

            cnt = lax.fori_loop(0, jnp.minimum(qi + 1, slots // bpt), rank_body, jnp.zeros((slots, tq), F32))
            pen = jnp.where((cnt < float(n_sel)) & valid, 0.0, NEG)
            if slots < nsb:
                pen = jnp.concatenate([pen, jnp.full((nsb - slots, tq), NEG, F32)], axis=0)
            pen_sc[...] = pen.astype(BF16)

        ncp = kca_ref.shape[2]
        n_var = VISIBILITY_VARIANTS if ncp % (VISIBILITY_VARIANTS * 4 * 16) == 0 and nsb * 4 == ncp else 1
        chunk = ncp // n_var
        visible = (s0 + tq - CMP_BLOCK) // CMP_STRIDE + 1
        variant = jnp.clip((visible + chunk - 1) // chunk, 1, n_var)
        for k in range(1, n_var + 1):
            pl.when(variant == k)(functools.partial(compressed_and_select, chunk * k, nsb * k // n_var))
        oc = oc_sc[...]
        pen = pen_sc[...]
        qa = jnp.concatenate([jnp.concatenate([x, pen], axis=0) for x in qh], axis=1)

        def tile_start(kt):
            return pl.multiple_of(kt * tq, tq)

        def key_pos(kt):
            return tile_start(kt) + lax.broadcasted_iota(jnp.int32, (tq, 1), 0)

        def run_branch(scores, values, first, last_mask):
            m_sc[...] = jnp.full(m_sc.shape, M_INIT, F32)
            acc_sc[...] = jnp.zeros(acc_sc.shape, F32)
            ones = jnp.ones((SUM_ROWS, tq), BF16)

            def update(s_ref, kt, mask=None):
                s = s_ref[...] if mask is None else mask(s_ref[...])
                _online_update_t(s, jnp.concatenate([values(kt), ones], axis=0), m_sc, acc_sc)

            def pair(kt):
                sb_sc[...] = scores(kt + 1)
                update(sa_sc, kt)
                sa_sc[...] = scores(kt + 2)
                update(sb_sc, kt + 1)

            def trip(i, c):
                pair(first + 4 * i)
                pair(first + 4 * i + 2)
                return c

            ahead = qi - first
            n_pairs = ahead // 2
            sa_sc[...] = scores(first)
            lax.fori_loop(0, n_pairs // 2, trip, 0)

            @pl.when(n_pairs % 2 == 1)
            def _():
                pair(first + 2 * (n_pairs - 1))

            @pl.when(ahead % 2 == 1)
            def _():
                sb_sc[...] = scores(qi)
                update(sa_sc, qi - 1)
                update(sb_sc, qi, last_mask)

            @pl.when(ahead % 2 == 0)
            def _():
                update(sa_sc, qi, last_mask)

            return acc_sc[0:HEAD_DIM, :] / acc_sc[HEAD_DIM:HEAD_DIM + 1, :]

        o_s = run_branch(
            lambda kt: _dot(ksa_ref[0, g, pl.ds(tile_start(kt), tq), :], qa),
            lambda kt: vsb_ref[0, vrows, pl.ds(tile_start(kt), tq)],
            0, lambda s: jnp.where(key_pos(qi) <= qpos3, s, NEG))

        def win_scores(kt, lower, causal):
            s = _dot(kwa_ref[0, g, pl.ds(tile_start(kt), tq), :], qc)
            if lower:
                s = jnp.where(key_pos(kt) > qpos3 - WINDOW, s, NEG)
            if causal:
                s = jnp.where(key_pos(kt) <= qpos3, s, NEG)
            return s

        def win_update(s_ref, kt):
            vt_ones = jnp.concatenate([vwb_ref[0, vrows, pl.ds(tile_start(kt), tq)], jnp.ones((SUM_ROWS, tq), BF16)], axis=0)
            _online_update_t(s_ref[...], vt_ones, m_sc, acc_sc)

        m_sc[...] = jnp.full(m_sc.shape, M_INIT, F32)
        acc_sc[...] = jnp.zeros(acc_sc.shape, F32)

        @pl.when(qi >= 2)
        def _():
            sa_sc[...] = win_scores(qi - 2, True, False)
            sb_sc[...] = win_scores(qi - 1, False, False)
            win_update(sa_sc, qi - 2)
            sa_sc[...] = win_scores(qi, False, True)
            win_update(sb_sc, qi - 1)
            win_update(sa_sc, qi)

        @pl.when(qi == 1)
        def _():
            sa_sc[...] = win_scores(0, False, False)
            sb_sc[...] = win_scores(1, False, True)
            win_update(sa_sc, 0)
            win_update(sb_sc, 1)

        @pl.when(qi == 0)
        def _():
            sa_sc[...] = win_scores(0, False, True)
            win_update(sa_sc, 0)

        o_w = acc_sc[0:HEAD_DIM, :] / acc_sc[HEAD_DIM:HEAD_DIM + 1, :]

        for r in range(GROUP):
            hd = GROUP * g + r
            cs = slice(r * tq, (r + 1) * tq)
            mix_t.append(gates[hd:hd + 1, :] * oc[:, cs] + gates[N_HEADS + hd:N_HEADS + hd + 1, :] * o_s[:, cs]
                         + gates[2 * N_HEADS + hd:2 * N_HEADS + hd + 1, :] * o_w[:, cs])
    for c in range(N_HEADS // 2):
        lanes = slice(c * LANES, (c + 1) * LANES)
        mix = jnp.concatenate([mix_t[2 * c], mix_t[2 * c + 1]], axis=0).T
        o_ref[0, :, lanes] = (zgs_ref[0, :, lanes] * mix).astype(o_ref.dtype)


def _attn_prompt(q, kca, vcb, ksa, vsb, kwa, vwb, gates, zgs, tq):
    b, _, t = q.shape
    nck = kca.shape[2]
    nsb = -(-t // SEL_BLOCK)
    assert WINDOW == 2 * tq or t <= tq
    slots = LANES - HEAD_DIM
    assert nsb <= slots
    m2t = np.zeros((slots, nck), np.float32)
    m2t[:nsb] = _imp_matrix(nck, nsb).T
    per_b4 = lambda n, r: pl.BlockSpec((1, n, r, LANES), lambda i, j: (i, 0, 0, 0))
    per_b3 = lambda r, w: pl.BlockSpec((1, r, w), lambda i, j: (i, 0, 0))
    cols = GROUP * tq
    return pl.pallas_call(
        functools.partial(_attn_prompt_kernel, tq=tq, n_sel=min(N_SEL, nsb)),
        grid=(b, t // tq),
        in_specs=[pl.BlockSpec((1, MIX_WIDTH, tq), lambda i, j: (i, 0, j)),
                  per_b4(KV_HEADS, nck), per_b3(KV_WIDTH, nck),
                  per_b4(KV_HEADS, t), per_b3(KV_WIDTH, t),
                  per_b4(KV_HEADS, t), per_b3(KV_WIDTH, t),
                  pl.BlockSpec((1, LANES, tq), lambda i, j: (i, 0, j)),
                  pl.BlockSpec((1, tq, MIX_WIDTH), lambda i, j: (i, j, 0)),
                  pl.BlockSpec(m2t.shape, lambda i, j: (0, 0))],
        out_specs=pl.BlockSpec((1, tq, MIX_WIDTH), lambda i, j: (i, j, 0)),
        out_shape=jax.ShapeDtypeStruct((b, t, MIX_WIDTH), BF16),
        scratch_shapes=[pltpu.VMEM((1, cols), F32), pltpu.VMEM((HEAD_DIM + SUM_ROWS, cols), F32),
                        pltpu.VMEM((slots, tq), F32), pltpu.VMEM((tq, cols), F32), pltpu.VMEM((tq, cols), F32),
                        pltpu.VMEM((HEAD_DIM, cols), F32), pltpu.VMEM((slots, tq), BF16)],
        compiler_params=_cparams("arbitrary", "arbitrary"),
        name="attn_prompt",
    )(q, kca, vcb, ksa, vsb, kwa, vwb, gates, zgs, jnp.asarray(m2t, BF16))


def _online_update(s, vt, m_sc, l_sc, acc_sc):
    m_old = m_sc[...]
    m_new = jnp.maximum(m_old, jnp.max(s, axis=-1, keepdims=True))
    alpha = jnp.exp(m_old - m_new)
    p = jnp.exp(s - m_new)
    l_sc[...] = alpha * l_sc[...] + jnp.sum(p, axis=-1, keepdims=True)
    acc_sc[...] = alpha * acc_sc[...] + _dot_nt(p.astype(BF16), vt)
    m_sc[...] = m_new


def _attn_sample_kernel(pt_ref, q_ref, kca_ref, vcb_ref, ksn_ref, kwn_ref, win_ref, gates_ref, zgs_ref, mmat_ref,
                        *refs, pg, past_len):
    pages = refs[:pg]
    o_ref, qbd_sc, pen_sc, m_sc, l_sc, acc_sc, oc_sc, ow_sc = refs[pg:]
    step = pl.program_id(1)
    t = q_ref.shape[2]
    rows = N_HEADS * t
    rpad = qbd_sc.shape[0]
    npb = mmat_ref.shape[1]
    tok = jnp.bitwise_and(lax.broadcasted_iota(jnp.int32, (rpad, 1), 0), t - 1)
    qpos = past_len + tok

    @pl.when(step == 0)
    def _():
        zero_t = jnp.zeros((t, LANES), F32)
        qrows = []
        for hd in range(N_HEADS):
            g = hd // GROUP
            qh = q_ref[0, hd]
            if g % 2 == 1:
                qh = pltpu.roll(qh, HEAD_DIM, axis=1)
            qrows.append(jnp.concatenate([qh, zero_t] if g // 2 == 0 else [zero_t, qh], axis=1))
        qbd = jnp.concatenate(qrows + [jnp.zeros((rpad - rows, 2 * LANES), F32)], axis=0).astype(BF16)

        imps = []
        rg = GROUP * t
        qpos_g = qpos[0:rg]
        for g in range(KV_HEADS):
            qg = jnp.concatenate([q_ref[0, GROUP * g + r] for r in range(GROUP)], axis=0).astype(BF16)
            sc = _dot_nt(qg, kca_ref[0, g])
            ncp = sc.shape[1]
            blk_end = lax.broadcasted_iota(jnp.int32, (1, ncp), 1) * CMP_STRIDE + (CMP_BLOCK - 1)
            pc = _softmax_masked(sc, blk_end <= qpos_g)
            oc_sc[g * rg:(g + 1) * rg, :] = _dot(pc.astype(BF16), vcb_ref[0])
            imps.append(pc[0:t] + pc[t:2 * t] + pc[2 * t:3 * t])
        oc_sc[rows:, :] = jnp.zeros((rpad - rows, 2 * LANES), F32)
        imp = jnp.concatenate(imps, axis=0)
        imp_s = None
        for part in _split3(imp):
            d = _dot(part, mmat_ref[...])
            imp_s = d if imp_s is None else imp_s + d
        ngt = KV_HEADS * t
        width = npb + LANES
        base = jnp.concatenate([imp_s, jnp.zeros((ngt, LANES), F32)], axis=1)
        j = lax.broadcasted_iota(jnp.int32, (ngt, width), 1)
        cur = jnp.right_shift(past_len + jnp.bitwise_and(lax.broadcasted_iota(jnp.int32, (ngt, width), 0), t - 1), SEL_SHIFT)
        valid = j <= cur
        forced = (j == 0) | (j == cur) | (j == cur - 1)
        score = jnp.where(valid, base + jnp.where(forced, FORCE_BONUS, 0.0), -jnp.inf)
        picked = jnp.zeros((ngt, width), F32)
        jf = j.astype(F32)
        for _ in range(N_SEL):
            mx = jnp.max(score, axis=-1, keepdims=True)
            idx = jnp.min(jnp.where(score == mx, jf, float(width)), axis=-1, keepdims=True)
            hit = jf == idx
            picked = jnp.where(hit, 1.0, picked)
            score = jnp.where(hit, -jnp.inf, score)
        pen_gt = jnp.where((picked > 0.5) & valid, 0.0, NEG)
        pen = jnp.concatenate([pen_gt[(hd // GROUP) * t:(hd // GROUP + 1) * t] for hd in range(N_HEADS)]
                              + [jnp.zeros((rpad - rows, width), F32)], axis=0)
        qbd_sc[...] = qbd
        for c in range(width // LANES):
            pen_sc[c * LANES:(c + 1) * LANES, :] = pen[:, c * LANES:(c + 1) * LANES].T.astype(BF16)
        pen_new = pen[:, npb:npb + 1]

        def padded(ref, cols):
            return jnp.concatenate([ref[0, :, cols], jnp.zeros((LANES - t, KV_WIDTH), F32)], axis=0).astype(BF16)

        kcols, vcols = slice(0, KV_WIDTH), slice(KV_WIDTH, 2 * KV_WIDTH)
        inew = lax.broadcasted_iota(jnp.int32, (1, LANES), 1)
        new_ok = (inew < t) & (past_len + inew <= qpos)

        s = jnp.where(new_ok, _dot_nt(qbd, padded(ksn_ref, kcols)) + pen_new, NEG)
        m0 = jnp.maximum(jnp.max(s, axis=-1, keepdims=True), M_INIT)
        p = jnp.exp(s - m0)
        m_sc[...] = m0
        l_sc[...] = jnp.sum(p, axis=-1, keepdims=True)
        acc_sc[...] = _dot(p.astype(BF16), padded(ksn_ref, vcols))

        wk = win_ref.shape[2]
        kw_pos = past_len - wk + lax.broadcasted_iota(jnp.int32, (1, wk), 1)
        mask1 = (kw_pos <= qpos) & (kw_pos > qpos - WINDOW) & (kw_pos >= 0)
        mask2 = new_ok & (past_len + inew > qpos - WINDOW)
        s1 = jnp.where(mask1, _dot(qbd, win_ref[0, kcols, :].astype(BF16)), -1e30)
        s2 = jnp.where(mask2, _dot_nt(qbd, padded(kwn_ref, kcols)), -1e30)
        mw = jnp.maximum(jnp.max(s1, axis=-1, keepdims=True), jnp.max(s2, axis=-1, keepdims=True))
        p1 = jnp.exp(s1 - mw) * mask1.astype(F32)
        p2 = jnp.exp(s2 - mw) * mask2.astype(F32)
        lw = jnp.sum(p1, axis=-1, keepdims=True) + jnp.sum(p2, axis=-1, keepdims=True)
        ow = _dot_nt(p1.astype(BF16), win_ref[0, vcols, :].astype(BF16)) + _dot(p2.astype(BF16), padded(kwn_ref, vcols))
        ow_sc[...] = ow / jnp.maximum(lw, 1e-30)

    bpu = PAGES_PER_UPDATE * (PAGE_SIZE // SEL_BLOCK)
    keys = PAGES_PER_UPDATE * PAGE_SIZE
    onehot = jnp.where(jnp.right_shift(lax.broadcasted_iota(jnp.int32, (bpu, keys), 1), SEL_SHIFT)
                       == lax.broadcasted_iota(jnp.int32, (bpu, keys), 0), 1.0, 0.0).astype(BF16)
    for u in range(pg // PAGES_PER_UPDATE):
        blocks = [pages[u * PAGES_PER_UPDATE + i][0] for i in range(PAGES_PER_UPDATE)]
        kt = jnp.concatenate([x[0:KV_WIDTH, :] for x in blocks], axis=1).astype(BF16)
        vt = jnp.concatenate([x[KV_WIDTH:, :] for x in blocks], axis=1).astype(BF16)
        b0 = pl.multiple_of((step * (pg // PAGES_PER_UPDATE) + u) * bpu, bpu)
        bias = lax.dot_general(pen_sc[pl.ds(b0, bpu), :], onehot, (((0,), (0,)), ((), ())), preferred_element_type=F32)
        _online_update(_dot(qbd_sc[...], kt) + bias, vt, m_sc, l_sc, acc_sc)

    @pl.when(step == pl.num_programs(1) - 1)
    def _():
        o_s = acc_sc[...] / l_sc[...]
        lo = _lo_half(t)
        gates = gates_ref[0]
        placed = []
        for hd in range(N_HEADS):
            g = hd // GROUP
            rs = slice(hd * t, (hd + 1) * t)
            cols = slice((g // 2) * LANES, (g // 2 + 1) * LANES)
            mix = (gates[:, hd:hd + 1] * oc_sc[rs, cols] + gates[:, N_HEADS + hd:N_HEADS + hd + 1] * o_s[rs, cols]
                   + gates[:, 2 * N_HEADS + hd:2 * N_HEADS + hd + 1] * ow_sc[rs, cols])
            placed.append(mix if g % 2 == hd % 2 else pltpu.roll(mix, HEAD_DIM, axis=1))
        for c in range(N_HEADS // 2):
            cols = slice(c * LANES, (c + 1) * LANES)
            o_ref[0, :, cols] = zgs_ref[0, :, cols] * jnp.where(lo, placed[2 * c], placed[2 * c + 1])


def _attn_sample(q, kca, vcb, kvs_new, kvw_new, win_state, gates, zgs, cache_sel, page_table, t, pg):
    bd, n_pages = page_table.shape
    past_len = n_pages * PAGE_SIZE
    nck = kca.shape[2]
    npb = past_len // SEL_BLOCK
    mmat = jnp.asarray(_imp_matrix(nck, npb), BF16)
    wk = win_state.shape[2]
    rpad = -(-N_HEADS * t // LANES) * LANES
    tok = lambda w: pl.BlockSpec((1, t, w), lambda b, s, pt: (0, b, 0))
    grid_spec = pltpu.PrefetchScalarGridSpec(
        num_scalar_prefetch=1,
        grid=(bd, n_pages // pg),
        in_specs=[pl.BlockSpec((1, N_HEADS, t, LANES), lambda b, s, pt: (0, 0, b, 0)),
                  pl.BlockSpec((1, KV_HEADS, nck, LANES), lambda b, s, pt: (b, 0, 0, 0)),
                  pl.BlockSpec((1, nck, KV_WIDTH), lambda b, s, pt: (b, 0, 0)),
                  tok(2 * KV_WIDTH), tok(2 * KV_WIDTH),
                  pl.BlockSpec((1, 2 * KV_WIDTH, wk), lambda b, s, pt: (b, 0, 0)),
                  tok(LANES), tok(MIX_WIDTH),
                  pl.BlockSpec(mmat.shape, lambda b, s, pt: (0, 0))] + _page_specs(pg),
        out_specs=tok(MIX_WIDTH),
        scratch_shapes=[pltpu.VMEM((rpad, 2 * LANES), BF16), pltpu.VMEM((npb + LANES, rpad), BF16),
                        pltpu.VMEM((rpad, 1), F32), pltpu.VMEM((rpad, 1), F32),
                        pltpu.VMEM((rpad, 2 * LANES), F32), pltpu.VMEM((rpad, 2 * LANES), F32),
                        pltpu.VMEM((rpad, 2 * LANES), F32)],
    )
    return pl.pallas_call(
        functools.partial(_attn_sample_kernel, pg=pg, past_len=past_len),
        grid_spec=grid_spec,
        out_shape=jax.ShapeDtypeStruct((1, bd * t, MIX_WIDTH), F32),
        compiler_params=_cparams("arbitrary", "arbitrary"),
        name="attn_sample",
    )(page_table, q, kca, vcb, kvs_new, kvw_new, win_state, gates, zgs, mmat, *([cache_sel] * pg))


def _reorder_nsa_weight(w):
    d = w.shape[0]
    n_gate = 3 * N_HEADS
    zg0 = _C_ZG + n_gate
    return jnp.concatenate([w[:, :_C_ZG], w[:, zg0:zg0 + MIX_WIDTH], w[:, _C_ZG:zg0],
                            jnp.zeros((d, LANES - n_gate), w.dtype)], axis=1).astype(BF16)


def _cmp_weights(cmp_pos_w, cmp_phi):
    ratio = CMP_BLOCK // CMP_STRIDE
    pw = cmp_pos_w.reshape(2, ratio, CMP_STRIDE, HEAD_DIM)
    tiles = [jnp.concatenate([jnp.tile(pw[s, m], (1, KV_HEADS)) for s in range(2)], axis=1) for m in range(ratio)]
    eye = jnp.eye(KV_HEADS, dtype=cmp_phi.dtype)
    z = jnp.zeros((KV_WIDTH, KV_WIDTH), cmp_phi.dtype)
    phi_bd = jnp.concatenate([jnp.concatenate([jnp.kron(eye, cmp_phi[0]), z], axis=1),
                              jnp.concatenate([z, jnp.kron(eye, cmp_phi[1])], axis=1)], axis=0).astype(BF16)
    return tiles[0], tiles[1], phi_bd


def _feature_major(cache):
    n, rows = cache.shape[:2]
    return jnp.transpose(cache, (0, 2, 3, 4, 1)).reshape(n, -1, rows)


def kernel(x_prompt, x_sample, mem_prompt, cache_mem_kv, cache_cmp_kv, cache_sel_kv, page_table, state_conv, state_win_kv, norm_g, final_norm_g, mem_norm_g, w_mem_kv, w_in_conv, conv_w, w_in_nsa, cmp_pos_w, cmp_phi, w_out):
    b, t, d = x_prompt.shape
    bd, td, _ = x_sample.shape
    n_mem = mem_prompt.shape[1]
    n_pages = page_table.shape[1]
    past_len = n_pages * PAGE_SIZE
    assert w_in_conv.shape[0] == 1 and w_in_nsa.shape[0] == 1 and w_out.shape[0] == 2
    assert CMP_BLOCK == 2 * CMP_STRIDE and td & (td - 1) == 0 and td < CMP_STRIDE

    w_conv_b = w_in_conv[0].astype(BF16)
    w_nsa_b = _reorder_nsa_weight(w_in_nsa[0])
    w_out_b = w_out.astype(BF16)
    w_mem_b = w_mem_kv.astype(BF16)
    pw0, pw1, phi_bd = _cmp_weights(cmp_pos_w[0], cmp_phi[0])
    kv6 = lambda a, n, r: a.reshape(1, n, r, 2, KV_HEADS, HEAD_DIM)

    mem_kv_p = _memory_kv(mem_prompt, mem_norm_g, w_mem_b)
    xp = x_prompt.reshape(b * t, d)
    tm = min(512, t)
    tmo_p = min(1024, t)
    mqg, mix, conv_p = _proj_conv(xp, norm_g[0], w_conv_b, conv_w[0], seg=t, tm=tm, mix_dtype=BF16)
    x1 = _layer_out(xp, mqg, mem_kv_p[0], mix, w_out_b[0], tm=tmo_p, seg=t)
    tabs = _rope_tables(jnp.arange(t, dtype=jnp.int32))
    tq = min(256, t)
    (mqg, q, kvc_p, kvs_p, kvw_p, gates, zgs, ksa, kwa, vsb, vwb) = _proj_nsa(
        x1.reshape(b, t, d), norm_g[1], w_nsa_b, tabs, tm=tm, prompt=True)
    kca, vcb = _cmp_prompt(kvc_p, pw0, pw1, phi_bd)
    mix = _attn_prompt(q, kca, vcb, ksa, vsb, kwa, vwb, gates, zgs, tq=tq)
    y_prompt = _layer_out(x1, mqg.reshape(b * t, MQG_WIDTH), mem_kv_p[1], mix.reshape(b * t, MIX_WIDTH), w_out_b[1],
                          tm=tmo_p, seg=t, final_g=final_norm_g).reshape(b, t, d)
    w_keep_p = min(WINDOW, t)

    ms = bd * td
    xs = x_sample.reshape(ms, d)
    st = state_conv[0]
    zrow = jnp.zeros((bd, td - 1, MIX_WIDTH), F32)
    s1 = jnp.concatenate([st[:, 1:2], zrow], axis=1).reshape(ms, MIX_WIDTH)
    s2 = jnp.concatenate([st, zrow[:, 1:]], axis=1).reshape(ms, MIX_WIDTH)
    mqg, mix, u_s = _proj_conv(xs, norm_g[0], w_conv_b, conv_w[0], seg=td, tm=ms, state=(s1, s2))
    conv_s = u_s.reshape(bd, td, MIX_WIDTH)[:, td - 2:]
    tmo = 8 * td
    mem_s = _feature_major(cache_mem_kv.reshape((-1,) + cache_mem_kv.shape[2:]))
    x1s = _layer_out(xs, mqg, mem_s, mix, w_out_b[0], tm=tmo, seg=td, feature_major=True)
    tabs_s = _rope_tables(jnp.tile(past_len + jnp.arange(td, dtype=jnp.int32), bd))
    (mqg, q_s, kvc_s, kvs_s, kvw_s, gates_s, zgs_s) = _proj_nsa(
        x1s.reshape(1, ms, d), norm_g[1], w_nsa_b, tabs_s, tm=ms, prompt=False)
    pg = min(32, n_pages)
    kca_s, vcb_s = _cmp_sample(_feature_major(cache_cmp_kv[0]), page_table, pw0, pw1, phi_bd, pg)
    mix_s = _attn_sample(q_s, kca_s, vcb_s, kvs_s, kvw_s, _feature_major(state_win_kv[0]), gates_s, zgs_s,
                         _feature_major(cache_sel_kv[0]), page_table, td, pg)
    y_sample = _layer_out(x1s, mqg.reshape(ms, MQG_WIDTH), mem_s, mix_s.reshape(ms, MIX_WIDTH), w_out_b[1], tm=tmo, seg=td,
                          final_g=final_norm_g, feature_major=True, memkv_offset=bd).reshape(bd, td, d)
    w_keep = state_win_kv.shape[2]
    win_s = jnp.concatenate([state_win_kv[0], kvw_s.reshape(bd, td, 2, KV_HEADS, HEAD_DIM)], axis=1)[:, -w_keep:]

    return (y_prompt, y_sample, conv_p[None],
            kv6(kvc_p, b, t), kv6(kvs_p, b, t), kv6(kvw_p[:, t - w_keep_p:], b, w_keep_p),
            mem_kv_p.reshape(2, b, n_mem, 2, MEM_WIDTH // HEAD_DIM, HEAD_DIM),
            conv_s[None], kv6(kvc_s, bd, td), kv6(kvs_s, bd, td), win_s[None])
```

```python
import functools

import numpy as np
import jax
import jax.numpy as jnp
from jax import lax
from jax.experimental import pallas as pl
from jax.experimental.pallas import tpu as pltpu

F32 = jnp.float32
BF16 = jnp.bfloat16

D_MODEL = 1024
HEAD_DIM = 64
MIX_WIDTH = 768
MEM_WIDTH = 256
N_HEADS = 12
KV_HEADS = 4
GROUP = 3
KV_WIDTH = 256
CMP_BLOCK = 32
CMP_STRIDE = 16
SEL_BLOCK = 64
N_SEL = 16
WINDOW = 512
ROT_DIM = 16
ROPE_THETA = 500000.0
NORM_EPS = 1e-6
FORCE_BONUS = 1e4
PAGE_SIZE = 128
SCALE = HEAD_DIM ** -0.5
MQG_WIDTH = 2 * MEM_WIDTH
KV2_WIDTH = 2 * KV_WIDTH
SEL_SHIFT = SEL_BLOCK.bit_length() - 1
_V_B, _V_C, _V_H, _V_Z, _V_END = (MQG_WIDTH + i * MIX_WIDTH for i in range(5))
NEG = -(2.0 ** 100)
M_INIT = -1e30
LANES = 128
VISIBILITY_VARIANTS = 4
SUM_ROWS = 16
PAGES_PER_UPDATE = 8
VMEM_LIMIT = 56 * 2 ** 20


def _cparams(*sem):
    return pltpu.CompilerParams(dimension_semantics=sem, vmem_limit_bytes=VMEM_LIMIT)


def _dot(a, b):
    return jnp.dot(a, b, preferred_element_type=F32)


def _dot_nt(a, b):
    return lax.dot_general(a, b, (((1,), (1,)), ((), ())), preferred_element_type=F32)


def _rms(x, g):
    return x * lax.rsqrt(jnp.mean(x * x, axis=-1, keepdims=True) + NORM_EPS) * g


def _silu(x):
    return x * jax.nn.sigmoid(x)


def _split3(a):
    hi = a.astype(BF16)
    r1 = a - hi.astype(F32)
    mid = r1.astype(BF16)
    lo = (r1 - mid.astype(F32)).astype(BF16)
    return hi, mid, lo


def _lo_half(rows):
    return lax.broadcasted_iota(jnp.int32, (rows, LANES), 1) < HEAD_DIM


def _memkv_kernel(mem_ref, g_ref, w_ref, o_ref):
    h = _rms(mem_ref[0], g_ref[0]).astype(BF16)
    o_ref[0, 0] = _dot(h, w_ref[0])


def _memory_kv(mem, mem_norm_g, w_mem_kv_b):
    depth = w_mem_kv_b.shape[0]
    b, n_mem, d = mem.shape
    return pl.pallas_call(
        _memkv_kernel,
        grid=(depth, b),
        in_specs=[pl.BlockSpec((1, n_mem, d), lambda i, j: (j, 0, 0)),
                  pl.BlockSpec((1, 1, d), lambda i, j: (i, 0, 0)),
                  pl.BlockSpec((1, d, 2 * MEM_WIDTH), lambda i, j: (i, 0, 0))],
        out_specs=pl.BlockSpec((1, 1, n_mem, 2 * MEM_WIDTH), lambda i, j: (i, j, 0, 0)),
        out_shape=jax.ShapeDtypeStruct((depth, b, n_mem, 2 * MEM_WIDTH), F32),
        compiler_params=_cparams("arbitrary", "arbitrary"),
        name="memory_kv",
    )(mem, mem_norm_g.reshape(depth, 1, d), w_mem_kv_b)


def _proj_conv_kernel(*refs, tm, seg, has_state):
    if has_state:
        x_ref, g_ref, w_ref, cw_ref, s1_ref, s2_ref, mqg_ref, mix_ref, st_ref, ubuf = refs
    else:
        x_ref, g_ref, w_ref, cw_ref, mqg_ref, mix_ref, st_ref, ubuf = refs
    i = pl.program_id(0)
    h = _rms(x_ref[...], g_ref[...]).astype(BF16)
    mqg_ref[...] = _dot(h, w_ref[:, 0:_V_B])
    bg = _dot(h, w_ref[:, _V_B:_V_C])
    cg = _dot(h, w_ref[:, _V_C:_V_H])
    hin = _dot(h, w_ref[:, _V_H:_V_Z])
    zg = _dot(h, w_ref[:, _V_Z:_V_END])
    u = cg * hin

    @pl.when(i == 0)
    def _():
        ubuf[0:8, :] = jnp.zeros((8, MIX_WIDTH), F32)

    @pl.when(i > 0)
    def _():
        ubuf[0:8, :] = ubuf[tm:tm + 8, :]

    ubuf[8:8 + tm, :] = u
    u1 = ubuf[7:7 + tm, :]
    u2 = ubuf[6:6 + tm, :]
    rowpos = lax.rem(i * tm + lax.broadcasted_iota(jnp.int32, (tm, 1), 0), seg)
    if has_state:
        u1 = jnp.where(rowpos >= 1, u1, s1_ref[...])
        u2 = jnp.where(rowpos >= 2, u2, s2_ref[...])
        st_ref[...] = u
    else:
        u1 = jnp.where(rowpos >= 1, u1, 0.0)
        u2 = jnp.where(rowpos >= 2, u2, 0.0)
        st_ref[0] = ubuf[8 + tm - 2:8 + tm, :]
    y = cw_ref[0:1, :] * u2 + cw_ref[1:2, :] * u1 + cw_ref[2:3, :] * u
    mix_ref[...] = (_silu(zg) * (bg * y)).astype(mix_ref.dtype)


def _proj_conv(x2d, g, w_b, cw, seg, tm, state=None, mix_dtype=F32):
    m = x2d.shape[0]
    n_in = w_b.shape[1]
    has_state = state is not None
    in_specs = [pl.BlockSpec((tm, D_MODEL), lambda i: (i, 0)),
                pl.BlockSpec((1, D_MODEL), lambda i: (0, 0)),
                pl.BlockSpec((D_MODEL, n_in), lambda i: (0, 0)),
                pl.BlockSpec((3, MIX_WIDTH), lambda i: (0, 0))]
    args = [x2d, g.reshape(1, D_MODEL), w_b, cw]
    if has_state:
        in_specs += [pl.BlockSpec((tm, MIX_WIDTH), lambda i: (i, 0))] * 2
        args += list(state)
        st_shape = jax.ShapeDtypeStruct((m, MIX_WIDTH), F32)
        st_spec = pl.BlockSpec((tm, MIX_WIDTH), lambda i: (i, 0))
    else:
        st_shape = jax.ShapeDtypeStruct((m // seg, 2, MIX_WIDTH), F32)
        st_spec = pl.BlockSpec((1, 2, MIX_WIDTH), lambda i: ((i * tm) // seg, 0, 0))
    return pl.pallas_call(
        functools.partial(_proj_conv_kernel, tm=tm, seg=seg, has_state=has_state),
        grid=(m // tm,),
        in_specs=in_specs,
        out_specs=[pl.BlockSpec((tm, MQG_WIDTH), lambda i: (i, 0)),
                   pl.BlockSpec((tm, MIX_WIDTH), lambda i: (i, 0)),
                   st_spec],
        out_shape=[jax.ShapeDtypeStruct((m, MQG_WIDTH), F32),
                   jax.ShapeDtypeStruct((m, MIX_WIDTH), mix_dtype),
                   st_shape],
        scratch_shapes=[pltpu.VMEM((tm + 8, MIX_WIDTH), F32)],
        compiler_params=_cparams("arbitrary"),
        name="proj_conv",
    )(*args)


def _out_kernel(*refs, nseg, seg, final, feature_major):
    if final:
        x_ref, mqg_ref, memkv_ref, mix_ref, w_ref, g_ref, o_ref, mo_sc = refs
    else:
        x_ref, mqg_ref, memkv_ref, mix_ref, w_ref, o_ref, mo_sc = refs
    lo = _lo_half(seg)

    def seg_body(s, r0):
        for pair in range(2):
            cols = slice(pair * LANES, (pair + 1) * LANES)
            qp = mqg_ref[pl.ds(r0, seg), cols]
            vcols = slice(MEM_WIDTH + pair * LANES, MEM_WIDTH + (pair + 1) * LANES)
            if feature_major:
                kp, vp = memkv_ref[s, cols, :].astype(BF16), memkv_ref[s, vcols, :].astype(BF16)
            else:
                kp, vp = memkv_ref[s, :, cols].astype(BF16), memkv_ref[s, :, vcols].astype(BF16)
            outs = []
            for half in range(2):
                keep = lo if half == 0 else jnp.logical_not(lo)
                qm = jnp.where(keep, qp, 0.0).astype(BF16)
                sc = (_dot(qm, kp) if feature_major else _dot_nt(qm, kp)) * SCALE
                e = jnp.exp(sc - jnp.max(sc, axis=-1, keepdims=True))
                p = e / jnp.sum(e, axis=-1, keepdims=True)
                outs.append(_dot_nt(p.astype(BF16), vp) if feature_major else _dot(p.astype(BF16), vp))
            mo_sc[pl.ds(r0, seg), cols] = jnp.where(lo, outs[0], outs[1])

    if nseg == 1:
        seg_body(0, 0)
    else:
        def body(s, c):
            seg_body(s, pl.multiple_of(s * seg, seg))
            return c
        lax.fori_loop(0, nseg, body, 0)

    mg = mqg_ref[:, MEM_WIDTH:2 * MEM_WIDTH]
    a = (_silu(mg) * mo_sc[...]).astype(BF16)
    xn = x_ref[...] + (_dot(a, w_ref[0:MEM_WIDTH, :]) + _dot(mix_ref[...].astype(BF16), w_ref[MEM_WIDTH:, :]))
    if final:
        o_ref[...] = _rms(xn, g_ref[...])
    else:
        o_ref[...] = xn


def _layer_out(x2d, mqg, memkv, mix, w_out_b, tm, seg, final_g=None, feature_major=False, memkv_offset=0):
    m = x2d.shape[0]
    nseg = max(tm // seg, 1)
    seg_in = min(seg, tm)
    final = final_g is not None
    in_specs = [pl.BlockSpec((tm, D_MODEL), lambda i: (i, 0)),
                pl.BlockSpec((tm, MQG_WIDTH), lambda i: (i, 0)),
                pl.BlockSpec((nseg,) + memkv.shape[1:], lambda i: (memkv_offset // nseg + (i * tm) // (seg * nseg), 0, 0)),
                pl.BlockSpec((tm, MIX_WIDTH), lambda i: (i, 0)),
                pl.BlockSpec((D_MODEL, D_MODEL), lambda i: (0, 0))]
    args = [x2d, mqg, memkv, mix, w_out_b]
    if final:
        in_specs.append(pl.BlockSpec((1, D_MODEL), lambda i: (0, 0)))
        args.append(final_g.reshape(1, D_MODEL))
    return pl.pallas_call(
        functools.partial(_out_kernel, nseg=nseg, seg=seg_in, final=final, feature_major=feature_major),
        grid=(m // tm,),
        in_specs=in_specs,
        out_specs=pl.BlockSpec((tm, D_MODEL), lambda i: (i, 0)),
        out_shape=jax.ShapeDtypeStruct((m, D_MODEL), F32),
        scratch_shapes=[pltpu.VMEM((tm, MEM_WIDTH), F32)],
        compiler_params=_cparams("arbitrary"),
        name="layer_out",
    )(*args)


_C_Q = MQG_WIDTH
_C_KC = _C_Q + MIX_WIDTH
_C_KS, _C_KW, _C_ZG = (_C_KC + i * KV2_WIDTH for i in (1, 2, 3))
_C_GL = _C_ZG + MIX_WIDTH
_C_END = _C_GL + LANES


def _proj_nsa_kernel(*refs, tm, prompt):
    (x_ref, g_ref, w_ref, c_ref, sa_ref, sb_ref,
     mqg_ref, q_ref, kvc_ref, kvs_ref, kvw_ref, gates_ref, zgs_ref) = refs[:13]
    ti = pl.program_id(1)
    h = _rms(x_ref[0], g_ref[...]).astype(BF16)
    cos, sa, sb = c_ref[...], sa_ref[...], sb_ref[...]
    lo = _lo_half(tm)

    def rope(chunk):
        return chunk * cos + pltpu.roll(chunk, LANES - ROT_DIM // 2, axis=1) * sa + pltpu.roll(chunk, ROT_DIM // 2, axis=1) * sb

    mqg_ref[0] = _dot(h, w_ref[:, 0:_C_Q])
    qf = _dot(h, w_ref[:, _C_Q:_C_KC])
    for c in range(N_HEADS // 2):
        qc = rope(qf[:, c * LANES:(c + 1) * LANES]) * SCALE
        if prompt:
            q_ref[0, c * LANES:(c + 1) * LANES, :] = qc.T.astype(q_ref.dtype)
        else:
            q_ref[0, 2 * c] = jnp.where(lo, qc, 0.0).astype(q_ref.dtype)
            q_ref[0, 2 * c + 1] = jnp.where(lo, pltpu.roll(qc, HEAD_DIM, axis=1), 0.0).astype(q_ref.dtype)

    if prompt:
        ksa_ref, kwa_ref, vsb_ref, vwb_ref = refs[13:17]
        pos = ti * tm + lax.broadcasted_iota(jnp.int32, (tm, LANES), 0)
        lane = lax.broadcasted_iota(jnp.int32, (tm, LANES), 1)
        onehot = jnp.where(jnp.right_shift(pos, SEL_SHIFT) == lane - HEAD_DIM, 1.0, 0.0)

    for name, off in (("c", _C_KC), ("s", _C_KS), ("w", _C_KW)):
        kv_ref = {"c": kvc_ref, "s": kvs_ref, "w": kvw_ref}[name]
        kk = _dot(h, w_ref[:, off:off + KV_WIDTH])
        vv = _dot(h, w_ref[:, off + KV_WIDTH:off + 2 * KV_WIDTH])
        kv_ref[0, :, KV_WIDTH:2 * KV_WIDTH] = vv
        for c in range(KV_HEADS // 2):
            kr = rope(kk[:, c * LANES:(c + 1) * LANES])
            kv_ref[0, :, c * LANES:(c + 1) * LANES] = kr
            if prompt and name != "c":
                pad = onehot if name == "s" else 0.0
                aug_ref = ksa_ref if name == "s" else kwa_ref
                aug_ref[0, 2 * c] = jnp.where(lo, kr, pad).astype(BF16)
                aug_ref[0, 2 * c + 1] = jnp.where(lo, pltpu.roll(kr, HEAD_DIM, axis=1), pad).astype(BF16)
        if prompt and name != "c":
            vt_ref = vsb_ref if name == "s" else vwb_ref
            for c in range(KV_WIDTH // LANES):
                vt_ref[0, c * LANES:(c + 1) * LANES, :] = vv[:, c * LANES:(c + 1) * LANES].T.astype(BF16)

    gates = jax.nn.sigmoid(_dot(h, w_ref[:, _C_GL:_C_END]))
    gates_ref[0] = gates.T if prompt else gates
    zgs_ref[0] = _silu(_dot(h, w_ref[:, _C_ZG:_C_GL]))


def _proj_nsa(x3d, g, w_b, tabs, tm, prompt):
    nb, t, _ = x3d.shape
    row = lambda w: pl.BlockSpec((1, tm, w), lambda b, i: (b, i, 0))
    head = lambda n: pl.BlockSpec((1, n, tm, LANES), lambda b, i: (b, 0, i, 0))
    tab = pl.BlockSpec((tm, LANES), lambda b, i: (i, 0))
    sds = jax.ShapeDtypeStruct
    col = lambda w: pl.BlockSpec((1, w, tm), lambda b, i: (b, 0, i))
    if prompt:
        q_spec, q_shape = col(MIX_WIDTH), sds((nb, MIX_WIDTH, t), BF16)
        g_spec, g_shape = col(LANES), sds((nb, LANES, t), F32)
    else:
        q_spec, q_shape = head(N_HEADS), sds((nb, N_HEADS, t, LANES), F32)
        g_spec, g_shape = row(LANES), sds((nb, t, LANES), F32)
    out_specs = [row(MQG_WIDTH), q_spec, row(KV2_WIDTH), row(KV2_WIDTH), row(KV2_WIDTH), g_spec, row(MIX_WIDTH)]
    out_shape = [sds((nb, t, MQG_WIDTH), F32), q_shape,
                 sds((nb, t, KV2_WIDTH), F32), sds((nb, t, KV2_WIDTH), F32), sds((nb, t, KV2_WIDTH), F32),
                 g_shape, sds((nb, t, MIX_WIDTH), F32)]
    if prompt:
        out_specs += [head(KV_HEADS), head(KV_HEADS), col(KV_WIDTH), col(KV_WIDTH)]
        out_shape += [sds((nb, KV_HEADS, t, LANES), BF16), sds((nb, KV_HEADS, t, LANES), BF16),
                      sds((nb, KV_WIDTH, t), BF16), sds((nb, KV_WIDTH, t), BF16)]
    return pl.pallas_call(
        functools.partial(_proj_nsa_kernel, tm=tm, prompt=prompt),
        grid=(nb, t // tm),
        in_specs=[pl.BlockSpec((1, tm, D_MODEL), lambda b, i: (b, i, 0)),
                  pl.BlockSpec((1, D_MODEL), lambda b, i: (0, 0)),
                  pl.BlockSpec((D_MODEL, _C_END), lambda b, i: (0, 0)),
                  tab, tab, tab],
        out_specs=out_specs,
        out_shape=out_shape,
        compiler_params=_cparams("arbitrary", "arbitrary"),
        name="proj_nsa",
    )(x3d, g.reshape(1, D_MODEL), w_b, *tabs)


def _rope_tables(pos):
    half = ROT_DIM // 2
    inv = ROPE_THETA ** (-jnp.arange(half, dtype=F32) * 2.0 / ROT_DIM)
    ang = pos.astype(F32)[:, None] * inv[None, :]
    cos, sin = jnp.cos(ang), jnp.sin(ang)
    n = pos.shape[0]
    one = jnp.ones((n, HEAD_DIM - ROT_DIM), F32)
    zero = jnp.zeros((n, HEAD_DIM - ROT_DIM), F32)
    zh = jnp.zeros((n, half), F32)
    c = jnp.concatenate([cos, cos, one], axis=1)
    sa = jnp.concatenate([-sin, zh, zero], axis=1)
    sb = jnp.concatenate([zh, sin, zero], axis=1)
    return tuple(jnp.tile(a, (1, LANES // HEAD_DIM)) for a in (c, sa, sb))


def _emit_cmp_blocks(out, kca_ref, v_ref, transpose_v):
    n = out.shape[0]
    lo = _lo_half(n)
    for c in range(KV_HEADS // 2):
        kc = out[:, c * LANES:(c + 1) * LANES]
        kca_ref[0, 2 * c] = jnp.where(lo, kc, 0.0).astype(BF16)
        kca_ref[0, 2 * c + 1] = jnp.where(lo, pltpu.roll(kc, HEAD_DIM, axis=1), 0.0).astype(BF16)
        if transpose_v:
            v_ref[0, c * LANES:(c + 1) * LANES, :] = out[:, KV_WIDTH + c * LANES:KV_WIDTH + (c + 1) * LANES].T.astype(BF16)
    if not transpose_v:
        v_ref[0] = out[:, KV_WIDTH:].astype(BF16)


def _cmp_prompt_kernel(kvc_ref, pw0_ref, pw1_ref, phi_ref, kca_ref, vcb_ref, p0_sc, p1_sc, *, rows):
    t = kvc_ref.shape[1]
    cpr = rows // CMP_STRIDE
    pw0 = pw0_ref[...][None]
    pw1 = pw1_ref[...][None]
    for i in range(t // rows):
        x = kvc_ref[0, i * rows:(i + 1) * rows, :].reshape(cpr, CMP_STRIDE, 2 * KV_WIDTH)
        p0_sc[i * cpr:(i + 1) * cpr, :] = jnp.sum(x * pw0, axis=1)
        p1_sc[i * cpr:(i + 1) * cpr, :] = jnp.sum(x * pw1, axis=1)
    nck = t // CMP_STRIDE
    blk = p0_sc[...] + pltpu.roll(p1_sc[...], nck - 1, axis=0)
    _emit_cmp_blocks(_dot(blk.astype(BF16), phi_ref[...]), kca_ref, vcb_ref, transpose_v=True)


def _cmp_prompt(kvc, pw0, pw1, phi_bd):
    b, t, _ = kvc.shape
    nck = t // CMP_STRIDE
    full = lambda s: pl.BlockSpec(s, lambda i: (0,) * len(s))
    return pl.pallas_call(
        functools.partial(_cmp_prompt_kernel, rows=min(t, 32 * CMP_STRIDE)),
        grid=(b,),
        in_specs=[pl.BlockSpec((1, t, 2 * KV_WIDTH), lambda i: (i, 0, 0)),
                  full((CMP_STRIDE, 2 * KV_WIDTH)), full((CMP_STRIDE, 2 * KV_WIDTH)),
                  full((2 * KV_WIDTH, 2 * KV_WIDTH))],
        out_specs=[pl.BlockSpec((1, KV_HEADS, nck, LANES), lambda i: (i, 0, 0, 0)),
                   pl.BlockSpec((1, KV_WIDTH, nck), lambda i: (i, 0, 0))],
        out_shape=[jax.ShapeDtypeStruct((b, KV_HEADS, nck, LANES), BF16),
                   jax.ShapeDtypeStruct((b, KV_WIDTH, nck), BF16)],
        scratch_shapes=[pltpu.VMEM((nck, 2 * KV_WIDTH), F32), pltpu.VMEM((nck, 2 * KV_WIDTH), F32)],
        compiler_params=_cparams("arbitrary"),
        name="cmp_prompt",
    )(kvc, pw0, pw1, phi_bd)


def _cmp_sample_kernel(pt_ref, w0_ref, w1_ref, seg_ref, phi_ref, *refs, pg):
    pages = refs[:pg]
    kca_ref, vcb_ref, p0_sc, p1_sc = refs[pg:]
    step = pl.program_id(1)
    cpp = PAGE_SIZE // CMP_STRIDE
    for k in range(pg):
        xt = pages[k][0]
        r0 = pl.multiple_of((step * pg + k) * cpp, cpp)
        for w_ref, p_sc in ((w0_ref, p0_sc), (w1_ref, p1_sc)):
            y = (xt * w_ref[...]).astype(BF16)
            p_sc[pl.ds(r0, cpp), :] = _dot_nt(seg_ref[...], y)[0:cpp]

    @pl.when(step == pl.num_programs(1) - 1)
    def _():
        nck = p0_sc.shape[0]
        blk = p0_sc[...] + pltpu.roll(p1_sc[...], nck - 1, axis=0)
        _emit_cmp_blocks(_dot(blk.astype(BF16), phi_ref[...]), kca_ref, vcb_ref, transpose_v=False)


def _page_specs(pg):
    return [pl.BlockSpec((1, 2 * KV_WIDTH, PAGE_SIZE), functools.partial(
        lambda b, s, pt, k: (pt[b, s * pg + k], 0, 0), k=k)) for k in range(pg)]


def _cmp_sample(cache_t, page_table, pw0, pw1, phi_bd, pg):
    bd, n_pages = page_table.shape
    nck = n_pages * PAGE_SIZE // CMP_STRIDE
    cpp = PAGE_SIZE // CMP_STRIDE
    w0, w1 = (jnp.tile(pw.T, (1, cpp)) for pw in (pw0, pw1))
    seg = np.zeros((max(cpp, 16), PAGE_SIZE), np.float32)
    seg[np.arange(PAGE_SIZE) // CMP_STRIDE, np.arange(PAGE_SIZE)] = 1.0
    full = lambda s: pl.BlockSpec(s, lambda b, i, pt: (0,) * len(s))
    grid_spec = pltpu.PrefetchScalarGridSpec(
        num_scalar_prefetch=1,
        grid=(bd, n_pages // pg),
        in_specs=[full(w0.shape), full(w1.shape), full(seg.shape),
                  full((2 * KV_WIDTH, 2 * KV_WIDTH))] + _page_specs(pg),
        out_specs=[pl.BlockSpec((1, KV_HEADS, nck, LANES), lambda b, i, pt: (b, 0, 0, 0)),
                   pl.BlockSpec((1, nck, KV_WIDTH), lambda b, i, pt: (b, 0, 0))],
        scratch_shapes=[pltpu.VMEM((nck, 2 * KV_WIDTH), F32), pltpu.VMEM((nck, 2 * KV_WIDTH), F32)],
    )
    return pl.pallas_call(
        functools.partial(_cmp_sample_kernel, pg=pg),
        grid_spec=grid_spec,
        out_shape=[jax.ShapeDtypeStruct((bd, KV_HEADS, nck, LANES), BF16),
                   jax.ShapeDtypeStruct((bd, nck, KV_WIDTH), BF16)],
        compiler_params=_cparams("arbitrary", "arbitrary"),
        name="cmp_sample",
    )(page_table, w0, w1, jnp.asarray(seg, BF16), phi_bd, *([cache_t] * pg))


def _imp_matrix(nck, n_sel_blocks):
    cps = SEL_BLOCK // CMP_STRIDE
    c = np.arange(nck)[:, None]
    j = np.arange(n_sel_blocks)[None, :]
    m = ((c >= cps * j) & (c <= cps * j + cps - 1)).astype(np.float32)
    m += ((c + 1 >= cps * j) & (c + 1 <= cps * j + cps - 1)).astype(np.float32)
    return m


def _softmax_masked(s, mask, axis=-1):
    s = jnp.where(mask, s, -1e30)
    p = jnp.exp(s - jnp.max(s, axis=axis, keepdims=True)) * mask.astype(F32)
    return p / jnp.maximum(jnp.sum(p, axis=axis, keepdims=True), 1e-30)


def _online_update_t(s, vt_ones, m_sc, acc_sc):
    m_old = m_sc[...]
    m_new = jnp.maximum(m_old, jnp.max(s, axis=0, keepdims=True))
    p = jnp.exp((s - m_new).astype(BF16))
    acc_sc[...] = jnp.exp(m_old - m_new) * acc_sc[...] + _dot(vt_ones, p)
    m_sc[...] = m_new


def _reset_online(m_sc, l_sc, acc_sc):
    m_sc[...] = jnp.full(m_sc.shape, M_INIT, F32)
    l_sc[...] = jnp.zeros(l_sc.shape, F32)
    acc_sc[...] = jnp.zeros(acc_sc.shape, F32)


def _attn_prompt_kernel(q_ref, kca_ref, vcb_ref, ksa_ref, vsb_ref, kwa_ref, vwb_ref, gates_ref, zgs_ref,
                        m2t_ref, o_ref, m_sc, acc_sc, score_sc, sa_sc, sb_sc, oc_sc, pen_sc, *, tq, n_sel):
    qi = pl.program_id(1)
    s0 = qi * tq
    cols = GROUP * tq
    nsb = m2t_ref.shape[0]
    qpos = s0 + lax.broadcasted_iota(jnp.int32, (1, tq), 1)
    qpos3 = jnp.concatenate([qpos] * GROUP, axis=1)
    gates = gates_ref[0]
    zero_h = jnp.zeros((HEAD_DIM, tq), BF16)
    mix_t = []
    for g in range(KV_HEADS):
        vrows = slice(g * HEAD_DIM, (g + 1) * HEAD_DIM)
        qh = [q_ref[0, (GROUP * g + r) * HEAD_DIM:(GROUP * g + r + 1) * HEAD_DIM, :] for r in range(GROUP)]
        qc = jnp.concatenate([jnp.concatenate([x, zero_h], axis=0) for x in qh], axis=1)

        def compressed_and_select(rows, slots):
            sc = _dot(kca_ref[0, g, 0:rows, :], qc)
            blk_end = lax.broadcasted_iota(jnp.int32, (rows, 1), 0) * CMP_STRIDE + (CMP_BLOCK - 1)
            pc = _softmax_masked(sc, blk_end <= qpos3, axis=0)
            oc_sc[...] = _dot(vcb_ref[0, vrows, 0:rows], pc.astype(BF16))
            imp = pc[:, 0:tq] + pc[:, tq:2 * tq] + pc[:, 2 * tq:3 * tq]

            imp_t = None
            for part in _split3(imp):
                d = _dot(m2t_ref[0:slots, 0:rows], part)
                imp_t = d if imp_t is None else imp_t + d
            j = lax.broadcasted_iota(jnp.int32, (slots, tq), 0)
            cur = jnp.right_shift(s0 + lax.broadcasted_iota(jnp.int32, (slots, tq), 1), SEL_SHIFT)
            valid = j <= cur
            forced = (j == 0) | (j == cur) | (j == cur - 1)
            score = jnp.where(valid, imp_t + jnp.where(forced, FORCE_BONUS, 0.0), -jnp.inf)
            score_sc[0:slots, :] = score
            bpt = tq // SEL_BLOCK

            def rank_body(it, cnt):
                for u in range(bpt):
                    i = it * bpt + u
                    row = score_sc[pl.ds(i, 1), :]
                    tie = jnp.where(j > i, 1.0, 0.0)
                    cnt = cnt + jnp.where(row > score, 1.0, jnp.where(row == score, tie, 0.0))
                return cnt

            cnt = lax.fori_loop(0, jnp.minimum(qi + 1, slots // bpt), rank_body, jnp.zeros((slots, tq), F32))
            pen = jnp.where((cnt < float(n_sel)) & valid, 0.0, NEG)
            if slots < nsb:
                pen = jnp.concatenate([pen, jnp.full((nsb - slots, tq), NEG, F32)], axis=0)
            pen_sc[...] = pen.astype(BF16)

        ncp = kca_ref.shape[2]
        n_var = VISIBILITY_VARIANTS if ncp % (VISIBILITY_VARIANTS * 4 * 16) == 0 and nsb * 4 == ncp else 1
        chunk = ncp // n_var
        visible = (s0 + tq - CMP_BLOCK) // CMP_STRIDE + 1
        variant = jnp.clip((visible + chunk - 1) // chunk, 1, n_var)
        for k in range(1, n_var + 1):
            pl.when(variant == k)(functools.partial(compressed_and_select, chunk * k, nsb * k // n_var))
        oc = oc_sc[...]
        pen = pen_sc[...]
        qa = jnp.concatenate([jnp.concatenate([x, pen], axis=0) for x in qh], axis=1)

        def tile_start(kt):
            return pl.multiple_of(kt * tq, tq)

        def key_pos(kt):
            return tile_start(kt) + lax.broadcasted_iota(jnp.int32, (tq, 1), 0)

        def run_branch(scores, values, first, last_mask):
            m_sc[...] = jnp.full(m_sc.shape, M_INIT, F32)
            acc_sc[...] = jnp.zeros(acc_sc.shape, F32)
            ones = jnp.ones((SUM_ROWS, tq), BF16)

            def update(s_ref, kt, mask=None):
                s = s_ref[...] if mask is None else mask(s_ref[...])
                _online_update_t(s, jnp.concatenate([values(kt), ones], axis=0), m_sc, acc_sc)

            def pair(kt):
                sb_sc[...] = scores(kt + 1)
                update(sa_sc, kt)
                sa_sc[...] = scores(kt + 2)
                update(sb_sc, kt + 1)

            def trip(i, c):
                pair(first + 4 * i)
                pair(first + 4 * i + 2)
                return c

            ahead = qi - first
            n_pairs = ahead // 2
            sa_sc[...] = scores(first)
            lax.fori_loop(0, n_pairs // 2, trip, 0)

            @pl.when(n_pairs % 2 == 1)
            def _():
                pair(first + 2 * (n_pairs - 1))

            @pl.when(ahead % 2 == 1)
            def _():
                sb_sc[...] = scores(qi)
                update(sa_sc, qi - 1)
                update(sb_sc, qi, last_mask)

            @pl.when(ahead % 2 == 0)
            def _():
                update(sa_sc, qi, last_mask)

            return acc_sc[0:HEAD_DIM, :] / acc_sc[HEAD_DIM:HEAD_DIM + 1, :]

        o_s = run_branch(
            lambda kt: _dot(ksa_ref[0, g, pl.ds(tile_start(kt), tq), :], qa),
            lambda kt: vsb_ref[0, vrows, pl.ds(tile_start(kt), tq)],
            0, lambda s: jnp.where(key_pos(qi) <= qpos3, s, NEG))

        def win_scores(kt, lower, causal):
            s = _dot(kwa_ref[0, g, pl.ds(tile_start(kt), tq), :], qc)
            if lower:
                s = jnp.where(key_pos(kt) > qpos3 - WINDOW, s, NEG)
            if causal:
                s = jnp.where(key_pos(kt) <= qpos3, s, NEG)
            return s

        def win_update(s_ref, kt):
            vt_ones = jnp.concatenate([vwb_ref[0, vrows, pl.ds(tile_start(kt), tq)], jnp.ones((SUM_ROWS, tq), BF16)], axis=0)
            _online_update_t(s_ref[...], vt_ones, m_sc, acc_sc)

        m_sc[...] = jnp.full(m_sc.shape, M_INIT, F32)
        acc_sc[...] = jnp.zeros(acc_sc.shape, F32)

        @pl.when(qi >= 2)
        def _():
            sa_sc[...] = win_scores(qi - 2, True, False)
            sb_sc[...] = win_scores(qi - 1, False, False)
            win_update(sa_sc, qi - 2)
            sa_sc[...] = win_scores(qi, False, True)
            win_update(sb_sc, qi - 1)
            win_update(sa_sc, qi)

        @pl.when(qi == 1)
        def _():
            sa_sc[...] = win_scores(0, False, False)
            sb_sc[...] = win_scores(1, False, True)
            win_update(sa_sc, 0)
            win_update(sb_sc, 1)

        @pl.when(qi == 0)
        def _():
            sa_sc[...] = win_scores(0, False, True)
            win_update(sa_sc, 0)

        o_w = acc_sc[0:HEAD_DIM, :] / acc_sc[HEAD_DIM:HEAD_DIM + 1, :]

        for r in range(GROUP):
            hd = GROUP * g + r
            cs = slice(r * tq, (r + 1) * tq)
            mix_t.append(gates[hd:hd + 1, :] * oc[:, cs] + gates[N_HEADS + hd:N_HEADS + hd + 1, :] * o_s[:, cs]
                         + gates[2 * N_HEADS + hd:2 * N_HEADS + hd + 1, :] * o_w[:, cs])
    for c in range(N_HEADS // 2):
        lanes = slice(c * LANES, (c + 1) * LANES)
        mix = jnp.concatenate([mix_t[2 * c], mix_t[2 * c + 1]], axis=0).T
        o_ref[0, :, lanes] = (zgs_ref[0, :, lanes] * mix).astype(o_ref.dtype)


def _attn_prompt(q, kca, vcb, ksa, vsb, kwa, vwb, gates, zgs, tq):
    b, _, t = q.shape
    nck = kca.shape[2]
    nsb = -(-t // SEL_BLOCK)
    assert WINDOW == 2 * tq or t <= tq
    slots = LANES - HEAD_DIM
    assert nsb <= slots
    m2t = np.zeros((slots, nck), np.float32)
    m2t[:nsb] = _imp_matrix(nck, nsb).T
    per_b4 = lambda n, r: pl.BlockSpec((1, n, r, LANES), lambda i, j: (i, 0, 0, 0))
    per_b3 = lambda r, w: pl.BlockSpec((1, r, w), lambda i, j: (i, 0, 0))
    cols = GROUP * tq
    return pl.pallas_call(
        functools.partial(_attn_prompt_kernel, tq=tq, n_sel=min(N_SEL, nsb)),
        grid=(b, t // tq),
        in_specs=[pl.BlockSpec((1, MIX_WIDTH, tq), lambda i, j: (i, 0, j)),
                  per_b4(KV_HEADS, nck), per_b3(KV_WIDTH, nck),
                  per_b4(KV_HEADS, t), per_b3(KV_WIDTH, t),
                  per_b4(KV_HEADS, t), per_b3(KV_WIDTH, t),
                  pl.BlockSpec((1, LANES, tq), lambda i, j: (i, 0, j)),
                  pl.BlockSpec((1, tq, MIX_WIDTH), lambda i, j: (i, j, 0)),
                  pl.BlockSpec(m2t.shape, lambda i, j: (0, 0))],
        out_specs=pl.BlockSpec((1, tq, MIX_WIDTH), lambda i, j: (i, j, 0)),
        out_shape=jax.ShapeDtypeStruct((b, t, MIX_WIDTH), BF16),
        scratch_shapes=[pltpu.VMEM((1, cols), F32), pltpu.VMEM((HEAD_DIM + SUM_ROWS, cols), F32),
                        pltpu.VMEM((slots, tq), F32), pltpu.VMEM((tq, cols), F32), pltpu.VMEM((tq, cols), F32),
                        pltpu.VMEM((HEAD_DIM, cols), F32), pltpu.VMEM((slots, tq), BF16)],
        compiler_params=_cparams("arbitrary", "arbitrary"),
        name="attn_prompt",
    )(q, kca, vcb, ksa, vsb, kwa, vwb, gates, zgs, jnp.asarray(m2t, BF16))


def _online_update(s, vt, m_sc, l_sc, acc_sc):
    m_old = m_sc[...]
    m_new = jnp.maximum(m_old, jnp.max(s, axis=-1, keepdims=True))
    alpha = jnp.exp(m_old - m_new)
    p = jnp.exp(s - m_new)
    l_sc[...] = alpha * l_sc[...] + jnp.sum(p, axis=-1, keepdims=True)
    acc_sc[...] = alpha * acc_sc[...] + _dot_nt(p.astype(BF16), vt)
    m_sc[...] = m_new


def _attn_sample_kernel(pt_ref, q_ref, kca_ref, vcb_ref, ksn_ref, kwn_ref, win_ref, gates_ref, zgs_ref, mmat_ref,
                        *refs, pg, past_len):
    pages = refs[:pg]
    o_ref, qbd_sc, pen_sc, m_sc, l_sc, acc_sc, oc_sc, ow_sc = refs[pg:]
    step = pl.program_id(1)
    t = q_ref.shape[2]
    rows = N_HEADS * t
    rpad = qbd_sc.shape[0]
    npb = mmat_ref.shape[1]
    tok = jnp.bitwise_and(lax.broadcasted_iota(jnp.int32, (rpad, 1), 0), t - 1)
    qpos = past_len + tok

    @pl.when(step == 0)
    def _():
        zero_t = jnp.zeros((t, LANES), F32)
        qrows = []
        for hd in range(N_HEADS):
            g = hd // GROUP
            qh = q_ref[0, hd]
            if g % 2 == 1:
                qh = pltpu.roll(qh, HEAD_DIM, axis=1)
            qrows.append(jnp.concatenate([qh, zero_t] if g // 2 == 0 else [zero_t, qh], axis=1))
        qbd = jnp.concatenate(qrows + [jnp.zeros((rpad - rows, 2 * LANES), F32)], axis=0).astype(BF16)

        imps = []
        rg = GROUP * t
        qpos_g = qpos[0:rg]
        for g in range(KV_HEADS):
            qg = jnp.concatenate([q_ref[0, GROUP * g + r] for r in range(GROUP)], axis=0).astype(BF16)
            sc = _dot_nt(qg, kca_ref[0, g])
            ncp = sc.shape[1]
            blk_end = lax.broadcasted_iota(jnp.int32, (1, ncp), 1) * CMP_STRIDE + (CMP_BLOCK - 1)
            pc = _softmax_masked(sc, blk_end <= qpos_g)
            oc_sc[g * rg:(g + 1) * rg, :] = _dot(pc.astype(BF16), vcb_ref[0])
            imps.append(pc[0:t] + pc[t:2 * t] + pc[2 * t:3 * t])
        oc_sc[rows:, :] = jnp.zeros((rpad - rows, 2 * LANES), F32)
        imp = jnp.concatenate(imps, axis=0)
        imp_s = None
        for part in _split3(imp):
            d = _dot(part, mmat_ref[...])
            imp_s = d if imp_s is None else imp_s + d
        ngt = KV_HEADS * t
        width = npb + LANES
        base = jnp.concatenate([imp_s, jnp.zeros((ngt, LANES), F32)], axis=1)
        j = lax.broadcasted_iota(jnp.int32, (ngt, width), 1)
        cur = jnp.right_shift(past_len + jnp.bitwise_and(lax.broadcasted_iota(jnp.int32, (ngt, width), 0), t - 1), SEL_SHIFT)
        valid = j <= cur
        forced = (j == 0) | (j == cur) | (j == cur - 1)
        score = jnp.where(valid, base + jnp.where(forced, FORCE_BONUS, 0.0), -jnp.inf)
        picked = jnp.zeros((ngt, width), F32)
        jf = j.astype(F32)
        for _ in range(N_SEL):
            mx = jnp.max(score, axis=-1, keepdims=True)
            idx = jnp.min(jnp.where(score == mx, jf, float(width)), axis=-1, keepdims=True)
            hit = jf == idx
            picked = jnp.where(hit, 1.0, picked)
            score = jnp.where(hit, -jnp.inf, score)
        pen_gt = jnp.where((picked > 0.5) & valid, 0.0, NEG)
        pen = jnp.concatenate([pen_gt[(hd // GROUP) * t:(hd // GROUP + 1) * t] for hd in range(N_HEADS)]
                              + [jnp.zeros((rpad - rows, width), F32)], axis=0)
        qbd_sc[...] = qbd
        for c in range(width // LANES):
            pen_sc[c * LANES:(c + 1) * LANES, :] = pen[:, c * LANES:(c + 1) * LANES].T.astype(BF16)
        pen_new = pen[:, npb:npb + 1]

        def padded(ref, cols):
            return jnp.concatenate([ref[0, :, cols], jnp.zeros((LANES - t, KV_WIDTH), F32)], axis=0).astype(BF16)

        kcols, vcols = slice(0, KV_WIDTH), slice(KV_WIDTH, 2 * KV_WIDTH)
        inew = lax.broadcasted_iota(jnp.int32, (1, LANES), 1)
        new_ok = (inew < t) & (past_len + inew <= qpos)

        s = jnp.where(new_ok, _dot_nt(qbd, padded(ksn_ref, kcols)) + pen_new, NEG)
        m0 = jnp.maximum(jnp.max(s, axis=-1, keepdims=True), M_INIT)
        p = jnp.exp(s - m0)
        m_sc[...] = m0
        l_sc[...] = jnp.sum(p, axis=-1, keepdims=True)
        acc_sc[...] = _dot(p.astype(BF16), padded(ksn_ref, vcols))

        wk = win_ref.shape[2]
        kw_pos = past_len - wk + lax.broadcasted_iota(jnp.int32, (1, wk), 1)
        mask1 = (kw_pos <= qpos) & (kw_pos > qpos - WINDOW) & (kw_pos >= 0)
        mask2 = new_ok & (past_len + inew > qpos - WINDOW)
        s1 = jnp.where(mask1, _dot(qbd, win_ref[0, kcols, :].astype(BF16)), -1e30)
        s2 = jnp.where(mask2, _dot_nt(qbd, padded(kwn_ref, kcols)), -1e30)
        mw = jnp.maximum(jnp.max(s1, axis=-1, keepdims=True), jnp.max(s2, axis=-1, keepdims=True))
        p1 = jnp.exp(s1 - mw) * mask1.astype(F32)
        p2 = jnp.exp(s2 - mw) * mask2.astype(F32)
        lw = jnp.sum(p1, axis=-1, keepdims=True) + jnp.sum(p2, axis=-1, keepdims=True)
        ow = _dot_nt(p1.astype(BF16), win_ref[0, vcols, :].astype(BF16)) + _dot(p2.astype(BF16), padded(kwn_ref, vcols))
        ow_sc[...] = ow / jnp.maximum(lw, 1e-30)

    bpu = PAGES_PER_UPDATE * (PAGE_SIZE // SEL_BLOCK)
    keys = PAGES_PER_UPDATE * PAGE_SIZE
    onehot = jnp.where(jnp.right_shift(lax.broadcasted_iota(jnp.int32, (bpu, keys), 1), SEL_SHIFT)
                       == lax.broadcasted_iota(jnp.int32, (bpu, keys), 0), 1.0, 0.0).astype(BF16)
    for u in range(pg // PAGES_PER_UPDATE):
        blocks = [pages[u * PAGES_PER_UPDATE + i][0] for i in range(PAGES_PER_UPDATE)]
        kt = jnp.concatenate([x[0:KV_WIDTH, :] for x in blocks], axis=1).astype(BF16)
        vt = jnp.concatenate([x[KV_WIDTH:, :] for x in blocks], axis=1).astype(BF16)
        b0 = pl.multiple_of((step * (pg // PAGES_PER_UPDATE) + u) * bpu, bpu)
        bias = lax.dot_general(pen_sc[pl.ds(b0, bpu), :], onehot, (((0,), (0,)), ((), ())), preferred_element_type=F32)
        _online_update(_dot(qbd_sc[...], kt) + bias, vt, m_sc, l_sc, acc_sc)

    @pl.when(step == pl.num_programs(1) - 1)
    def _():
        o_s = acc_sc[...] / l_sc[...]
        lo = _lo_half(t)
        gates = gates_ref[0]
        placed = []
        for hd in range(N_HEADS):
            g = hd // GROUP
            rs = slice(hd * t, (hd + 1) * t)
            cols = slice((g // 2) * LANES, (g // 2 + 1) * LANES)
            mix = (gates[:, hd:hd + 1] * oc_sc[rs, cols] + gates[:, N_HEADS + hd:N_HEADS + hd + 1] * o_s[rs, cols]
                   + gates[:, 2 * N_HEADS + hd:2 * N_HEADS + hd + 1] * ow_sc[rs, cols])
            placed.append(mix if g % 2 == hd % 2 else pltpu.roll(mix, HEAD_DIM, axis=1))
        for c in range(N_HEADS // 2):
            cols = slice(c * LANES, (c + 1) * LANES)
            o_ref[0, :, cols] = zgs_ref[0, :, cols] * jnp.where(lo, placed[2 * c], placed[2 * c + 1])


def _attn_sample(q, kca, vcb, kvs_new, kvw_new, win_state, gates, zgs, cache_sel, page_table, t, pg):
    bd, n_pages = page_table.shape
    past_len = n_pages * PAGE_SIZE
    nck = kca.shape[2]
    npb = past_len // SEL_BLOCK
    mmat = jnp.asarray(_imp_matrix(nck, npb), BF16)
    wk = win_state.shape[2]
    rpad = -(-N_HEADS * t // LANES) * LANES
    tok = lambda w: pl.BlockSpec((1, t, w), lambda b, s, pt: (0, b, 0))
    grid_spec = pltpu.PrefetchScalarGridSpec(
        num_scalar_prefetch=1,
        grid=(bd, n_pages // pg),
        in_specs=[pl.BlockSpec((1, N_HEADS, t, LANES), lambda b, s, pt: (0, 0, b, 0)),
                  pl.BlockSpec((1, KV_HEADS, nck, LANES), lambda b, s, pt: (b, 0, 0, 0)),
                  pl.BlockSpec((1, nck, KV_WIDTH), lambda b, s, pt: (b, 0, 0)),
                  tok(2 * KV_WIDTH), tok(2 * KV_WIDTH),
                  pl.BlockSpec((1, 2 * KV_WIDTH, wk), lambda b, s, pt: (b, 0, 0)),
                  tok(LANES), tok(MIX_WIDTH),
                  pl.BlockSpec(mmat.shape, lambda b, s, pt: (0, 0))] + _page_specs(pg),
        out_specs=tok(MIX_WIDTH),
        scratch_shapes=[pltpu.VMEM((rpad, 2 * LANES), BF16), pltpu.VMEM((npb + LANES, rpad), BF16),
                        pltpu.VMEM((rpad, 1), F32), pltpu.VMEM((rpad, 1), F32),
                        pltpu.VMEM((rpad, 2 * LANES), F32), pltpu.VMEM((rpad, 2 * LANES), F32),
                        pltpu.VMEM((rpad, 2 * LANES), F32)],
    )
    return pl.pallas_call(
        functools.partial(_attn_sample_kernel, pg=pg, past_len=past_len),
        grid_spec=grid_spec,
        out_shape=jax.ShapeDtypeStruct((1, bd * t, MIX_WIDTH), F32),
        compiler_params=_cparams("arbitrary", "arbitrary"),
        name="attn_sample",
    )(page_table, q, kca, vcb, kvs_new, kvw_new, win_state, gates, zgs, mmat, *([cache_sel] * pg))


def _reorder_nsa_weight(w):
    d = w.shape[0]
    n_gate = 3 * N_HEADS
    zg0 = _C_ZG + n_gate
    return jnp.concatenate([w[:, :_C_ZG], w[:, zg0:zg0 + MIX_WIDTH], w[:, _C_ZG:zg0],
                            jnp.zeros((d, LANES - n_gate), w.dtype)], axis=1).astype(BF16)


def _cmp_weights(cmp_pos_w, cmp_phi):
    ratio = CMP_BLOCK // CMP_STRIDE
    pw = cmp_pos_w.reshape(2, ratio, CMP_STRIDE, HEAD_DIM)
    tiles = [jnp.concatenate([jnp.tile(pw[s, m], (1, KV_HEADS)) for s in range(2)], axis=1) for m in range(ratio)]
    eye = jnp.eye(KV_HEADS, dtype=cmp_phi.dtype)
    z = jnp.zeros((KV_WIDTH, KV_WIDTH), cmp_phi.dtype)
    phi_bd = jnp.concatenate([jnp.concatenate([jnp.kron(eye, cmp_phi[0]), z], axis=1),
                              jnp.concatenate([z, jnp.kron(eye, cmp_phi[1])], axis=1)], axis=0).astype(BF16)
    return tiles[0], tiles[1], phi_bd


def _feature_major(cache):
    n, rows = cache.shape[:2]
    return jnp.transpose(cache, (0, 2, 3, 4, 1)).reshape(n, -1, rows)


def kernel(x_prompt, x_sample, mem_prompt, cache_mem_kv, cache_cmp_kv, cache_sel_kv, page_table, state_conv, state_win_kv, norm_g, final_norm_g, mem_norm_g, w_mem_kv, w_in_conv, conv_w, w_in_nsa, cmp_pos_w, cmp_phi, w_out):
    b, t, d = x_prompt.shape
    bd, td, _ = x_sample.shape
    n_mem = mem_prompt.shape[1]
    n_pages = page_table.shape[1]
    past_len = n_pages * PAGE_SIZE
    assert w_in_conv.shape[0] == 1 and w_in_nsa.shape[0] == 1 and w_out.shape[0] == 2
    assert CMP_BLOCK == 2 * CMP_STRIDE and td & (td - 1) == 0 and td < CMP_STRIDE

    w_conv_b = w_in_conv[0].astype(BF16)
    w_nsa_b = _reorder_nsa_weight(w_in_nsa[0])
    w_out_b = w_out.astype(BF16)
    w_mem_b = w_mem_kv.astype(BF16)
    pw0, pw1, phi_bd = _cmp_weights(cmp_pos_w[0], cmp_phi[0])
    kv6 = lambda a, n, r: a.reshape(1, n, r, 2, KV_HEADS, HEAD_DIM)

    mem_kv_p = _memory_kv(mem_prompt, mem_norm_g, w_mem_b)
    xp = x_prompt.reshape(b * t, d)
    tm = min(512, t)
    tmo_p = min(1024, t)
    mqg, mix, conv_p = _proj_conv(xp, norm_g[0], w_conv_b, conv_w[0], seg=t, tm=tm, mix_dtype=BF16)
    x1 = _layer_out(xp, mqg, mem_kv_p[0], mix, w_out_b[0], tm=tmo_p, seg=t)
    tabs = _rope_tables(jnp.arange(t, dtype=jnp.int32))
    tq = min(256, t)
    (mqg, q, kvc_p, kvs_p, kvw_p, gates, zgs, ksa, kwa, vsb, vwb) = _proj_nsa(
        x1.reshape(b, t, d), norm_g[1], w_nsa_b, tabs, tm=tm, prompt=True)
    kca, vcb = _cmp_prompt(kvc_p, pw0, pw1, phi_bd)
    mix = _attn_prompt(q, kca, vcb, ksa, vsb, kwa, vwb, gates, zgs, tq=tq)
    y_prompt = _layer_out(x1, mqg.reshape(b * t, MQG_WIDTH), mem_kv_p[1], mix.reshape(b * t, MIX_WIDTH), w_out_b[1],
                          tm=tmo_p, seg=t, final_g=final_norm_g).reshape(b, t, d)
    w_keep_p = min(WINDOW, t)

    ms = bd * td
    xs = x_sample.reshape(ms, d)
    st = state_conv[0]
    zrow = jnp.zeros((bd, td - 1, MIX_WIDTH), F32)
    s1 = jnp.concatenate([st[:, 1:2], zrow], axis=1).reshape(ms, MIX_WIDTH)
    s2 = jnp.concatenate([st, zrow[:, 1:]], axis=1).reshape(ms, MIX_WIDTH)
    mqg, mix, u_s = _proj_conv(xs, norm_g[0], w_conv_b, conv_w[0], seg=td, tm=ms, state=(s1, s2))
    conv_s = u_s.reshape(bd, td, MIX_WIDTH)[:, td - 2:]
    tmo = 8 * td
    mem_s = _feature_major(cache_mem_kv.reshape((-1,) + cache_mem_kv.shape[2:]))
    x1s = _layer_out(xs, mqg, mem_s, mix, w_out_b[0], tm=tmo, seg=td, feature_major=True)
    tabs_s = _rope_tables(jnp.tile(past_len + jnp.arange(td, dtype=jnp.int32), bd))
    (mqg, q_s, kvc_s, kvs_s, kvw_s, gates_s, zgs_s) = _proj_nsa(
        x1s.reshape(1, ms, d), norm_g[1], w_nsa_b, tabs_s, tm=ms, prompt=False)
    pg = min(64, n_pages)
    kca_s, vcb_s = _cmp_sample(_feature_major(cache_cmp_kv[0]), page_table, pw0, pw1, phi_bd, pg)
    mix_s = _attn_sample(q_s, kca_s, vcb_s, kvs_s, kvw_s, _feature_major(state_win_kv[0]), gates_s, zgs_s,
                         _feature_major(cache_sel_kv[0]), page_table, td, pg)
    y_sample = _layer_out(x1s, mqg.reshape(ms, MQG_WIDTH), mem_s, mix_s.reshape(ms, MIX_WIDTH), w_out_b[1], tm=tmo, seg=td,
                          final_g=final_norm_g, feature_major=True, memkv_offset=bd).reshape(bd, td, d)
    w_keep = state_win_kv.shape[2]
    win_s = jnp.concatenate([state_win_kv[0], kvw_s.reshape(bd, td, 2, KV_HEADS, HEAD_DIM)], axis=1)[:, -w_keep:]

    return (y_prompt, y_sample, conv_p[None],
            kv6(kvc_p, b, t), kv6(kvs_p, b, t), kv6(kvw_p[:, t - w_keep_p:], b, w_keep_p),
            mem_kv_p.reshape(2, b, n_mem, 2, MEM_WIDTH // HEAD_DIM, HEAD_DIM),
            conv_s[None], kv6(kvc_s, bd, td), kv6(kvs_s, bd, td), win_s[None])
```

```python
import functools

import numpy as np
import jax
import jax.numpy as jnp
from jax import lax
from jax.experimental import pallas as pl
from jax.experimental.pallas import tpu as pltpu

F32 = jnp.float32
BF16 = jnp.bfloat16

D_MODEL = 1024
HEAD_DIM = 64
MIX_WIDTH = 768
MEM_WIDTH = 256
N_HEADS = 12
KV_HEADS = 4
GROUP = 3
KV_WIDTH = 256
CMP_BLOCK = 32
CMP_STRIDE = 16
SEL_BLOCK = 64
N_SEL = 16
WINDOW = 512
ROT_DIM = 16
ROPE_THETA = 500000.0
NORM_EPS = 1e-6
FORCE_BONUS = 1e4
PAGE_SIZE = 128
SCALE = HEAD_DIM ** -0.5
MQG_WIDTH = 2 * MEM_WIDTH
KV2_WIDTH = 2 * KV_WIDTH
SEL_SHIFT = SEL_BLOCK.bit_length() - 1
_V_B, _V_C, _V_H, _V_Z, _V_END = (MQG_WIDTH + i * MIX_WIDTH for i in range(5))
NEG = -(2.0 ** 100)
M_INIT = -1e30
LANES = 128
VISIBILITY_VARIANTS = 4
SUM_ROWS = 16
PAGES_PER_UPDATE = 8
VMEM_LIMIT = 56 * 2 ** 20


def _cparams(*sem):
    return pltpu.CompilerParams(dimension_semantics=sem, vmem_limit_bytes=VMEM_LIMIT)


def _dot(a, b):
    return jnp.dot(a, b, preferred_element_type=F32)


def _dot_nt(a, b):
    return lax.dot_general(a, b, (((1,), (1,)), ((), ())), preferred_element_type=F32)


def _rms(x, g):
    return x * lax.rsqrt(jnp.mean(x * x, axis=-1, keepdims=True) + NORM_EPS) * g


def _silu(x):
    return x * jax.nn.sigmoid(x)


def _split3(a):
    hi = a.astype(BF16)
    r1 = a - hi.astype(F32)
    mid = r1.astype(BF16)
    lo = (r1 - mid.astype(F32)).astype(BF16)
    return hi, mid, lo


def _lo_half(rows):
    return lax.broadcasted_iota(jnp.int32, (rows, LANES), 1) < HEAD_DIM


def _memkv_kernel(mem_ref, g_ref, w_ref, o_ref):
    h = _rms(mem_ref[0], g_ref[0]).astype(BF16)
    o_ref[0, 0] = _dot(h, w_ref[0])


def _memory_kv(mem, mem_norm_g, w_mem_kv_b):
    depth = w_mem_kv_b.shape[0]
    b, n_mem, d = mem.shape
    return pl.pallas_call(
        _memkv_kernel,
        grid=(depth, b),
        in_specs=[pl.BlockSpec((1, n_mem, d), lambda i, j: (j, 0, 0)),
                  pl.BlockSpec((1, 1, d), lambda i, j: (i, 0, 0)),
                  pl.BlockSpec((1, d, 2 * MEM_WIDTH), lambda i, j: (i, 0, 0))],
        out_specs=pl.BlockSpec((1, 1, n_mem, 2 * MEM_WIDTH), lambda i, j: (i, j, 0, 0)),
        out_shape=jax.ShapeDtypeStruct((depth, b, n_mem, 2 * MEM_WIDTH), F32),
        compiler_params=_cparams("arbitrary", "arbitrary"),
        name="memory_kv",
    )(mem, mem_norm_g.reshape(depth, 1, d), w_mem_kv_b)


def _proj_conv_kernel(*refs, tm, seg, has_state):
    if has_state:
        x_ref, g_ref, w_ref, cw_ref, s1_ref, s2_ref, mqg_ref, mix_ref, st_ref, ubuf = refs
    else:
        x_ref, g_ref, w_ref, cw_ref, mqg_ref, mix_ref, st_ref, ubuf = refs
    i = pl.program_id(0)
    h = _rms(x_ref[...], g_ref[...]).astype(BF16)
    mqg_ref[...] = _dot(h, w_ref[:, 0:_V_B])
    bg = _dot(h, w_ref[:, _V_B:_V_C])
    cg = _dot(h, w_ref[:, _V_C:_V_H])
    hin = _dot(h, w_ref[:, _V_H:_V_Z])
    zg = _dot(h, w_ref[:, _V_Z:_V_END])
    u = cg * hin

    @pl.when(i == 0)
    def _():
        ubuf[0:8, :] = jnp.zeros((8, MIX_WIDTH), F32)

    @pl.when(i > 0)
    def _():
        ubuf[0:8, :] = ubuf[tm:tm + 8, :]

    ubuf[8:8 + tm, :] = u
    u1 = ubuf[7:7 + tm, :]
    u2 = ubuf[6:6 + tm, :]
    rowpos = lax.rem(i * tm + lax.broadcasted_iota(jnp.int32, (tm, 1), 0), seg)
    if has_state:
        u1 = jnp.where(rowpos >= 1, u1, s1_ref[...])
        u2 = jnp.where(rowpos >= 2, u2, s2_ref[...])
        st_ref[...] = u
    else:
        u1 = jnp.where(rowpos >= 1, u1, 0.0)
        u2 = jnp.where(rowpos >= 2, u2, 0.0)
        st_ref[0] = ubuf[8 + tm - 2:8 + tm, :]
    y = cw_ref[0:1, :] * u2 + cw_ref[1:2, :] * u1 + cw_ref[2:3, :] * u
    mix_ref[...] = (_silu(zg) * (bg * y)).astype(mix_ref.dtype)


def _proj_conv(x2d, g, w_b, cw, seg, tm, state=None, mix_dtype=F32):
    m = x2d.shape[0]
    n_in = w_b.shape[1]
    has_state = state is not None
    in_specs = [pl.BlockSpec((tm, D_MODEL), lambda i: (i, 0)),
                pl.BlockSpec((1, D_MODEL), lambda i: (0, 0)),
                pl.BlockSpec((D_MODEL, n_in), lambda i: (0, 0)),
                pl.BlockSpec((3, MIX_WIDTH), lambda i: (0, 0))]
    args = [x2d, g.reshape(1, D_MODEL), w_b, cw]
    if has_state:
        in_specs += [pl.BlockSpec((tm, MIX_WIDTH), lambda i: (i, 0))] * 2
        args += list(state)
        st_shape = jax.ShapeDtypeStruct((m, MIX_WIDTH), F32)
        st_spec = pl.BlockSpec((tm, MIX_WIDTH), lambda i: (i, 0))
    else:
        st_shape = jax.ShapeDtypeStruct((m // seg, 2, MIX_WIDTH), F32)
        st_spec = pl.BlockSpec((1, 2, MIX_WIDTH), lambda i: ((i * tm) // seg, 0, 0))
    return pl.pallas_call(
        functools.partial(_proj_conv_kernel, tm=tm, seg=seg, has_state=has_state),
        grid=(m // tm,),
        in_specs=in_specs,
        out_specs=[pl.BlockSpec((tm, MQG_WIDTH), lambda i: (i, 0)),
                   pl.BlockSpec((tm, MIX_WIDTH), lambda i: (i, 0)),
                   st_spec],
        out_shape=[jax.ShapeDtypeStruct((m, MQG_WIDTH), F32),
                   jax.ShapeDtypeStruct((m, MIX_WIDTH), mix_dtype),
                   st_shape],
        scratch_shapes=[pltpu.VMEM((tm + 8, MIX_WIDTH), F32)],
        compiler_params=_cparams("arbitrary"),
        name="proj_conv",
    )(*args)


def _out_kernel(*refs, nseg, seg, final, feature_major):
    if final:
        x_ref, mqg_ref, memkv_ref, mix_ref, w_ref, g_ref, o_ref, mo_sc = refs
    else:
        x_ref, mqg_ref, memkv_ref, mix_ref, w_ref, o_ref, mo_sc = refs
    lo = _lo_half(seg)

    def seg_body(s, r0):
        for pair in range(2):
            cols = slice(pair * LANES, (pair + 1) * LANES)
            qp = mqg_ref[pl.ds(r0, seg), cols]
            vcols = slice(MEM_WIDTH + pair * LANES, MEM_WIDTH + (pair + 1) * LANES)
            if feature_major:
                kp, vp = memkv_ref[s, cols, :].astype(BF16), memkv_ref[s, vcols, :].astype(BF16)
            else:
                kp, vp = memkv_ref[s, :, cols].astype(BF16), memkv_ref[s, :, vcols].astype(BF16)
            outs = []
            for half in range(2):
                keep = lo if half == 0 else jnp.logical_not(lo)
                qm = jnp.where(keep, qp, 0.0).astype(BF16)
                sc = (_dot(qm, kp) if feature_major else _dot_nt(qm, kp)) * SCALE
                e = jnp.exp(sc - jnp.max(sc, axis=-1, keepdims=True))
                p = e / jnp.sum(e, axis=-1, keepdims=True)
                outs.append(_dot_nt(p.astype(BF16), vp) if feature_major else _dot(p.astype(BF16), vp))
            mo_sc[pl.ds(r0, seg), cols] = jnp.where(lo, outs[0], outs[1])

    if nseg == 1:
        seg_body(0, 0)
    else:
        def body(s, c):
            seg_body(s, pl.multiple_of(s * seg, seg))
            return c
        lax.fori_loop(0, nseg, body, 0)

    mg = mqg_ref[:, MEM_WIDTH:2 * MEM_WIDTH]
    a = (_silu(mg) * mo_sc[...]).astype(BF16)
    xn = x_ref[...] + (_dot(a, w_ref[0:MEM_WIDTH, :]) + _dot(mix_ref[...].astype(BF16), w_ref[MEM_WIDTH:, :]))
    if final:
        o_ref[...] = _rms(xn, g_ref[...])
    else:
        o_ref[...] = xn


def _layer_out(x2d, mqg, memkv, mix, w_out_b, tm, seg, final_g=None, feature_major=False, memkv_offset=0):
    m = x2d.shape[0]
    nseg = max(tm // seg, 1)
    seg_in = min(seg, tm)
    final = final_g is not None
    in_specs = [pl.BlockSpec((tm, D_MODEL), lambda i: (i, 0)),
                pl.BlockSpec((tm, MQG_WIDTH), lambda i: (i, 0)),
                pl.BlockSpec((nseg,) + memkv.shape[1:], lambda i: (memkv_offset // nseg + (i * tm) // (seg * nseg), 0, 0)),
                pl.BlockSpec((tm, MIX_WIDTH), lambda i: (i, 0)),
                pl.BlockSpec((D_MODEL, D_MODEL), lambda i: (0, 0))]
    args = [x2d, mqg, memkv, mix, w_out_b]
    if final:
        in_specs.append(pl.BlockSpec((1, D_MODEL), lambda i: (0, 0)))
        args.append(final_g.reshape(1, D_MODEL))
    return pl.pallas_call(
        functools.partial(_out_kernel, nseg=nseg, seg=seg_in, final=final, feature_major=feature_major),
        grid=(m // tm,),
        in_specs=in_specs,
        out_specs=pl.BlockSpec((tm, D_MODEL), lambda i: (i, 0)),
        out_shape=jax.ShapeDtypeStruct((m, D_MODEL), F32),
        scratch_shapes=[pltpu.VMEM((tm, MEM_WIDTH), F32)],
        compiler_params=_cparams("arbitrary"),
        name="layer_out",
    )(*args)


_C_Q = MQG_WIDTH
_C_KC = _C_Q + MIX_WIDTH
_C_KS, _C_KW, _C_ZG = (_C_KC + i * KV2_WIDTH for i in (1, 2, 3))
_C_GL = _C_ZG + MIX_WIDTH
_C_END = _C_GL + LANES


def _proj_nsa_kernel(*refs, tm, prompt):
    (x_ref, g_ref, w_ref, c_ref, sa_ref, sb_ref,
     mqg_ref, q_ref, kvc_ref, kvs_ref, kvw_ref, gates_ref, zgs_ref) = refs[:13]
    ti = pl.program_id(1)
    h = _rms(x_ref[0], g_ref[...]).astype(BF16)
    cos, sa, sb = c_ref[...], sa_ref[...], sb_ref[...]
    lo = _lo_half(tm)

    def rope(chunk):
        return chunk * cos + pltpu.roll(chunk, LANES - ROT_DIM // 2, axis=1) * sa + pltpu.roll(chunk, ROT_DIM // 2, axis=1) * sb

    mqg_ref[0] = _dot(h, w_ref[:, 0:_C_Q])
    qf = _dot(h, w_ref[:, _C_Q:_C_KC])
    for c in range(N_HEADS // 2):
        qc = rope(qf[:, c * LANES:(c + 1) * LANES]) * SCALE
        if prompt:
            q_ref[0, c * LANES:(c + 1) * LANES, :] = qc.T.astype(q_ref.dtype)
        else:
            q_ref[0, 2 * c] = jnp.where(lo, qc, 0.0).astype(q_ref.dtype)
            q_ref[0, 2 * c + 1] = jnp.where(lo, pltpu.roll(qc, HEAD_DIM, axis=1), 0.0).astype(q_ref.dtype)

    if prompt:
        ksa_ref, kwa_ref, vsb_ref, vwb_ref = refs[13:17]
        pos = ti * tm + lax.broadcasted_iota(jnp.int32, (tm, LANES), 0)
        lane = lax.broadcasted_iota(jnp.int32, (tm, LANES), 1)
        onehot = jnp.where(jnp.right_shift(pos, SEL_SHIFT) == lane - HEAD_DIM, 1.0, 0.0)

    for name, off in (("c", _C_KC), ("s", _C_KS), ("w", _C_KW)):
        kv_ref = {"c": kvc_ref, "s": kvs_ref, "w": kvw_ref}[name]
        kk = _dot(h, w_ref[:, off:off + KV_WIDTH])
        vv = _dot(h, w_ref[:, off + KV_WIDTH:off + 2 * KV_WIDTH])
        kv_ref[0, :, KV_WIDTH:2 * KV_WIDTH] = vv
        for c in range(KV_HEADS // 2):
            kr = rope(kk[:, c * LANES:(c + 1) * LANES])
            kv_ref[0, :, c * LANES:(c + 1) * LANES] = kr
            if prompt and name != "c":
                pad = onehot if name == "s" else 0.0
                aug_ref = ksa_ref if name == "s" else kwa_ref
                aug_ref[0, 2 * c] = jnp.where(lo, kr, pad).astype(BF16)
                aug_ref[0, 2 * c + 1] = jnp.where(lo, pltpu.roll(kr, HEAD_DIM, axis=1), pad).astype(BF16)
        if prompt and name != "c":
            vt_ref = vsb_ref if name == "s" else vwb_ref
            for c in range(KV_WIDTH // LANES):
                vt_ref[0, c * LANES:(c + 1) * LANES, :] = vv[:, c * LANES:(c + 1) * LANES].T.astype(BF16)

    gates = jax.nn.sigmoid(_dot(h, w_ref[:, _C_GL:_C_END]))
    gates_ref[0] = gates.T if prompt else gates
    zgs_ref[0] = _silu(_dot(h, w_ref[:, _C_ZG:_C_GL]))


def _proj_nsa(x3d, g, w_b, tabs, tm, prompt):
    nb, t, _ = x3d.shape
    row = lambda w: pl.BlockSpec((1, tm, w), lambda b, i: (b, i, 0))
    head = lambda n: pl.BlockSpec((1, n, tm, LANES), lambda b, i: (b, 0, i, 0))
    tab = pl.BlockSpec((tm, LANES), lambda b, i: (i, 0))
    sds = jax.ShapeDtypeStruct
    col = lambda w: pl.BlockSpec((1, w, tm), lambda b, i: (b, 0, i))
    if prompt:
        q_spec, q_shape = col(MIX_WIDTH), sds((nb, MIX_WIDTH, t), BF16)
        g_spec, g_shape = col(LANES), sds((nb, LANES, t), F32)
    else:
        q_spec, q_shape = head(N_HEADS), sds((nb, N_HEADS, t, LANES), F32)
        g_spec, g_shape = row(LANES), sds((nb, t, LANES), F32)
    out_specs = [row(MQG_WIDTH), q_spec, row(KV2_WIDTH), row(KV2_WIDTH), row(KV2_WIDTH), g_spec, row(MIX_WIDTH)]
    out_shape = [sds((nb, t, MQG_WIDTH), F32), q_shape,
                 sds((nb, t, KV2_WIDTH), F32), sds((nb, t, KV2_WIDTH), F32), sds((nb, t, KV2_WIDTH), F32),
                 g_shape, sds((nb, t, MIX_WIDTH), F32)]
    if prompt:
        out_specs += [head(KV_HEADS), head(KV_HEADS), col(KV_WIDTH), col(KV_WIDTH)]
        out_shape += [sds((nb, KV_HEADS, t, LANES), BF16), sds((nb, KV_HEADS, t, LANES), BF16),
                      sds((nb, KV_WIDTH, t), BF16), sds((nb, KV_WIDTH, t), BF16)]
    return pl.pallas_call(
        functools.partial(_proj_nsa_kernel, tm=tm, prompt=prompt),
        grid=(nb, t // tm),
        in_specs=[pl.BlockSpec((1, tm, D_MODEL), lambda b, i: (b, i, 0)),
                  pl.BlockSpec((1, D_MODEL), lambda b, i: (0, 0)),
                  pl.BlockSpec((D_MODEL, _C_END), lambda b, i: (0, 0)),
                  tab, tab, tab],
        out_specs=out_specs,
        out_shape=out_shape,
        compiler_params=_cparams("arbitrary", "arbitrary"),
        name="proj_nsa",
    )(x3d, g.reshape(1, D_MODEL), w_b, *tabs)


def _rope_tables(pos):
    half = ROT_DIM // 2
    inv = ROPE_THETA ** (-jnp.arange(half, dtype=F32) * 2.0 / ROT_DIM)
    ang = pos.astype(F32)[:, None] * inv[None, :]
    cos, sin = jnp.cos(ang), jnp.sin(ang)
    n = pos.shape[0]
    one = jnp.ones((n, HEAD_DIM - ROT_DIM), F32)
    zero = jnp.zeros((n, HEAD_DIM - ROT_DIM), F32)
    zh = jnp.zeros((n, half), F32)
    c = jnp.concatenate([cos, cos, one], axis=1)
    sa = jnp.concatenate([-sin, zh, zero], axis=1)
    sb = jnp.concatenate([zh, sin, zero], axis=1)
    return tuple(jnp.tile(a, (1, LANES // HEAD_DIM)) for a in (c, sa, sb))


def _emit_cmp_blocks(out, kca_ref, v_ref, transpose_v):
    n = out.shape[0]
    lo = _lo_half(n)
    for c in range(KV_HEADS // 2):
        kc = out[:, c * LANES:(c + 1) * LANES]
        kca_ref[0, 2 * c] = jnp.where(lo, kc, 0.0).astype(BF16)
        kca_ref[0, 2 * c + 1] = jnp.where(lo, pltpu.roll(kc, HEAD_DIM, axis=1), 0.0).astype(BF16)
        if transpose_v:
            v_ref[0, c * LANES:(c + 1) * LANES, :] = out[:, KV_WIDTH + c * LANES:KV_WIDTH + (c + 1) * LANES].T.astype(BF16)
    if not transpose_v:
        v_ref[0] = out[:, KV_WIDTH:].astype(BF16)


def _cmp_prompt_kernel(kvc_ref, pw0_ref, pw1_ref, phi_ref, kca_ref, vcb_ref, p0_sc, p1_sc, *, rows):
    t = kvc_ref.shape[1]
    cpr = rows // CMP_STRIDE
    pw0 = pw0_ref[...][None]
    pw1 = pw1_ref[...][None]
    for i in range(t // rows):
        x = kvc_ref[0, i * rows:(i + 1) * rows, :].reshape(cpr, CMP_STRIDE, 2 * KV_WIDTH)
        p0_sc[i * cpr:(i + 1) * cpr, :] = jnp.sum(x * pw0, axis=1)
        p1_sc[i * cpr:(i + 1) * cpr, :] = jnp.sum(x * pw1, axis=1)
    nck = t // CMP_STRIDE
    blk = p0_sc[...] + pltpu.roll(p1_sc[...], nck - 1, axis=0)
    _emit_cmp_blocks(_dot(blk.astype(BF16), phi_ref[...]), kca_ref, vcb_ref, transpose_v=True)


def _cmp_prompt(kvc, pw0, pw1, phi_bd):
    b, t, _ = kvc.shape
    nck = t // CMP_STRIDE
    full = lambda s: pl.BlockSpec(s, lambda i: (0,) * len(s))
    return pl.pallas_call(
        functools.partial(_cmp_prompt_kernel, rows=min(t, 32 * CMP_STRIDE)),
        grid=(b,),
        in_specs=[pl.BlockSpec((1, t, 2 * KV_WIDTH), lambda i: (i, 0, 0)),
                  full((CMP_STRIDE, 2 * KV_WIDTH)), full((CMP_STRIDE, 2 * KV_WIDTH)),
                  full((2 * KV_WIDTH, 2 * KV_WIDTH))],
        out_specs=[pl.BlockSpec((1, KV_HEADS, nck, LANES), lambda i: (i, 0, 0, 0)),
                   pl.BlockSpec((1, KV_WIDTH, nck), lambda i: (i, 0, 0))],
        out_shape=[jax.ShapeDtypeStruct((b, KV_HEADS, nck, LANES), BF16),
                   jax.ShapeDtypeStruct((b, KV_WIDTH, nck), BF16)],
        scratch_shapes=[pltpu.VMEM((nck, 2 * KV_WIDTH), F32), pltpu.VMEM((nck, 2 * KV_WIDTH), F32)],
        compiler_params=_cparams("arbitrary"),
        name="cmp_prompt",
    )(kvc, pw0, pw1, phi_bd)


def _cmp_sample_kernel(pt_ref, w0_ref, w1_ref, seg_ref, phi_ref, *refs, pg):
    pages = refs[:pg]
    kca_ref, vcb_ref, p0_sc, p1_sc = refs[pg:]
    step = pl.program_id(1)
    cpp = PAGE_SIZE // CMP_STRIDE
    for k in range(pg):
        xt = pages[k][0]
        r0 = pl.multiple_of((step * pg + k) * cpp, cpp)
        for w_ref, p_sc in ((w0_ref, p0_sc), (w1_ref, p1_sc)):
            y = (xt * w_ref[...]).astype(BF16)
            p_sc[pl.ds(r0, cpp), :] = _dot_nt(seg_ref[...], y)[0:cpp]

    @pl.when(step == pl.num_programs(1) - 1)
    def _():
        nck = p0_sc.shape[0]
        blk = p0_sc[...] + pltpu.roll(p1_sc[...], nck - 1, axis=0)
        _emit_cmp_blocks(_dot(blk.astype(BF16), phi_ref[...]), kca_ref, vcb_ref, transpose_v=False)


def _page_specs(pg):
    return [pl.BlockSpec((1, 2 * KV_WIDTH, PAGE_SIZE), functools.partial(
        lambda b, s, pt, k: (pt[b, s * pg + k], 0, 0), k=k)) for k in range(pg)]


def _cmp_sample(cache_t, page_table, pw0, pw1, phi_bd, pg):
    bd, n_pages = page_table.shape
    nck = n_pages * PAGE_SIZE // CMP_STRIDE
    cpp = PAGE_SIZE // CMP_STRIDE
    w0, w1 = (jnp.tile(pw.T, (1, cpp)) for pw in (pw0, pw1))
    seg = np.zeros((max(cpp, 16), PAGE_SIZE), np.float32)
    seg[np.arange(PAGE_SIZE) // CMP_STRIDE, np.arange(PAGE_SIZE)] = 1.0
    full = lambda s: pl.BlockSpec(s, lambda b, i, pt: (0,) * len(s))
    grid_spec = pltpu.PrefetchScalarGridSpec(
        num_scalar_prefetch=1,
        grid=(bd, n_pages // pg),
        in_specs=[full(w0.shape), full(w1.shape), full(seg.shape),
                  full((2 * KV_WIDTH, 2 * KV_WIDTH))] + _page_specs(pg),
        out_specs=[pl.BlockSpec((1, KV_HEADS, nck, LANES), lambda b, i, pt: (b, 0, 0, 0)),
                   pl.BlockSpec((1, nck, KV_WIDTH), lambda b, i, pt: (b, 0, 0))],
        scratch_shapes=[pltpu.VMEM((nck, 2 * KV_WIDTH), F32), pltpu.VMEM((nck, 2 * KV_WIDTH), F32)],
    )
    return pl.pallas_call(
        functools.partial(_cmp_sample_kernel, pg=pg),
        grid_spec=grid_spec,
        out_shape=[jax.ShapeDtypeStruct((bd, KV_HEADS, nck, LANES), BF16),
                   jax.ShapeDtypeStruct((bd, nck, KV_WIDTH), BF16)],
        compiler_params=_cparams("arbitrary", "arbitrary"),
        name="cmp_sample",
    )(page_table, w0, w1, jnp.asarray(seg, BF16), phi_bd, *([cache_t] * pg))


def _imp_matrix(nck, n_sel_blocks):
    cps = SEL_BLOCK // CMP_STRIDE
    c = np.arange(nck)[:, None]
    j = np.arange(n_sel_blocks)[None, :]
    m = ((c >= cps * j) & (c <= cps * j + cps - 1)).astype(np.float32)
    m += ((c + 1 >= cps * j) & (c + 1 <= cps * j + cps - 1)).astype(np.float32)
    return m


def _softmax_masked(s, mask, axis=-1):
    s = jnp.where(mask, s, -1e30)
    p = jnp.exp(s - jnp.max(s, axis=axis, keepdims=True)) * mask.astype(F32)
    return p / jnp.maximum(jnp.sum(p, axis=axis, keepdims=True), 1e-30)


def _online_update_t(s, vt_ones, m_sc, acc_sc):
    m_old = m_sc[...]
    m_new = jnp.maximum(m_old, jnp.max(s, axis=0, keepdims=True))
    p = jnp.exp((s - m_new).astype(BF16))
    acc_sc[...] = jnp.exp(m_old - m_new) * acc_sc[...] + _dot(vt_ones, p)
    m_sc[...] = m_new


def _attn_prompt_kernel(q_ref, kca_ref, vcb_ref, ksa_ref, vsb_ref, kwa_ref, vwb_ref, gates_ref, zgs_ref,
                        m2t_ref, o_ref, m_sc, acc_sc, score_sc, sa_sc, sb_sc, oc_sc, pen_sc, *, tq, n_sel):
    qi = pl.program_id(1)
    s0 = qi * tq
    cols = GROUP * tq
    nsb = m2t_ref.shape[0]
    qpos = s0 + lax.broadcasted_iota(jnp.int32, (1, tq), 1)
    qpos3 = jnp.concatenate([qpos] * GROUP, axis=1)
    gates = gates_ref[0]
    zero_h = jnp.zeros((HEAD_DIM, tq), BF16)
    mix_t = []
    for g in range(KV_HEADS):
        vrows = slice(g * HEAD_DIM, (g + 1) * HEAD_DIM)
        qh = [q_ref[0, (GROUP * g + r) * HEAD_DIM:(GROUP * g + r + 1) * HEAD_DIM, :] for r in range(GROUP)]
        qc = jnp.concatenate([jnp.concatenate([x, zero_h], axis=0) for x in qh], axis=1)

        def compressed_and_select(rows, slots):
            sc = _dot(kca_ref[0, g, 0:rows, :], qc)
            blk_end = lax.broadcasted_iota(jnp.int32, (rows, 1), 0) * CMP_STRIDE + (CMP_BLOCK - 1)
            pc = _softmax_masked(sc, blk_end <= qpos3, axis=0)
            oc_sc[...] = _dot(vcb_ref[0, vrows, 0:rows], pc.astype(BF16))
            imp = pc[:, 0:tq] + pc[:, tq:2 * tq] + pc[:, 2 * tq:3 * tq]

            imp_t = None
            for part in _split3(imp):
                d = _dot(m2t_ref[0:slots, 0:rows], part)
                imp_t = d if imp_t is None else imp_t + d
            j = lax.broadcasted_iota(jnp.int32, (slots, tq), 0)
            cur = jnp.right_shift(s0 + lax.broadcasted_iota(jnp.int32, (slots, tq), 1), SEL_SHIFT)
            valid = j <= cur
            forced = (j == 0) | (j == cur) | (j == cur - 1)
            score = jnp.where(valid, imp_t + jnp.where(forced, FORCE_BONUS, 0.0), -jnp.inf)
            score_sc[0:slots, :] = score
            bpt = tq // SEL_BLOCK

            def rank_body(it, cnt):
                for u in range(bpt):
                    i = it * bpt + u
                    row = score_sc[pl.ds(i, 1), :]
                    tie = jnp.where(j > i, 1.0, 0.0)
                    cnt = cnt + jnp.where(row > score, 1.0, jnp.where(row == score, tie, 0.0))
                return cnt

            cnt = lax.fori_loop(0, jnp.minimum(qi + 1, slots // bpt), rank_body, jnp.zeros((slots, tq), F32))
            pen = jnp.where((cnt < float(n_sel)) & valid, 0.0, NEG)
            if slots < nsb:
                pen = jnp.concatenate([pen, jnp.full((nsb - slots, tq), NEG, F32)], axis=0)
            pen_sc[...] = pen.astype(BF16)

        ncp = kca_ref.shape[2]
        n_var = VISIBILITY_VARIANTS if ncp % (VISIBILITY_VARIANTS * 4 * 16) == 0 and nsb * 4 == ncp else 1
        chunk = ncp // n_var
        visible = (s0 + tq - CMP_BLOCK) // CMP_STRIDE + 1
        variant = jnp.clip((visible + chunk - 1) // chunk, 1, n_var)
        for k in range(1, n_var + 1):
            pl.when(variant == k)(functools.partial(compressed_and_select, chunk * k, nsb * k // n_var))
        oc = oc_sc[...]
        pen = pen_sc[...]
        qa = jnp.concatenate([jnp.concatenate([x, pen], axis=0) for x in qh], axis=1)

        def tile_start(kt):
            return pl.multiple_of(kt * tq, tq)

        def key_pos(kt):
            return tile_start(kt) + lax.broadcasted_iota(jnp.int32, (tq, 1), 0)

        def run_branch(scores, values, first, last_mask):
            m_sc[...] = jnp.full(m_sc.shape, M_INIT, F32)
            acc_sc[...] = jnp.zeros(acc_sc.shape, F32)
            ones = jnp.ones((SUM_ROWS, tq), BF16)

            def update(s_ref, kt, mask=None):
                s = s_ref[...] if mask is None else mask(s_ref[...])
                _online_update_t(s, jnp.concatenate([values(kt), ones], axis=0), m_sc, acc_sc)

            def pair(kt):
                sb_sc[...] = scores(kt + 1)
                update(sa_sc, kt)
                sa_sc[...] = scores(kt + 2)
                update(sb_sc, kt + 1)

            def trip(i, c):
                pair(first + 4 * i)
                pair(first + 4 * i + 2)
                return c

            ahead = qi - first
            n_pairs = ahead // 2
            sa_sc[...] = scores(first)
            lax.fori_loop(0, n_pairs // 2, trip, 0)

            @pl.when(n_pairs % 2 == 1)
            def _():
                pair(first + 2 * (n_pairs - 1))

            @pl.when(ahead % 2 == 1)
            def _():
                sb_sc[...] = scores(qi)
                update(sa_sc, qi - 1)
                update(sb_sc, qi, last_mask)

            @pl.when(ahead % 2 == 0)
            def _():
                update(sa_sc, qi, last_mask)

            return acc_sc[0:HEAD_DIM, :] / acc_sc[HEAD_DIM:HEAD_DIM + 1, :]

        o_s = run_branch(
            lambda kt: _dot(ksa_ref[0, g, pl.ds(tile_start(kt), tq), :], qa),
            lambda kt: vsb_ref[0, vrows, pl.ds(tile_start(kt), tq)],
            0, lambda s: jnp.where(key_pos(qi) <= qpos3, s, NEG))

        def win_scores(kt, lower, causal):
            s = _dot(kwa_ref[0, g, pl.ds(tile_start(kt), tq), :], qc)
            if lower:
                s = jnp.where(key_pos(kt) > qpos3 - WINDOW, s, NEG)
            if causal:
                s = jnp.where(key_pos(kt) <= qpos3, s, NEG)
            return s

        def win_update(s_ref, kt):
            vt_ones = jnp.concatenate([vwb_ref[0, vrows, pl.ds(tile_start(kt), tq)], jnp.ones((SUM_ROWS, tq), BF16)], axis=0)
            _online_update_t(s_ref[...], vt_ones, m_sc, acc_sc)

        m_sc[...] = jnp.full(m_sc.shape, M_INIT, F32)
        acc_sc[...] = jnp.zeros(acc_sc.shape, F32)

        @pl.when(qi >= 2)
        def _():
            sa_sc[...] = win_scores(qi - 2, True, False)
            sb_sc[...] = win_scores(qi - 1, False, False)
            win_update(sa_sc, qi - 2)
            sa_sc[...] = win_scores(qi, False, True)
            win_update(sb_sc, qi - 1)
            win_update(sa_sc, qi)

        @pl.when(qi == 1)
        def _():
            sa_sc[...] = win_scores(0, False, False)
            sb_sc[...] = win_scores(1, False, True)
            win_update(sa_sc, 0)
            win_update(sb_sc, 1)

        @pl.when(qi == 0)
        def _():
            sa_sc[...] = win_scores(0, False, True)
            win_update(sa_sc, 0)

        o_w = acc_sc[0:HEAD_DIM, :] / acc_sc[HEAD_DIM:HEAD_DIM + 1, :]

        for r in range(GROUP):
            hd = GROUP * g + r
            cs = slice(r * tq, (r + 1) * tq)
            mix_t.append(gates[hd:hd + 1, :] * oc[:, cs] + gates[N_HEADS + hd:N_HEADS + hd + 1, :] * o_s[:, cs]
                         + gates[2 * N_HEADS + hd:2 * N_HEADS + hd + 1, :] * o_w[:, cs])
    for c in range(N_HEADS // 2):
        lanes = slice(c * LANES, (c + 1) * LANES)
        mix = jnp.concatenate([mix_t[2 * c], mix_t[2 * c + 1]], axis=0).T
        o_ref[0, :, lanes] = (zgs_ref[0, :, lanes] * mix).astype(o_ref.dtype)


def _attn_prompt(q, kca, vcb, ksa, vsb, kwa, vwb, gates, zgs, tq):
    b, _, t = q.shape
    nck = kca.shape[2]
    nsb = -(-t // SEL_BLOCK)
    assert WINDOW == 2 * tq or t <= tq
    slots = LANES - HEAD_DIM
    assert nsb <= slots
    m2t = np.zeros((slots, nck), np.float32)
    m2t[:nsb] = _imp_matrix(nck, nsb).T
    per_b4 = lambda n, r: pl.BlockSpec((1, n, r, LANES), lambda i, j: (i, 0, 0, 0))
    per_b3 = lambda r, w: pl.BlockSpec((1, r, w), lambda i, j: (i, 0, 0))
    cols = GROUP * tq
    return pl.pallas_call(
        functools.partial(_attn_prompt_kernel, tq=tq, n_sel=min(N_SEL, nsb)),
        grid=(b, t // tq),
        in_specs=[pl.BlockSpec((1, MIX_WIDTH, tq), lambda i, j: (i, 0, j)),
                  per_b4(KV_HEADS, nck), per_b3(KV_WIDTH, nck),
                  per_b4(KV_HEADS, t), per_b3(KV_WIDTH, t),
                  per_b4(KV_HEADS, t), per_b3(KV_WIDTH, t),
                  pl.BlockSpec((1, LANES, tq), lambda i, j: (i, 0, j)),
                  pl.BlockSpec((1, tq, MIX_WIDTH), lambda i, j: (i, j, 0)),
                  pl.BlockSpec(m2t.shape, lambda i, j: (0, 0))],
        out_specs=pl.BlockSpec((1, tq, MIX_WIDTH), lambda i, j: (i, j, 0)),
        out_shape=jax.ShapeDtypeStruct((b, t, MIX_WIDTH), BF16),
        scratch_shapes=[pltpu.VMEM((1, cols), F32), pltpu.VMEM((HEAD_DIM + SUM_ROWS, cols), F32),
                        pltpu.VMEM((slots, tq), F32), pltpu.VMEM((tq, cols), F32), pltpu.VMEM((tq, cols), F32),
                        pltpu.VMEM((HEAD_DIM, cols), F32), pltpu.VMEM((slots, tq), BF16)],
        compiler_params=_cparams("arbitrary", "arbitrary"),
        name="attn_prompt",
    )(q, kca, vcb, ksa, vsb, kwa, vwb, gates, zgs, jnp.asarray(m2t, BF16))


def _online_update(s, vt, m_sc, l_sc, acc_sc):
    m_old = m_sc[...]
    m_new = jnp.maximum(m_old, jnp.max(s, axis=-1, keepdims=True))
    alpha = jnp.exp(m_old - m_new)
    p = jnp.exp(s - m_new)
    l_sc[...] = alpha * l_sc[...] + jnp.sum(p, axis=-1, keepdims=True)
    acc_sc[...] = alpha * acc_sc[...] + _dot_nt(p.astype(BF16), vt)
    m_sc[...] = m_new


def _attn_sample_kernel(pt_ref, q_ref, kca_ref, vcb_ref, ksn_ref, kwn_ref, win_ref, gates_ref, zgs_ref, mmat_ref,
                        *refs, pg, past_len):
    pages = refs[:pg]
    o_ref, qbd_sc, pen_sc, m_sc, l_sc, acc_sc, oc_sc, ow_sc = refs[pg:]
    step = pl.program_id(1)
    t = q_ref.shape[2]
    rows = N_HEADS * t
    rpad = qbd_sc.shape[0]
    npb = mmat_ref.shape[1]
    tok = jnp.bitwise_and(lax.broadcasted_iota(jnp.int32, (rpad, 1), 0), t - 1)
    qpos = past_len + tok

    @pl.when(step == 0)
    def _():
        zero_t = jnp.zeros((t, LANES), F32)
        qrows = []
        for hd in range(N_HEADS):
            g = hd // GROUP
            qh = q_ref[0, hd]
            if g % 2 == 1:
                qh = pltpu.roll(qh, HEAD_DIM, axis=1)
            qrows.append(jnp.concatenate([qh, zero_t] if g // 2 == 0 else [zero_t, qh], axis=1))
        qbd = jnp.concatenate(qrows + [jnp.zeros((rpad - rows, 2 * LANES), F32)], axis=0).astype(BF16)

        imps, pcs = [], []
        rg = GROUP * t
        qpos_g = qpos[0:rg]
        for g in range(KV_HEADS):
            qg = jnp.concatenate([q_ref[0, GROUP * g + r] for r in range(GROUP)], axis=0).astype(BF16)
            sc = _dot_nt(qg, kca_ref[0, g])
            ncp = sc.shape[1]
            blk_end = lax.broadcasted_iota(jnp.int32, (1, ncp), 1) * CMP_STRIDE + (CMP_BLOCK - 1)
            pc = _softmax_masked(sc, blk_end <= qpos_g)
            pcs.append(pc.astype(BF16))
            imps.append(pc[0:t] + pc[t:2 * t] + pc[2 * t:3 * t])
        oc_sc[0:rows, :] = _dot(jnp.concatenate(pcs, axis=0), vcb_ref[0])
        oc_sc[rows:, :] = jnp.zeros((rpad - rows, 2 * LANES), F32)
        imp = jnp.concatenate(imps, axis=0)
        imp_s = None
        for part in _split3(imp):
            d = _dot(part, mmat_ref[...])
            imp_s = d if imp_s is None else imp_s + d
        ngt = KV_HEADS * t
        width = npb + LANES
        base = jnp.concatenate([imp_s, jnp.zeros((ngt, LANES), F32)], axis=1)
        j = lax.broadcasted_iota(jnp.int32, (ngt, width), 1)
        cur = jnp.right_shift(past_len + jnp.bitwise_and(lax.broadcasted_iota(jnp.int32, (ngt, width), 0), t - 1), SEL_SHIFT)
        valid = j <= cur
        forced = (j == 0) | (j == cur) | (j == cur - 1)
        forced = forced & valid
        picked = jnp.where(forced, 1.0, 0.0)
        free_picks = float(N_SEL) - jnp.sum(picked, axis=-1, keepdims=True)
        score = jnp.where(valid & jnp.logical_not(forced), base, -jnp.inf)
        jf = j.astype(F32)
        cur_min = past_len >> SEL_SHIFT
        for r in range(N_SEL - 1 - min(cur_min, 2)):
            mx = jnp.max(score, axis=-1, keepdims=True)
            idx = jnp.min(jnp.where(score == mx, jf, float(width)), axis=-1, keepdims=True)
            hit = (jf == idx) & (float(r) < free_picks)
            picked = jnp.where(hit, 1.0, picked)
            score = jnp.where(hit, -jnp.inf, score)
        pen_gt = jnp.where((picked > 0.5) & valid, 0.0, NEG)
        pen = jnp.concatenate([pen_gt[(hd // GROUP) * t:(hd // GROUP + 1) * t] for hd in range(N_HEADS)]
                              + [jnp.zeros((rpad - rows, width), F32)], axis=0)
        qbd_sc[...] = qbd
        for c in range(width // LANES):
            pen_sc[c * LANES:(c + 1) * LANES, :] = pen[:, c * LANES:(c + 1) * LANES].T.astype(BF16)
        pen_new = pen[:, npb:npb + 1]

        def padded(ref, cols):
            return jnp.concatenate([ref[0, :, cols], jnp.zeros((LANES - t, KV_WIDTH), F32)], axis=0).astype(BF16)

        kcols, vcols = slice(0, KV_WIDTH), slice(KV_WIDTH, 2 * KV_WIDTH)
        inew = lax.broadcasted_iota(jnp.int32, (1, LANES), 1)
        new_ok = (inew < t) & (past_len + inew <= qpos)

        s = jnp.where(new_ok, _dot_nt(qbd, padded(ksn_ref, kcols)) + pen_new, NEG)
        m0 = jnp.maximum(jnp.max(s, axis=-1, keepdims=True), M_INIT)
        p = jnp.exp(s - m0)
        m_sc[...] = m0
        l_sc[...] = jnp.sum(p, axis=-1, keepdims=True)
        acc_sc[...] = _dot(p.astype(BF16), padded(ksn_ref, vcols))

        wk = win_ref.shape[2]
        kw_pos = past_len - wk + lax.broadcasted_iota(jnp.int32, (1, wk), 1)
        mask1 = (kw_pos <= qpos) & (kw_pos > qpos - WINDOW) & (kw_pos >= 0)
        mask2 = new_ok & (past_len + inew > qpos - WINDOW)
        s1 = jnp.where(mask1, _dot(qbd, win_ref[0, kcols, :].astype(BF16)), -1e30)
        s2 = jnp.where(mask2, _dot_nt(qbd, padded(kwn_ref, kcols)), -1e30)
        mw = jnp.maximum(jnp.max(s1, axis=-1, keepdims=True), jnp.max(s2, axis=-1, keepdims=True))
        p1 = jnp.exp(s1 - mw) * mask1.astype(F32)
        p2 = jnp.exp(s2 - mw) * mask2.astype(F32)
        lw = jnp.sum(p1, axis=-1, keepdims=True) + jnp.sum(p2, axis=-1, keepdims=True)
        ow = _dot_nt(p1.astype(BF16), win_ref[0, vcols, :].astype(BF16)) + _dot(p2.astype(BF16), padded(kwn_ref, vcols))
        ow_sc[...] = ow / jnp.maximum(lw, 1e-30)

    bpu = PAGES_PER_UPDATE * (PAGE_SIZE // SEL_BLOCK)
    keys = PAGES_PER_UPDATE * PAGE_SIZE
    onehot = jnp.where(jnp.right_shift(lax.broadcasted_iota(jnp.int32, (bpu, keys), 1), SEL_SHIFT)
                       == lax.broadcasted_iota(jnp.int32, (bpu, keys), 0), 1.0, 0.0).astype(BF16)
    for u in range(pg // PAGES_PER_UPDATE):
        blocks = [pages[u * PAGES_PER_UPDATE + i][0] for i in range(PAGES_PER_UPDATE)]
        kt = jnp.concatenate([x[0:KV_WIDTH, :] for x in blocks], axis=1).astype(BF16)
        vt = jnp.concatenate([x[KV_WIDTH:, :] for x in blocks], axis=1).astype(BF16)
        b0 = pl.multiple_of((step * (pg // PAGES_PER_UPDATE) + u) * bpu, bpu)
        bias = lax.dot_general(pen_sc[pl.ds(b0, bpu), :], onehot, (((0,), (0,)), ((), ())), preferred_element_type=F32)
        _online_update(_dot(qbd_sc[...], kt) + bias, vt, m_sc, l_sc, acc_sc)

    @pl.when(step == pl.num_programs(1) - 1)
    def _():
        o_s = acc_sc[...] / l_sc[...]
        lo = _lo_half(t)
        gates = gates_ref[0]
        placed = []
        for hd in range(N_HEADS):
            g = hd // GROUP
            rs = slice(hd * t, (hd + 1) * t)
            cols = slice((g // 2) * LANES, (g // 2 + 1) * LANES)
            mix = (gates[:, hd:hd + 1] * oc_sc[rs, cols] + gates[:, N_HEADS + hd:N_HEADS + hd + 1] * o_s[rs, cols]
                   + gates[:, 2 * N_HEADS + hd:2 * N_HEADS + hd + 1] * ow_sc[rs, cols])
            placed.append(mix if g % 2 == hd % 2 else pltpu.roll(mix, HEAD_DIM, axis=1))
        for c in range(N_HEADS // 2):
            cols = slice(c * LANES, (c + 1) * LANES)
            o_ref[0, :, cols] = zgs_ref[0, :, cols] * jnp.where(lo, placed[2 * c], placed[2 * c + 1])


def _attn_sample(q, kca, vcb, kvs_new, kvw_new, win_state, gates, zgs, cache_sel, page_table, t, pg):
    bd, n_pages = page_table.shape
    past_len = n_pages * PAGE_SIZE
    nck = kca.shape[2]
    npb = past_len // SEL_BLOCK
    mmat = jnp.asarray(_imp_matrix(nck, npb), BF16)
    wk = win_state.shape[2]
    rpad = -(-N_HEADS * t // LANES) * LANES
    tok = lambda w: pl.BlockSpec((1, t, w), lambda b, s, pt: (0, b, 0))
    grid_spec = pltpu.PrefetchScalarGridSpec(
        num_scalar_prefetch=1,
        grid=(bd, n_pages // pg),
        in_specs=[pl.BlockSpec((1, N_HEADS, t, LANES), lambda b, s, pt: (0, 0, b, 0)),
                  pl.BlockSpec((1, KV_HEADS, nck, LANES), lambda b, s, pt: (b, 0, 0, 0)),
                  pl.BlockSpec((1, nck, KV_WIDTH), lambda b, s, pt: (b, 0, 0)),
                  tok(2 * KV_WIDTH), tok(2 * KV_WIDTH),
                  pl.BlockSpec((1, 2 * KV_WIDTH, wk), lambda b, s, pt: (b, 0, 0)),
                  tok(LANES), tok(MIX_WIDTH),
                  pl.BlockSpec(mmat.shape, lambda b, s, pt: (0, 0))] + _page_specs(pg),
        out_specs=tok(MIX_WIDTH),
        scratch_shapes=[pltpu.VMEM((rpad, 2 * LANES), BF16), pltpu.VMEM((npb + LANES, rpad), BF16),
                        pltpu.VMEM((rpad, 1), F32), pltpu.VMEM((rpad, 1), F32),
                        pltpu.VMEM((rpad, 2 * LANES), F32), pltpu.VMEM((rpad, 2 * LANES), F32),
                        pltpu.VMEM((rpad, 2 * LANES), F32)],
    )
    return pl.pallas_call(
        functools.partial(_attn_sample_kernel, pg=pg, past_len=past_len),
        grid_spec=grid_spec,
        out_shape=jax.ShapeDtypeStruct((1, bd * t, MIX_WIDTH), F32),
        compiler_params=_cparams("arbitrary", "arbitrary"),
        name="attn_sample",
    )(page_table, q, kca, vcb, kvs_new, kvw_new, win_state, gates, zgs, mmat, *([cache_sel] * pg))


def _reorder_nsa_weight(w):
    d = w.shape[0]
    n_gate = 3 * N_HEADS
    zg0 = _C_ZG + n_gate
    return jnp.concatenate([w[:, :_C_ZG], w[:, zg0:zg0 + MIX_WIDTH], w[:, _C_ZG:zg0],
                            jnp.zeros((d, LANES - n_gate), w.dtype)], axis=1).astype(BF16)


def _cmp_weights(cmp_pos_w, cmp_phi):
    ratio = CMP_BLOCK // CMP_STRIDE
    pw = cmp_pos_w.reshape(2, ratio, CMP_STRIDE, HEAD_DIM)
    tiles = [jnp.concatenate([jnp.tile(pw[s, m], (1, KV_HEADS)) for s in range(2)], axis=1) for m in range(ratio)]
    eye = jnp.eye(KV_HEADS, dtype=cmp_phi.dtype)
    z = jnp.zeros((KV_WIDTH, KV_WIDTH), cmp_phi.dtype)
    phi_bd = jnp.concatenate([jnp.concatenate([jnp.kron(eye, cmp_phi[0]), z], axis=1),
                              jnp.concatenate([z, jnp.kron(eye, cmp_phi[1])], axis=1)], axis=0).astype(BF16)
    return tiles[0], tiles[1], phi_bd


def _feature_major(cache):
    n, rows = cache.shape[:2]
    return jnp.transpose(cache, (0, 2, 3, 4, 1)).reshape(n, -1, rows)


def kernel(x_prompt, x_sample, mem_prompt, cache_mem_kv, cache_cmp_kv, cache_sel_kv, page_table, state_conv, state_win_kv, norm_g, final_norm_g, mem_norm_g, w_mem_kv, w_in_conv, conv_w, w_in_nsa, cmp_pos_w, cmp_phi, w_out):
    b, t, d = x_prompt.shape
    bd, td, _ = x_sample.shape
    n_mem = mem_prompt.shape[1]
    n_pages = page_table.shape[1]
    past_len = n_pages * PAGE_SIZE
    assert w_in_conv.shape[0] == 1 and w_in_nsa.shape[0] == 1 and w_out.shape[0] == 2
    assert CMP_BLOCK == 2 * CMP_STRIDE and td & (td - 1) == 0 and td < CMP_STRIDE

    w_conv_b = w_in_conv[0].astype(BF16)
    w_nsa_b = _reorder_nsa_weight(w_in_nsa[0])
    w_out_b = w_out.astype(BF16)
    w_mem_b = w_mem_kv.astype(BF16)
    pw0, pw1, phi_bd = _cmp_weights(cmp_pos_w[0], cmp_phi[0])
    kv6 = lambda a, n, r: a.reshape(1, n, r, 2, KV_HEADS, HEAD_DIM)

    mem_kv_p = _memory_kv(mem_prompt, mem_norm_g, w_mem_b)
    xp = x_prompt.reshape(b * t, d)
    tm = min(512, t)
    tmo_p = min(1024, t)
    mqg, mix, conv_p = _proj_conv(xp, norm_g[0], w_conv_b, conv_w[0], seg=t, tm=tm, mix_dtype=BF16)
    x1 = _layer_out(xp, mqg, mem_kv_p[0], mix, w_out_b[0], tm=tmo_p, seg=t)
    tabs = _rope_tables(jnp.arange(t, dtype=jnp.int32))
    tq = min(256, t)
    (mqg, q, kvc_p, kvs_p, kvw_p, gates, zgs, ksa, kwa, vsb, vwb) = _proj_nsa(
        x1.reshape(b, t, d), norm_g[1], w_nsa_b, tabs, tm=tm, prompt=True)
    kca, vcb = _cmp_prompt(kvc_p, pw0, pw1, phi_bd)
    mix = _attn_prompt(q, kca, vcb, ksa, vsb, kwa, vwb, gates, zgs, tq=tq)
    y_prompt = _layer_out(x1, mqg.reshape(b * t, MQG_WIDTH), mem_kv_p[1], mix.reshape(b * t, MIX_WIDTH), w_out_b[1],
                          tm=tmo_p, seg=t, final_g=final_norm_g).reshape(b, t, d)
    w_keep_p = min(WINDOW, t)

    ms = bd * td
    xs = x_sample.reshape(ms, d)
    st = state_conv[0]
    zrow = jnp.zeros((bd, td - 1, MIX_WIDTH), F32)
    s1 = jnp.concatenate([st[:, 1:2], zrow], axis=1).reshape(ms, MIX_WIDTH)
    s2 = jnp.concatenate([st, zrow[:, 1:]], axis=1).reshape(ms, MIX_WIDTH)
    mqg, mix, u_s = _proj_conv(xs, norm_g[0], w_conv_b, conv_w[0], seg=td, tm=ms, state=(s1, s2))
    conv_s = u_s.reshape(bd, td, MIX_WIDTH)[:, td - 2:]
    tmo = 8 * td
    mem_s = _feature_major(cache_mem_kv.reshape((-1,) + cache_mem_kv.shape[2:]))
    x1s = _layer_out(xs, mqg, mem_s, mix, w_out_b[0], tm=tmo, seg=td, feature_major=True)
    tabs_s = _rope_tables(jnp.tile(past_len + jnp.arange(td, dtype=jnp.int32), bd))
    (mqg, q_s, kvc_s, kvs_s, kvw_s, gates_s, zgs_s) = _proj_nsa(
        x1s.reshape(1, ms, d), norm_g[1], w_nsa_b, tabs_s, tm=ms, prompt=False)
    pg = min(64, n_pages)
    kca_s, vcb_s = _cmp_sample(_feature_major(cache_cmp_kv[0]), page_table, pw0, pw1, phi_bd, pg)
    mix_s = _attn_sample(q_s, kca_s, vcb_s, kvs_s, kvw_s, _feature_major(state_win_kv[0]), gates_s, zgs_s,
                         _feature_major(cache_sel_kv[0]), page_table, td, pg)
    y_sample = _layer_out(x1s, mqg.reshape(ms, MQG_WIDTH), mem_s, mix_s.reshape(ms, MIX_WIDTH), w_out_b[1], tm=tmo, seg=td,
                          final_g=final_norm_g, feature_major=True, memkv_offset=bd).reshape(bd, td, d)
    w_keep = state_win_kv.shape[2]
    win_s = jnp.concatenate([state_win_kv[0], kvw_s.reshape(bd, td, 2, KV_HEADS, HEAD_DIM)], axis=1)[:, -w_keep:]

    return (y_prompt, y_sample, conv_p[None],
            kv6(kvc_p, b, t), kv6(kvs_p, b, t), kv6(kvw_p[:, t - w_keep_p:], b, w_keep_p),
            mem_kv_p.reshape(2, b, n_mem, 2, MEM_WIDTH // HEAD_DIM, HEAD_DIM),
            conv_s[None], kv6(kvc_s, bd, td), kv6(kvs_s, bd, td), win_s[None])
```
